```python
import jax, jax.numpy as jnp
from jax import lax
import numpy as np

D_MODEL = 4096
BATCH = 4
SEQ = 4096
DEPTH = 1
DEC_BATCH = 16
DEC_SEQ = 32
PAST_LEN = 2048

CHUNK = 64
GDN_HEADS = D_MODEL // 256
GDN_DK = 128
GDN_DV = 128
GDN_KEY = GDN_HEADS * GDN_DK
GDN_VAL = GDN_HEADS * GDN_DV
GDN_CONV_CH = 2 * GDN_KEY + GDN_VAL
CONV_W = 4
ATT_HEADS = D_MODEL // 256
ATT_KV_HEADS = ATT_HEADS // 4
ATT_HD = 128
ATT_Q = ATT_HEADS * ATT_HD
ATT_KV = ATT_KV_HEADS * ATT_HD
IDX_HEADS = D_MODEL // 128
IDX_HD = 128
TOPK_KEYS = 256
Q_BLOCK = 128
ROPE_THETA = 500000.0
N_BRANCH = 2
N_GROUPS = 4
EXP_PER_GROUP = 8
N_EXPERTS = N_GROUPS * EXP_PER_GROUP
TOPK_EXP = 2
D_EXPERT = D_MODEL // 4
MOE_BLOCK = 128
ALPHA = (2.0 * DEPTH) ** 0.25
BETA_INIT = (8.0 * DEPTH) ** -0.25
LN_EPS = 1e-5
RMS_EPS = 1e-6
IN_SIZES = (GDN_CONV_CH, GDN_HEADS, GDN_HEADS, GDN_VAL, ATT_Q, ATT_KV, ATT_KV, IDX_HEADS * IDX_HD, IDX_HD, IDX_HEADS, N_BRANCH * D_MODEL)
IN_COLS = sum(IN_SIZES)

kernel_name = 'streaming_gdn_dsa_hmoe_layer'


def _split_cols(t, sizes):
    outs, off = [], 0
    for s in sizes:
        outs.append(t[..., off:off + s])
        off += s
    return outs


def _layer_norm(x, g, b):
    xf = x.astype(jnp.float32)
    mu = jnp.mean(xf, -1, keepdims=True)
    var = jnp.mean(jnp.square(xf - mu), -1, keepdims=True)
    return ((xf - mu) * lax.rsqrt(var + LN_EPS) * g + b).astype(x.dtype)


def _l2norm(x):
    return x * lax.rsqrt(jnp.sum(x * x, -1, keepdims=True) + RMS_EPS)


def _partial_rope(x, pos):
    rot = x.shape[-1] // 4
    half = rot // 2
    inv = jnp.power(ROPE_THETA, -jnp.arange(half, dtype=jnp.float32) * (2.0 / rot))
    ang = pos.astype(jnp.float32)[:, None] * inv[None, :]
    cos = jnp.cos(ang)[None, :, None, :]
    sin = jnp.sin(ang)[None, :, None, :]
    xf = x.astype(jnp.float32)
    x1, x2 = xf[..., :half], xf[..., half:rot]
    return jnp.concatenate([x1 * cos - x2 * sin, x2 * cos + x1 * sin, xf[..., rot:]], -1).astype(x.dtype)


def _short_conv(u, conv_state, w):
    L = u.shape[1]
    xp = jnp.concatenate([conv_state.astype(u.dtype), u], axis=1)
    y = xp[:, 0:L] * w[0]
    for j in range(1, CONV_W):
        y = y + xp[:, j:j + L] * w[j]
    return jax.nn.silu(y), xp[:, L:]


def _gated_delta_chunked(q, k, v, g, beta, s0):
    B, L, H, dk = q.shape
    dv = v.shape[-1]
    C = min(CHUNK, L)
    N = L // C
    f32 = jnp.float32

    def heads_chunks(t):
        t = t.astype(f32).reshape((B, N, C) + t.shape[2:])
        return jnp.moveaxis(t, 3, 1)

    q = _l2norm(heads_chunks(q)) * (dk ** -0.5)
    k = _l2norm(heads_chunks(k))
    v = heads_chunks(v)
    g = heads_chunks(g)
    beta = heads_chunks(beta)
    gc = jnp.cumsum(g, axis=-1)
    causal = jnp.tril(jnp.ones((C, C), dtype=bool))
    strict = jnp.tril(jnp.ones((C, C), dtype=bool), -1)
    decay = jnp.exp(jnp.where(causal, gc[..., :, None] - gc[..., None, :], -jnp.inf))
    kb = k * beta[..., None]
    lmat = jnp.where(strict, jnp.einsum('bhncd,bhnsd->bhncs', kb, k) * decay, 0.0)
    rhs = jnp.concatenate([v * beta[..., None], kb * jnp.exp(gc)[..., None]], -1)
    sol = lax.linalg.triangular_solve(lmat, rhs, left_side=True, lower=True, unit_diagonal=True)
    u, w = sol[..., :dv], sol[..., dv:]
    intra = jnp.einsum('bhncd,bhnsd->bhncs', q, k) * decay
    qd = q * jnp.exp(gc)[..., None]
    kd = k * jnp.exp(gc[..., -1:] - gc)[..., None]
    glast = jnp.exp(gc[..., -1])

    def step(S, xs):
        u_n, w_n, a_n, qd_n, kd_n, gl_n = xs
        v_new = u_n - jnp.einsum('bhcd,bhde->bhce', w_n, S)
        o_n = jnp.einsum('bhcd,bhde->bhce', qd_n, S) + jnp.einsum('bhcs,bhse->bhce', a_n, v_new)
        S = S * gl_n[..., None, None] + jnp.einsum('bhcd,bhce->bhde', kd_n, v_new)
        return S, o_n

    xs = tuple(jnp.moveaxis(t, 2, 0) for t in (u, w, intra, qd, kd, glast))
    s_fin, o = lax.scan(step, s0.astype(f32), xs)
    o = jnp.transpose(o, (1, 0, 3, 2, 4)).reshape(B, L, H, dv)
    return o, s_fin


def _dsa_attention(q, iq, iw, k_all, v_all, ik_all, pos0):
    B, Lq, H, hd = q.shape
    S = k_all.shape[1]
    kvh = k_all.shape[2]
    rep = H // kvh
    hi, di = iq.shape[2], iq.shape[3]
    n_sel = min(TOPK_KEYS, S // 4)
    qb = min(Q_BLOCK, Lq)
    nb = Lq // qb
    key_chunk = jnp.arange(S) // CHUNK
    q_chunk = ((pos0 + jnp.arange(Lq)) // CHUNK).reshape(nb, qb)

    def blocks(t):
        return jnp.moveaxis(t.reshape((B, nb, qb) + t.shape[2:]), 1, 0)

    def attend_block(args):
        q_b, iq_b, iw_b, qc_b = args
        rel = jax.nn.relu(jnp.einsum('bqhd,bsd->bqhs', iq_b, ik_all).astype(jnp.float32) * (di ** -0.5))
        score = jnp.einsum('bqh,bqhs->bqs', iw_b.astype(jnp.float32) * (hi ** -0.5), rel)
        adm = key_chunk[None, :] <= qc_b[:, None]
        score = jnp.where(adm[None], score, -jnp.inf)
        _, idx = lax.top_k(score, n_sel)
        valid = key_chunk[idx] <= qc_b[None, :, None]
        kg = jax.vmap(lambda kk, ii: kk[ii])(k_all, idx)
        vg = jax.vmap(lambda vv, ii: vv[ii])(v_all, idx)
        qg = q_b.reshape(B, qb, kvh, rep, hd)
        logits = jnp.einsum('bqgrd,bqkgd->bqgrk', qg, kg).astype(jnp.float32) * (hd ** -0.5)
        logits = jnp.where(valid[:, :, None, None, :], logits, -jnp.inf)
        prob = jax.nn.softmax(logits, axis=-1).astype(vg.dtype)
        out = jnp.einsum('bqgrk,bqkgd->bqgrd', prob, vg)
        return out.reshape(B, qb, H * hd)

    outs = lax.map(attend_block, (blocks(q), blocks(iq), blocks(iw), q_chunk))
    return jnp.moveaxis(outs, 0, 1).reshape(B, Lq, H * hd)


def _hier_moe(t, w_grp, b_grp, w_rtr, b_rtr, w_up, w_down):
    T, D = t.shape
    grp_logits = (t @ w_grp + b_grp).astype(jnp.float32)
    grp_prob = jax.nn.softmax(grp_logits, axis=-1)
    g_sel = jnp.argmax(grp_logits, axis=-1)
    p_grp = jnp.take_along_axis(grp_prob, g_sel[:, None], axis=-1)
    exp_logits = (t @ w_rtr + b_rtr).astype(jnp.float32).reshape(T, N_GROUPS, EXP_PER_GROUP)
    in_grp = jnp.take_along_axis(exp_logits, g_sel[:, None, None], axis=1)[:, 0]
    w_top, e_top = lax.top_k(jax.nn.softmax(in_grp, axis=-1), TOPK_EXP)
    w_top = w_top / jnp.sum(w_top, -1, keepdims=True) * p_grp
    expert = (g_sel[:, None] * EXP_PER_GROUP + e_top).reshape(-1)
    tok = jnp.repeat(jnp.arange(T, dtype=jnp.int32), TOPK_EXP)
    wt = w_top.reshape(-1)
    A = T * TOPK_EXP
    order = jnp.argsort(expert)
    e_s = expert[order]
    counts = jnp.bincount(expert, length=N_EXPERTS)
    padded = ((counts + MOE_BLOCK - 1) // MOE_BLOCK) * MOE_BLOCK
    start = jnp.cumsum(counts) - counts
    pend = jnp.cumsum(padded)
    pstart = pend - padded
    dest = pstart[e_s] + (jnp.arange(A) - start[e_s])
    P = ((A + MOE_BLOCK - 1) // MOE_BLOCK) * MOE_BLOCK + N_EXPERTS * MOE_BLOCK
    nblk = P // MOE_BLOCK
    slot_tok = jnp.full((P,), T, dtype=jnp.int32).at[dest].set(tok[order])
    slot_w = jnp.zeros((P,), jnp.float32).at[dest].set(wt[order])
    blk_e = jnp.clip(jnp.searchsorted(pend, jnp.arange(nblk) * MOE_BLOCK, side='right'), 0, N_EXPERTS - 1)
    t_pad = jnp.concatenate([t, jnp.zeros((1, D), t.dtype)], axis=0)
    xin = t_pad[slot_tok].reshape(nblk, MOE_BLOCK, D)

    def expert_block(args):
        xb, e = args
        gu = xb @ w_up[e]
        gate, up = gu[:, :D_EXPERT], gu[:, D_EXPERT:]
        return (jax.nn.silu(gate) * up) @ w_down[e]

    yb = lax.map(expert_block, (xin, blk_e)).reshape(P, D)
    out = jax.ops.segment_sum(yb * slot_w[:, None].astype(yb.dtype), slot_tok, num_segments=T + 1)[:T]
    return out.astype(t.dtype)


def _layer(x, c, conv_state, gdn_state, past_k, past_v, past_ik, p):
    B, L, D = x.shape
    pos0 = past_k.shape[1]
    pos = pos0 + jnp.arange(L)
    mod = (jax.nn.silu(c) @ p['w_ada'] + p['b_ada'])[:, None, :]
    sh1, sc1, g1, sh2, sc2, g2 = jnp.split(mod, 6, axis=-1)
    h = x * (1.0 + sc1) + sh1
    proj = h @ p['w_in']
    qkv_a, a_a, b_a, z_a, q_b, k_b, v_b, iq_b, ik_b, iw_b, gate_logits = _split_cols(proj, IN_SIZES)
    qkv_a, new_conv = _short_conv(qkv_a, conv_state, p['conv_w'])
    qa = qkv_a[..., :GDN_KEY].reshape(B, L, GDN_HEADS, GDN_DK)
    ka = qkv_a[..., GDN_KEY:2 * GDN_KEY].reshape(B, L, GDN_HEADS, GDN_DK)
    va = qkv_a[..., 2 * GDN_KEY:].reshape(B, L, GDN_HEADS, GDN_DV)
    g = -jnp.exp(p['a_log'].astype(jnp.float32)) * jax.nn.softplus(a_a.astype(jnp.float32) + p['dt_bias'].astype(jnp.float32))
    beta = jax.nn.sigmoid(b_a.astype(jnp.float32))
    oa, new_gdn = _gated_delta_chunked(qa, ka, va, g, beta, gdn_state)
    za = z_a.reshape(B, L, GDN_HEADS, GDN_DV).astype(jnp.float32)
    oa = oa * lax.rsqrt(jnp.mean(oa * oa, -1, keepdims=True) + RMS_EPS) * p['gdn_norm_w'] * jax.nn.silu(za)
    oa = oa.reshape(B, L, GDN_VAL).astype(x.dtype)
    qb = _partial_rope(q_b.reshape(B, L, ATT_HEADS, ATT_HD), pos)
    kb = _partial_rope(k_b.reshape(B, L, ATT_KV_HEADS, ATT_HD), pos)
    vb = v_b.reshape(B, L, ATT_KV_HEADS, ATT_HD)
    iq = _partial_rope(iq_b.reshape(B, L, IDX_HEADS, IDX_HD), pos)
    ik = _partial_rope(ik_b[:, :, None, :], pos)[:, :, 0]
    k_all = jnp.concatenate([past_k.astype(kb.dtype), kb], axis=1)
    v_all = jnp.concatenate([past_v.astype(vb.dtype), vb], axis=1)
    ik_all = jnp.concatenate([past_ik.astype(ik.dtype), ik], axis=1)
    ob = _dsa_attention(qb, iq, iw_b, k_all, v_all, ik_all, pos0)
    gates = jax.nn.sigmoid(gate_logits)
    mix = (gates[..., :D] * (oa @ p['w_br_a']) + gates[..., D:] * (ob @ p['w_br_b'])) @ p['w_o']
    x1 = _layer_norm(ALPHA * x + g1 * mix, p['ln1_g'], p['ln1_b'])
    h2 = x1 * (1.0 + sc2) + sh2
    f = _hier_moe(h2.reshape(B * L, D), p['w_grp'], p['b_grp'], p['w_rtr'], p['b_rtr'], p['w_up'], p['w_down']).reshape(B, L, D)
    y = _layer_norm(ALPHA * x1 + g2 * f, p['ln2_g'], p['ln2_b'])
    return y, new_conv, new_gdn.astype(gdn_state.dtype), kb, vb, ik


def _normal(k, shape, scale):
    return jax.random.normal(k, shape, jnp.float32) * scale


def setup_inputs(seed: int = 0) -> dict:
    key = jax.random.key(seed)
    ks = jax.random.split(key, 32)
    Dd = DEPTH
    dt = jnp.exp(jax.random.uniform(ks[13], (Dd, GDN_HEADS), jnp.float32, np.log(1e-3), np.log(1e-1)))
    return {
        'x_prompt': _normal(ks[0], (BATCH, SEQ, D_MODEL), 1.0),
        'x_sample': _normal(ks[1], (DEC_BATCH, DEC_SEQ, D_MODEL), 1.0),
        'c_prompt': _normal(ks[2], (BATCH, D_MODEL), 1.0),
        'c_sample': _normal(ks[3], (DEC_BATCH, D_MODEL), 1.0),
        'state_conv': _normal(ks[4], (Dd, DEC_BATCH, CONV_W - 1, GDN_CONV_CH), 1.0),
        'state_gdn': _normal(ks[5], (Dd, DEC_BATCH, GDN_HEADS, GDN_DK, GDN_DV), 0.1),
        'cache_k': _normal(ks[6], (Dd, DEC_BATCH, PAST_LEN, ATT_KV_HEADS, ATT_HD), 1.0),
        'cache_v': _normal(ks[7], (Dd, DEC_BATCH, PAST_LEN, ATT_KV_HEADS, ATT_HD), 1.0),
        'cache_idx_k': _normal(ks[8], (Dd, DEC_BATCH, PAST_LEN, IDX_HD), 1.0),
        'w_ada': _normal(ks[9], (Dd, D_MODEL, 6 * D_MODEL), D_MODEL ** -0.5),
        'b_ada': _normal(ks[10], (Dd, 6 * D_MODEL), 0.02),
        'w_in': _normal(ks[11], (Dd, D_MODEL, IN_COLS), D_MODEL ** -0.5),
        'conv_w': _normal(ks[12], (Dd, CONV_W, GDN_CONV_CH), CONV_W ** -0.5),
        'a_log': jnp.log(jax.random.uniform(ks[14], (Dd, GDN_HEADS), jnp.float32, 1.0, 16.0)),
        'dt_bias': jnp.log(jnp.expm1(dt)),
        'gdn_norm_w': 1.0 + _normal(ks[15], (Dd, GDN_DV), 0.02),
        'w_br_a': _normal(ks[16], (Dd, GDN_VAL, D_MODEL), GDN_VAL ** -0.5),
        'w_br_b': _normal(ks[17], (Dd, ATT_Q, D_MODEL), ATT_Q ** -0.5),
        'w_o': _normal(ks[18], (Dd, D_MODEL, D_MODEL), BETA_INIT * D_MODEL ** -0.5),
        'ln1_g': 1.0 + _normal(ks[19], (Dd, D_MODEL), 0.02),
        'ln1_b': _normal(ks[20], (Dd, D_MODEL), 0.02),
        'ln2_g': 1.0 + _normal(ks[21], (Dd, D_MODEL), 0.02),
        'ln2_b': _normal(ks[22], (Dd, D_MODEL), 0.02),
        'w_grp': _normal(ks[23], (Dd, D_MODEL, N_GROUPS), D_MODEL ** -0.5),
        'b_grp': _normal(ks[24], (Dd, N_GROUPS), 0.01),
        'w_rtr': _normal(ks[25], (Dd, D_MODEL, N_EXPERTS), D_MODEL ** -0.5),
        'b_rtr': _normal(ks[26], (Dd, N_EXPERTS), 0.01),
        'w_up': _normal(ks[27], (Dd, N_EXPERTS, D_MODEL, 2 * D_EXPERT), D_MODEL ** -0.5),
        'w_down': _normal(ks[28], (Dd, N_EXPERTS, D_EXPERT, D_MODEL), BETA_INIT * D_EXPERT ** -0.5),
    }


def reference(x_prompt, x_sample, c_prompt, c_sample, state_conv, state_gdn, cache_k, cache_v, cache_idx_k,
              w_ada, b_ada, w_in, conv_w, a_log, dt_bias, gdn_norm_w, w_br_a, w_br_b, w_o,
              ln1_g, ln1_b, ln2_g, ln2_b, w_grp, b_grp, w_rtr, b_rtr, w_up, w_down):
    stacked = dict(w_ada=w_ada, b_ada=b_ada, w_in=w_in, conv_w=conv_w, a_log=a_log, dt_bias=dt_bias,
                   gdn_norm_w=gdn_norm_w, w_br_a=w_br_a, w_br_b=w_br_b, w_o=w_o, ln1_g=ln1_g, ln1_b=ln1_b,
                   ln2_g=ln2_g, ln2_b=ln2_b, w_grp=w_grp, b_grp=b_grp, w_rtr=w_rtr, b_rtr=b_rtr,
                   w_up=w_up, w_down=w_down)
    yp, ys = x_prompt, x_sample
    conv_p, gdn_p, k_p, v_p, ik_p = [], [], [], [], []
    conv_s, gdn_s, k_s, v_s, ik_s = [], [], [], [], []
    for l in range(DEPTH):
        p = {name: arr[l] for name, arr in stacked.items()}
        Bp = yp.shape[0]
        zc = jnp.zeros((Bp, CONV_W - 1, GDN_CONV_CH), yp.dtype)
        zs = jnp.zeros((Bp, GDN_HEADS, GDN_DK, GDN_DV), yp.dtype)
        zk = jnp.zeros((Bp, 0, ATT_KV_HEADS, ATT_HD), yp.dtype)
        zik = jnp.zeros((Bp, 0, IDX_HD), yp.dtype)
        yp, c1, g1, k1, v1, i1 = _layer(yp, c_prompt, zc, zs, zk, zk, zik, p)
        ys, c2, g2, k2, v2, i2 = _layer(ys, c_sample, state_conv[l], state_gdn[l], cache_k[l], cache_v[l], cache_idx_k[l], p)
        conv_p.append(c1); gdn_p.append(g1); k_p.append(k1); v_p.append(v1); ik_p.append(i1)
        conv_s.append(c2); gdn_s.append(g2); k_s.append(k2); v_s.append(v2); ik_s.append(i2)
    new_conv_p = jnp.stack(conv_p)
    new_gdn_p = jnp.stack(gdn_p)
    new_k_p = jnp.stack(k_p)
    new_v_p = jnp.stack(v_p)
    new_ik_p = jnp.stack(ik_p)
    new_conv_s = jnp.stack(conv_s)
    new_gdn_s = jnp.stack(gdn_s)
    new_k_s = jnp.stack(k_s)
    new_v_s = jnp.stack(v_s)
    new_ik_s = jnp.stack(ik_s)
    return (yp, ys, new_conv_p, new_gdn_p, new_k_p, new_v_p, new_ik_p, new_conv_s, new_gdn_s, new_k_s, new_v_s, new_ik_s)
```

```python
import functools

import jax
import jax.numpy as jnp
from jax import lax
from jax.experimental import pallas as pl
from jax.experimental.pallas import tpu as pltpu

F32 = jnp.float32
BF16 = jnp.bfloat16

D_MODEL = 4096
CHUNK = 64
GDN_HEADS = D_MODEL // 256
GDN_DK = 128
GDN_DV = 128
GDN_KEY = GDN_HEADS * GDN_DK
GDN_VAL = GDN_HEADS * GDN_DV
GDN_CONV_CH = 2 * GDN_KEY + GDN_VAL
CONV_W = 4
ATT_HEADS = D_MODEL // 256
ATT_KV_HEADS = ATT_HEADS // 4
ATT_REP = ATT_HEADS // ATT_KV_HEADS
ATT_HD = 128
ATT_Q = ATT_HEADS * ATT_HD
ATT_KV = ATT_KV_HEADS * ATT_HD
IDX_HEADS = D_MODEL // 128
IDX_HD = 128
TOPK_KEYS = 256
Q_BLOCK = 128
ROPE_THETA = 500000.0
ROPE_ROT = ATT_HD // 4
ROPE_HALF = ROPE_ROT // 2
N_GROUPS = 4
EXP_PER_GROUP = 8
N_EXPERTS = N_GROUPS * EXP_PER_GROUP
D_EXPERT = D_MODEL // 4
DEPTH = 1
ALPHA = (2.0 * DEPTH) ** 0.25
LN_EPS = 1e-5
RMS_EPS = 1e-6

C_QKV = 0
C_Z = C_QKV + GDN_CONV_CH
C_IQ = C_Z + GDN_VAL
C_Q = C_IQ + IDX_HEADS * IDX_HD
C_K = C_Q + ATT_Q
C_V = C_K + ATT_KV
C_IK = C_V + ATT_KV
C_SM = C_IK + IDX_HD
SM_W = 384
C_GA = C_SM + SM_W
C_GB = C_GA + D_MODEL
NP = C_GB + D_MODEL
SM_A, SM_B, SM_IW = 0, GDN_HEADS, 2 * GDN_HEADS

LANE = 128
NEG_BIG = -1e30
VMEM_LIMIT = 56 * 1024 * 1024

TN_IN = 512
TM_IN = 512
TM_MIX = 256
TN_MIX = 512
TK_ATT = 512
FC_MOE = 256


def _cparams(sem):
    return pltpu.CompilerParams(dimension_semantics=sem, vmem_limit_bytes=VMEM_LIMIT)


def _dot(a, b):
    return jnp.dot(a, b, preferred_element_type=F32)


def _dot_nt(a, b):
    return lax.dot_general(a, b, (((1,), (1,)), ((), ())), preferred_element_type=F32)


def _dot_tn(a, b):
    return lax.dot_general(a, b, (((0,), (0,)), ((), ())), preferred_element_type=F32)


def _split_bf16(a):
    hi = a.astype(BF16)
    lo = (a - hi.astype(F32)).astype(BF16)
    return hi, lo


def _mm3(a, b, dot=_dot):
    ah, al = _split_bf16(a)
    bh, bl = _split_bf16(b)
    return dot(ah, bh) + (dot(ah, bl) + dot(al, bh))


def _mm1(a, b):
    return _dot(a.astype(BF16), b.astype(BF16))


def _mm1_nt(a, b):
    return _dot_nt(a.astype(BF16), b.astype(BF16))


def _sigmoid(x):
    return 1.0 / (1.0 + jnp.exp(-x))


def _silu(x):
    return x * _sigmoid(x)


def _softplus(x):
    return jnp.maximum(x, 0.0) + jnp.log(1.0 + jnp.exp(-jnp.abs(x)))


def _ada_kernel(c_ref, w_ref, b_ref, o_ref):
    s = _silu(c_ref[...])
    o_ref[...] = _dot(s.astype(BF16), w_ref[...].astype(BF16)) + b_ref[...]


def _ada(c_all, w_ada, b_ada):
    R, D = c_all.shape
    N = w_ada.shape[1]
    tn = 512
    return pl.pallas_call(
        _ada_kernel,
        grid=(N // tn,),
        in_specs=[
            pl.BlockSpec((R, D), lambda j: (0, 0)),
            pl.BlockSpec((D, tn), lambda j: (0, j)),
            pl.BlockSpec((1, tn), lambda j: (0, j)),
        ],
        out_specs=pl.BlockSpec((R, tn), lambda j: (0, j)),
        out_shape=jax.ShapeDtypeStruct((R, N), F32),
        compiler_params=_cparams(("parallel",)),
        name="ada",
    )(c_all, w_ada, b_ada.reshape(1, N))


def _inproj_kernel(x_ref, sc_ref, sh_ref, w_ref, o_ref, h_ref):
    @pl.when(pl.program_id(1) == 0)
    def _():
        h = x_ref[...] * (1.0 + sc_ref[...]) + sh_ref[...]
        h_ref[...] = h.reshape(h_ref.shape).astype(BF16)

    o_ref[...] = _dot(h_ref[...], w_ref[...])


def _row_tiling(B, L, tm):
    if L >= tm:
        return 1, tm
    bb = max(1, min(B, tm // L))
    while B % bb:
        bb -= 1
    return bb, L


def _inproj(x3, mod3, row0, w_perm):
    B, L, D = x3.shape
    bb, tl = _row_tiling(B, L, TM_IN)
    nl = L // tl
    rows = bb * tl
    mrow = row0 // bb
    return pl.pallas_call(
        _inproj_kernel,
        grid=((B // bb) * nl, NP // TN_IN),
        in_specs=[
            pl.BlockSpec((bb, tl, D), lambda i, j: (i // nl, i % nl, 0)),
            pl.BlockSpec((bb, 1, D), lambda i, j: (mrow + i // nl, 0, 1)),
            pl.BlockSpec((bb, 1, D), lambda i, j: (mrow + i // nl, 0, 0)),
            pl.BlockSpec((D, TN_IN), lambda i, j: (0, j)),
        ],
        out_specs=pl.BlockSpec((rows, TN_IN), lambda i, j: (i, j)),
        out_shape=jax.ShapeDtypeStruct((B * L, NP), F32),
        scratch_shapes=[pltpu.VMEM((rows, D), BF16)],
        compiler_params=_cparams(("parallel", "arbitrary")),
        name="inproj",
    )(x3, mod3, mod3, w_perm)


def _gdn_kernel(qkv_ref, z_ref, sm_ref, abt_ref, cst_ref, cw_ref, alr_ref, dtr_ref, alc_ref, dtc_ref,
                nw_ref, s0_ref, o_ref, sfin_ref, s_ref, ext_ref, *, C, nchunks):
    n = pl.program_id(1)
    H = GDN_HEADS
    PADR = 8

    @pl.when(n == 0)
    def _():
        s_ref[...] = s0_ref[0]
        ext_ref[PADR - (CONV_W - 1):PADR, :] = cst_ref[0]

    ext_ref[PADR:PADR + C, :] = qkv_ref[...]
    cw = cw_ref[...]
    y = ext_ref[PADR - 3:PADR - 3 + C, :] * cw[0:1]
    for jw in range(1, CONV_W):
        y = y + ext_ref[PADR - 3 + jw:PADR - 3 + jw + C, :] * cw[jw:jw + 1]
    y = _silu(y)
    hist = ext_ref[PADR + C - (CONV_W - 1):PADR + C, :]
    ext_ref[PADR - (CONV_W - 1):PADR, :] = hist

    sm = sm_ref[...]
    g_col = -jnp.exp(alr_ref[...]) * _softplus(sm[:, SM_A:SM_A + H] + dtr_ref[...])
    beta_col = _sigmoid(sm[:, SM_B:SM_B + H])
    abt = abt_ref[0]
    g_row = -jnp.exp(alc_ref[...]) * _softplus(abt[0:H] + dtc_ref[...])

    ii = lax.broadcasted_iota(jnp.int32, (C, C), 0)
    jj = lax.broadcasted_iota(jnp.int32, (C, C), 1)
    causal = jj <= ii
    strict = jj < ii
    upper = ii <= jj
    nlev = C.bit_length() - 1
    eye = jnp.where(ii == jj, 1.0, 0.0)
    lev_masks = [
        (jnp.right_shift(ii, lev + 1) == jnp.right_shift(jj, lev + 1))
        & (jnp.bitwise_and(jnp.right_shift(ii, lev), 1) == 1)
        & (jnp.bitwise_and(jnp.right_shift(jj, lev), 1) == 0)
        for lev in range(nlev)
    ]
    nw = nw_ref[...]
    z = z_ref[...]

    for h in range(H):
        qh = y[:, h * GDN_DK:(h + 1) * GDN_DK]
        kh = y[:, GDN_KEY + h * GDN_DK:GDN_KEY + (h + 1) * GDN_DK]
        vh = y[:, 2 * GDN_KEY + h * GDN_DV:2 * GDN_KEY + (h + 1) * GDN_DV]
        q = qh * lax.rsqrt(jnp.sum(qh * qh, -1, keepdims=True) + RMS_EPS) * (GDN_DK ** -0.5)
        k = kh * lax.rsqrt(jnp.sum(kh * kh, -1, keepdims=True) + RMS_EPS)
        gr = g_row[h:h + 1, :]
        gcl = g_col[:, h:h + 1]
        beta = beta_col[:, h:h + 1]
        gc_col = jnp.sum(jnp.where(causal, gr, 0.0), axis=1, keepdims=True)
        gc_row = jnp.sum(jnp.where(upper, gcl, 0.0), axis=0, keepdims=True)
        decay = jnp.where(causal, jnp.exp(gc_col - gc_row), 0.0)
        egc = jnp.exp(gc_col)
        kb = k * beta
        lmat = jnp.where(strict, _mm3(kb, k, _dot_nt) * decay, 0.0)
        intra = _mm1_nt(q, k) * decay
        x = jnp.concatenate([vh * beta, kb * egc], axis=-1)
        tinv = eye
        for lev in range(nlev):
            b_lev = jnp.where(lev_masks[lev], lmat, 0.0)
            tinv = tinv - _mm3(tinv, _mm3(b_lev, tinv))
        x = _mm3(tinv, x)
        u = x[:, :GDN_DV]
        w = x[:, GDN_DV:]
        s = s_ref[h]
        v_new = u - _mm3(w, s)
        o = _mm1(q * egc, s) + _mm1(intra, v_new)
        gc_last = gc_col[C - 1:C, :]
        kd = k * jnp.exp(gc_last - gc_col)
        s_ref[h] = s * jnp.exp(gc_last) + _mm3(kd, v_new, _dot_tn)
        zh = z[:, h * GDN_DV:(h + 1) * GDN_DV]
        on = o * lax.rsqrt(jnp.mean(o * o, -1, keepdims=True) + RMS_EPS) * nw * _silu(zh)
        o_ref[:, h * GDN_DV:(h + 1) * GDN_DV] = on.astype(o_ref.dtype)

    @pl.when(n == nchunks - 1)
    def _():
        sfin_ref[0] = s_ref[...]


def _gdn(proj, B, L, conv_state, gdn_state, conv_w, a_log, dt_bias, norm_w):
    C = min(CHUNK, L)
    N = L // C
    H = GDN_HEADS
    sm = proj[:, C_SM:C_SM + 2 * H]
    abt = sm.reshape(B * N, C, 2 * H).transpose(0, 2, 1)
    kern = functools.partial(_gdn_kernel, C=C, nchunks=N)
    row = lambda b, n: (b * N + n)
    return pl.pallas_call(
        kern,
        grid=(B, N),
        in_specs=[
            pl.BlockSpec((C, GDN_CONV_CH), lambda b, n: (b * N + n, C_QKV // GDN_CONV_CH)),
            pl.BlockSpec((C, GDN_VAL), lambda b, n: (b * N + n, C_Z // GDN_VAL)),
            pl.BlockSpec((C, LANE), lambda b, n: (b * N + n, C_SM // LANE)),
            pl.BlockSpec((1, 2 * H, C), lambda b, n: (b * N + n, 0, 0)),
            pl.BlockSpec((1, CONV_W - 1, GDN_CONV_CH), lambda b, n: (b, 0, 0)),
            pl.BlockSpec((CONV_W, GDN_CONV_CH), lambda b, n: (0, 0)),
            pl.BlockSpec((1, H), lambda b, n: (0, 0)),
            pl.BlockSpec((1, H), lambda b, n: (0, 0)),
            pl.BlockSpec((H, 1), lambda b, n: (0, 0)),
            pl.BlockSpec((H, 1), lambda b, n: (0, 0)),
            pl.BlockSpec((1, GDN_DV), lambda b, n: (0, 0)),
            pl.BlockSpec((1, H, GDN_DK, GDN_DV), lambda b, n: (b, 0, 0, 0)),
        ],
        out_specs=[
            pl.BlockSpec((C, GDN_VAL), lambda b, n: (b * N + n, 0)),
            pl.BlockSpec((1, H, GDN_DK, GDN_DV), lambda b, n: (b, 0, 0, 0)),
        ],
        out_shape=[
            jax.ShapeDtypeStruct((B * L, GDN_VAL), BF16),
            jax.ShapeDtypeStruct((B, H, GDN_DK, GDN_DV), F32),
        ],
        scratch_shapes=[
            pltpu.VMEM((H, GDN_DK, GDN_DV), F32),
            pltpu.VMEM((8 + C, GDN_CONV_CH), F32),
        ],
        compiler_params=_cparams(("parallel", "arbitrary")),
        name="gdn",
    )(proj, proj, proj, abt, conv_state, conv_w,
      a_log.reshape(1, H), dt_bias.reshape(1, H), a_log.reshape(H, 1), dt_bias.reshape(H, 1),
      norm_w.reshape(1, GDN_DV), gdn_state)


def _rope_kernel(q_ref, k_ref, v_ref, iq_ref, ik_ref, cos_ref, sa_ref, sb_ref,
                 qo_ref, kf_ref, kb_ref, vb_ref, iqo_ref, ikf_ref, ikb_ref):
    cosf = cos_ref[...]
    sa = sa_ref[...]
    sb = sb_ref[...]

    def rope(x):
        return (x * cosf + pltpu.roll(x, LANE - ROPE_HALF, 1) * sa + pltpu.roll(x, ROPE_HALF, 1) * sb)

    for h in range(ATT_HEADS):
        qo_ref[0, h] = rope(q_ref[:, h * ATT_HD:(h + 1) * ATT_HD]).astype(BF16)
    for h in range(ATT_KV_HEADS):
        kr = rope(k_ref[:, h * ATT_HD:(h + 1) * ATT_HD])
        kf_ref[:, h * ATT_HD:(h + 1) * ATT_HD] = kr
        kb_ref[:, h * ATT_HD:(h + 1) * ATT_HD] = kr.astype(BF16)
    vb_ref[...] = v_ref[...].astype(BF16)
    for h in range(IDX_HEADS):
        iqo_ref[0, h] = rope(iq_ref[:, h * IDX_HD:(h + 1) * IDX_HD]).astype(BF16)
    ikr = rope(ik_ref[...])
    ikf_ref[...] = ikr
    ikb_ref[...] = ikr.astype(BF16)


def _rope_tables(L, pos0, reps):
    inv = jnp.power(ROPE_THETA, -jnp.arange(ROPE_HALF, dtype=F32) * (2.0 / ROPE_ROT))
    ang = (pos0 + jnp.arange(L)).astype(F32)[:, None] * inv[None, :]
    cos, sin = jnp.cos(ang), jnp.sin(ang)
    z16 = jnp.zeros((L, ROPE_HALF), F32)
    rest0 = jnp.zeros((L, ATT_HD - ROPE_ROT), F32)
    cosf = jnp.concatenate([cos, cos, jnp.ones((L, ATT_HD - ROPE_ROT), F32)], -1)
    sa = jnp.concatenate([-sin, z16, rest0], -1)
    sb = jnp.concatenate([z16, sin, rest0], -1)
    if reps > 1:
        cosf, sa, sb = (jnp.tile(t, (reps, 1)) for t in (cosf, sa, sb))
    return cosf, sa, sb


def _rope(proj, B, L, pos0, qb):
    T = B * L
    reps = 1
    tab_rows = L
    cosf, sa, sb = _rope_tables(L, pos0, reps)
    nq = L // qb
    tspec = pl.BlockSpec((qb, LANE), lambda i: (i % nq, 0))
    outs = pl.pallas_call(
        _rope_kernel,
        grid=(T // qb,),
        in_specs=[
            pl.BlockSpec((qb, ATT_Q), lambda i: (i, C_Q // ATT_Q)),
            pl.BlockSpec((qb, ATT_KV), lambda i: (i, C_K // ATT_KV)),
            pl.BlockSpec((qb, ATT_KV), lambda i: (i, C_V // ATT_KV)),
            pl.BlockSpec((qb, IDX_HEADS * IDX_HD), lambda i: (i, C_IQ // (IDX_HEADS * IDX_HD))),
            pl.BlockSpec((qb, IDX_HD), lambda i: (i, C_IK // IDX_HD)),
            tspec, tspec, tspec,
        ],
        out_specs=[
            pl.BlockSpec((1, ATT_HEADS, qb, ATT_HD), lambda i: (i, 0, 0, 0)),
            pl.BlockSpec((qb, ATT_KV), lambda i: (i, 0)),
            pl.BlockSpec((qb, ATT_KV), lambda i: (i, 0)),
            pl.BlockSpec((qb, ATT_KV), lambda i: (i, 0)),
            pl.BlockSpec((1, IDX_HEADS, qb, IDX_HD), lambda i: (i, 0, 0, 0)),
            pl.BlockSpec((qb, IDX_HD), lambda i: (i, 0)),
            pl.BlockSpec((qb, IDX_HD), lambda i: (i, 0)),
        ],
        out_shape=[
            jax.ShapeDtypeStruct((T // qb, ATT_HEADS, qb, ATT_HD), BF16),
            jax.ShapeDtypeStruct((T, ATT_KV), F32),
            jax.ShapeDtypeStruct((T, ATT_KV), BF16),
            jax.ShapeDtypeStruct((T, ATT_KV), BF16),
            jax.ShapeDtypeStruct((T // qb, IDX_HEADS, qb, IDX_HD), BF16),
            jax.ShapeDtypeStruct((T, IDX_HD), F32),
            jax.ShapeDtypeStruct((T, IDX_HD), BF16),
        ],
        compiler_params=_cparams(("parallel",)),
        name="rope",
    )(proj, proj, proj, proj, proj, cosf, sa, sb)
    return outs


def _dsa_kernel(iq_ref, sm_ref, q_ref, ik_ref, k_ref, v_ref, o_ref, sc_ref, *, qb, S, pos0, n_sel):
    j = pl.program_id(1)
    TK = TK_ATT
    q_last = pos0 + (j + 1) * qb - 1
    lim = jnp.minimum(((q_last // CHUNK) + 1) * CHUNK, S)
    nkt = (lim + TK - 1) // TK
    qpos = pos0 + j * qb + lax.broadcasted_iota(jnp.int32, (qb, 1), 0)
    qlim = jnp.minimum((jnp.right_shift(qpos, CHUNK.bit_length() - 1) + 1) * CHUNK, S)
    iw = sm_ref[:, SM_IW:SM_IW + IDX_HEADS] * ((IDX_HEADS ** -0.5) * (IDX_HD ** -0.5))
    HG = 4

    def score_tile(kt, carry):
        ks = pl.multiple_of(kt * TK, TK)
        ik_t = ik_ref[0, pl.ds(ks, TK), :]
        acc = jnp.zeros((qb, TK), F32)
        for hg in range(IDX_HEADS // HG):
            iq_g = iq_ref[0, hg * HG:(hg + 1) * HG].reshape(HG * qb, IDX_HD)
            r = jnp.maximum(_dot_nt(iq_g, ik_t), 0.0)
            for t in range(HG):
                hh = hg * HG + t
                acc = acc + iw[:, hh:hh + 1] * r[t * qb:(t + 1) * qb]
        kpos = ks + lax.broadcasted_iota(jnp.int32, (qb, TK), 1)
        sc_ref[kt] = jnp.where(kpos < qlim, acc, -jnp.inf)
        return carry

    lax.fori_loop(0, nkt, score_tile, 0)

    def lane_fold(m):
        p = m[:, 0:LANE]
        for t in range(1, TK // LANE):
            p = p + m[:, t * LANE:(t + 1) * LANE]
        return p

    def count_ge(x):
        def body(kt, c):
            return c + lane_fold(jnp.where(sc_ref[kt] >= x, 1.0, 0.0))
        part = lax.fori_loop(0, nkt, body, jnp.zeros((qb, LANE), F32))
        return jnp.sum(part, axis=1, keepdims=True)

    def minmax(kt, c):
        lo, hi = c
        t = sc_ref[kt]
        lo = jnp.minimum(lo, jnp.min(jnp.where(t > -jnp.inf, t, jnp.inf), axis=1, keepdims=True))
        hi = jnp.maximum(hi, jnp.max(t, axis=1, keepdims=True))
        return lo, hi

    lo0, hi0 = lax.fori_loop(0, nkt, minmax,
                             (jnp.full((qb, 1), jnp.inf, F32), jnp.full((qb, 1), -jnp.inf, F32)))
    kf = float(n_sel)
    cnt_all = count_ge(lo0)
    cnt_hi = count_ge(hi0)
    top_tied = cnt_hi >= kf
    fixed = (cnt_all <= kf) | top_tied
    lo_init = jnp.where(top_tied, hi0, lo0)

    def cond(st):
        return jnp.logical_and(st[2] > 0, st[3] < 256)

    def body(st):
        lo, hi, _, it = st
        mid = lo + (hi - lo) * 0.5
        stop = (mid <= lo) | (mid >= hi) | fixed
        ge = count_ge(mid) >= kf
        lo2 = jnp.where(stop, lo, jnp.where(ge, mid, lo))
        hi2 = jnp.where(stop, hi, jnp.where(ge, hi, mid))
        nact = jnp.sum(jnp.where(stop, 0, 1))
        return lo2, hi2, nact, it + 1

    thr, _, _, _ = lax.while_loop(cond, body, (lo_init, hi0, jnp.int32(1), jnp.int32(0)))

    def to_bias(kt, carry):
        sc_ref[kt] = jnp.where(sc_ref[kt] >= thr, 0.0, NEG_BIG)
        return carry

    lax.fori_loop(0, nkt, to_bias, 0)

    R = ATT_REP
    for g in range(ATT_KV_HEADS):
        qg = q_ref[0, g * R:(g + 1) * R].reshape(R * qb, ATT_HD)

        def att_tile(kt, carry, g=g, qg=qg):
            m, l, acc = carry
            ks = pl.multiple_of(kt * TK, TK)
            k_t = k_ref[0, pl.ds(ks, TK), g * ATT_HD:(g + 1) * ATT_HD]
            v_t = v_ref[0, pl.ds(ks, TK), g * ATT_HD:(g + 1) * ATT_HD]
            s = _dot_nt(qg, k_t) * (ATT_HD ** -0.5)
            s = (s.reshape(R, qb, TK) + sc_ref[kt][None]).reshape(R * qb, TK)
            m_new = jnp.maximum(m, jnp.max(s, axis=1, keepdims=True))
            p = jnp.exp(s - m_new)
            a = jnp.exp(m - m_new)
            l = a * l + jnp.sum(p, axis=1, keepdims=True)
            acc = a * acc + _dot(p.astype(BF16), v_t)
            return m_new, l, acc

        m0 = jnp.full((R * qb, 1), NEG_BIG, F32)
        l0 = jnp.zeros((R * qb, 1), F32)
        a0 = jnp.zeros((R * qb, ATT_HD), F32)
        _, l, acc = lax.fori_loop(0, nkt, att_tile, (m0, l0, a0))
        out = acc / l
        for r in range(R):
            hh = g * R + r
            o_ref[:, hh * ATT_HD:(hh + 1) * ATT_HD] = out[r * qb:(r + 1) * qb].astype(o_ref.dtype)


def _dsa(iq_hm, proj, q_hm, ik_all, k_all, v_all, B, L, S, pos0):
    qb = min(Q_BLOCK, L)
    nq = L // qb
    S_pad = ik_all.shape[1]
    n_sel = min(TOPK_KEYS, S // 4)
    kern = functools.partial(_dsa_kernel, qb=qb, S=S, pos0=pos0, n_sel=n_sel)
    return pl.pallas_call(
        kern,
        grid=(B, nq),
        in_specs=[
            pl.BlockSpec((1, IDX_HEADS, qb, IDX_HD), lambda b, j: (b * nq + j, 0, 0, 0)),
            pl.BlockSpec((qb, LANE), lambda b, j: (b * nq + j, C_SM // LANE)),
            pl.BlockSpec((1, ATT_HEADS, qb, ATT_HD), lambda b, j: (b * nq + j, 0, 0, 0)),
            pl.BlockSpec((1, S_pad, IDX_HD), lambda b, j: (b, 0, 0)),
            pl.BlockSpec((1, S_pad, ATT_KV), lambda b, j: (b, 0, 0)),
            pl.BlockSpec((1, S_pad, ATT_KV), lambda b, j: (b, 0, 0)),
        ],
        out_specs=pl.BlockSpec((qb, ATT_Q), lambda b, j: (b * nq + j, 0)),
        out_shape=jax.ShapeDtypeStruct((B * L, ATT_Q), BF16),
        scratch_shapes=[pltpu.VMEM((S_pad // TK_ATT, qb, TK_ATT), F32)],
        compiler_params=_cparams(("parallel", "arbitrary")),
        name="dsa",
    )(iq_hm, proj, q_hm, ik_all, k_all, v_all)


def _merge_kernel(oa_ref, ob_ref, wa_ref, wb_ref, ga_ref, gb_ref, o_ref):
    ya = _dot(oa_ref[...], wa_ref[...])
    yb = _dot(ob_ref[...], wb_ref[...])
    o_ref[...] = (_sigmoid(ga_ref[...]) * ya + _sigmoid(gb_ref[...]) * yb).astype(o_ref.dtype)


def _merge(oa, ob, wa, wb, proj):
    T = oa.shape[0]
    tm = min(512, T)
    tn = TN_MIX
    return pl.pallas_call(
        _merge_kernel,
        grid=(T // tm, D_MODEL // tn),
        in_specs=[
            pl.BlockSpec((tm, GDN_VAL), lambda i, j: (i, 0)),
            pl.BlockSpec((tm, ATT_Q), lambda i, j: (i, 0)),
            pl.BlockSpec((GDN_VAL, tn), lambda i, j: (0, j)),
            pl.BlockSpec((ATT_Q, tn), lambda i, j: (0, j)),
            pl.BlockSpec((tm, tn), lambda i, j: (i, C_GA // tn + j)),
            pl.BlockSpec((tm, tn), lambda i, j: (i, C_GB // tn + j)),
        ],
        out_specs=pl.BlockSpec((tm, tn), lambda i, j: (i, j)),
        out_shape=jax.ShapeDtypeStruct((T, D_MODEL), BF16),
        compiler_params=_cparams(("parallel", "arbitrary")),
        name="merge",
    )(oa, ob, wa, wb, proj, proj)


def _oproj_kernel(m_ref, w_ref, x_ref, g1_ref, sc2_ref, sh2_ref, lg_ref, lb_ref, x1_ref, h2_ref, acc_ref, *, nj):
    j = pl.program_id(1)
    acc_ref[j] = _dot(m_ref[...], w_ref[...])

    @pl.when(j == nj - 1)
    def _():
        bb, tl, D = x_ref.shape
        tn = acc_ref.shape[2]
        sl = lambda t: slice(t * tn, (t + 1) * tn)
        s1 = jnp.zeros((bb, tl, 1), F32)
        for t in range(nj):
            v = ALPHA * x_ref[:, :, sl(t)] + g1_ref[:, :, sl(t)] * acc_ref[t].reshape(bb, tl, tn)
            acc_ref[t] = v.reshape(bb * tl, tn)
            s1 = s1 + jnp.sum(v, -1, keepdims=True)
        mu = s1 * (1.0 / D)
        s2 = jnp.zeros((bb, tl, 1), F32)
        for t in range(nj):
            d = acc_ref[t].reshape(bb, tl, tn) - mu
            s2 = s2 + jnp.sum(d * d, -1, keepdims=True)
        rstd = lax.rsqrt(s2 * (1.0 / D) + LN_EPS)
        for t in range(nj):
            x1 = (acc_ref[t].reshape(bb, tl, tn) - mu) * rstd * lg_ref[:, :, sl(t)] + lb_ref[:, :, sl(t)]
            x1_ref[:, :, sl(t)] = x1
            h2_ref[:, sl(t)] = (x1 * (1.0 + sc2_ref[:, :, sl(t)]) + sh2_ref[:, :, sl(t)]).reshape(bb * tl, tn)


def _oproj(m, w_o, x3, mod3, row0, ln_g, ln_b):
    B, L, D = x3.shape
    bb, tl = _row_tiling(B, L, TM_MIX)
    nl = L // tl
    rows = bb * tl
    mrow = row0 // bb
    tn = TN_MIX
    nj = D // tn
    kern = functools.partial(_oproj_kernel, nj=nj)
    modspec = lambda c: pl.BlockSpec((bb, 1, D), lambda i, j: (mrow + i // nl, 0, c))
    return pl.pallas_call(
        kern,
        grid=((B // bb) * nl, nj),
        in_specs=[
            pl.BlockSpec((rows, D), lambda i, j: (i, 0)),
            pl.BlockSpec((D, tn), lambda i, j: (0, j)),
            pl.BlockSpec((bb, tl, D), lambda i, j: (i // nl, i % nl, 0)),
            modspec(2), modspec(4), modspec(3),
            pl.BlockSpec((1, 1, D), lambda i, j: (0, 0, 0)),
            pl.BlockSpec((1, 1, D), lambda i, j: (0, 0, 0)),
        ],
        out_specs=[
            pl.BlockSpec((bb, tl, D), lambda i, j: (i // nl, i % nl, 0)),
            pl.BlockSpec((rows, D), lambda i, j: (i, 0)),
        ],
        out_shape=[
            jax.ShapeDtypeStruct((B, L, D), F32),
            jax.ShapeDtypeStruct((B * L, D), F32),
        ],
        scratch_shapes=[pltpu.VMEM((nj, rows, tn), F32)],
        compiler_params=_cparams(("parallel", "arbitrary")),
        name="oproj_ln1",
    )(m, w_o, x3, mod3, mod3, mod3, ln_g.reshape(1, 1, D), ln_b.reshape(1, 1, D))


def _router_kernel(h_ref, w_ref, b_ref, id_ref, wt_ref):
    logits = jnp.dot(h_ref[...], w_ref[...], precision=lax.Precision.HIGHEST,
                     preferred_element_type=F32) + b_ref[...]
    rows = logits.shape[0]
    lane = lax.broadcasted_iota(jnp.int32, (rows, LANE), 1)
    is_g = lane < N_GROUPS
    gl = jnp.where(is_g, logits, -jnp.inf)
    gmax = jnp.max(gl, axis=1, keepdims=True)
    g_sel = jnp.min(jnp.where(gl == gmax, lane, LANE), axis=1, keepdims=True)
    p_grp = 1.0 / jnp.sum(jnp.where(is_g, jnp.exp(gl - gmax), 0.0), axis=1, keepdims=True)
    e_lo = N_GROUPS + g_sel * EXP_PER_GROUP
    in_g = (lane >= e_lo) & (lane < e_lo + EXP_PER_GROUP)
    el = jnp.where(in_g, logits, -jnp.inf)
    m1 = jnp.max(el, axis=1, keepdims=True)
    i1 = jnp.min(jnp.where(el == m1, lane, LANE), axis=1, keepdims=True)
    el2 = jnp.where(lane == i1, -jnp.inf, el)
    m2 = jnp.max(el2, axis=1, keepdims=True)
    i2 = jnp.min(jnp.where(el2 == m2, lane, LANE), axis=1, keepdims=True)
    e21 = jnp.exp(m2 - m1)
    w1 = p_grp / (1.0 + e21)
    w2 = p_grp * e21 / (1.0 + e21)
    id_ref[...] = jnp.where(lane == 0, i1 - N_GROUPS, jnp.where(lane == 1, i2 - N_GROUPS, 0))
    wt_ref[...] = jnp.where(lane == 0, w1, jnp.where(lane == 1, w2, 0.0))


def _router(h2, w_r, b_r):
    T, D = h2.shape
    tm = min(512, T)
    return pl.pallas_call(
        _router_kernel,
        grid=(T // tm,),
        in_specs=[
            pl.BlockSpec((tm, D), lambda i: (i, 0)),
            pl.BlockSpec((D, LANE), lambda i: (0, 0)),
            pl.BlockSpec((1, LANE), lambda i: (0, 0)),
        ],
        out_specs=[pl.BlockSpec((tm, LANE), lambda i: (i, 0)), pl.BlockSpec((tm, LANE), lambda i: (i, 0))],
        out_shape=[jax.ShapeDtypeStruct((T, LANE), jnp.int32), jax.ShapeDtypeStruct((T, LANE), F32)],
        compiler_params=_cparams(("parallel",)),
        name="router",
    )(h2, w_r, b_r)


def _expert_kernel(blk_e_ref, tok_ref, nused_ref, h_hbm, wg_ref, wu_ref, wd_ref, o_ref,
                   xbuf_ref, xb16_ref, acc_ref, sem_ref, *, BM, nfc):
    i = pl.program_id(0)
    c = pl.program_id(1)
    nused = nused_ref[0]
    slot = lax.rem(i, 2)

    def row_copy(blk, r, sl):
        tok = tok_ref[blk * BM + r]
        return pltpu.make_async_copy(h_hbm.at[pl.ds(tok, 1)], xbuf_ref.at[sl, pl.ds(r, 1)], sem_ref.at[sl])

    def start_gather(blk, sl):
        def body(r, carry):
            row_copy(blk, r, sl).start()
            return carry
        lax.fori_loop(0, BM, body, 0)

    def wait_gather(blk, sl):
        def body(r, carry):
            row_copy(blk, r, sl).wait()
            return carry
        lax.fori_loop(0, BM, body, 0)

    @pl.when(jnp.logical_and(c == 0, i < nused))
    def _():
        @pl.when(i == 0)
        def _():
            start_gather(0, 0)

        @pl.when(i + 1 < nused)
        def _():
            start_gather(i + 1, 1 - slot)

        wait_gather(i, slot)
        xb16_ref[...] = xbuf_ref[slot].astype(BF16)

    @pl.when(i < nused)
    def _():
        x = xb16_ref[...]
        gate = _dot(x, wg_ref[0])
        up = _dot(x, wu_ref[0])
        act = (_silu(gate) * up).astype(BF16)
        part = _dot(act, wd_ref[0])

        @pl.when(c == 0)
        def _():
            acc_ref[...] = part

        @pl.when(c > 0)
        def _():
            acc_ref[...] += part

        @pl.when(c == nfc - 1)
        def _():
            o_ref[...] = acc_ref[...]

    @pl.when(jnp.logical_and(i >= nused, c == nfc - 1))
    def _():
        o_ref[...] = jnp.zeros_like(o_ref)


def _experts(h2, blk_e, slot_tok, nused, w_up16, w_down16, BM):
    T, D = h2.shape
    P = slot_tok.shape[0]
    nblk = P // BM
    fc = FC_MOE
    nfc = D_EXPERT // fc
    kern = functools.partial(_expert_kernel, BM=BM, nfc=nfc)

    def eidx(i, nu):
        return jnp.minimum(i, nu[0] - 1)

    def cidx(i, c, nu):
        return jnp.where(i < nu[0], c, nfc - 1)

    grid_spec = pltpu.PrefetchScalarGridSpec(
        num_scalar_prefetch=3,
        grid=(nblk, nfc),
        in_specs=[
            pl.BlockSpec(memory_space=pl.ANY),
            pl.BlockSpec((1, D, fc), lambda i, c, be, tk, nu: (be[eidx(i, nu)], 0, cidx(i, c, nu))),
            pl.BlockSpec((1, D, fc), lambda i, c, be, tk, nu: (be[eidx(i, nu)], 0, nfc + cidx(i, c, nu))),
            pl.BlockSpec((1, fc, D), lambda i, c, be, tk, nu: (be[eidx(i, nu)], cidx(i, c, nu), 0)),
        ],
        out_specs=pl.BlockSpec((BM, D), lambda i, c, be, tk, nu: (i, 0)),
        scratch_shapes=[
            pltpu.VMEM((2, BM, D), F32),
            pltpu.VMEM((BM, D), BF16),
            pltpu.VMEM((BM, D), F32),
            pltpu.SemaphoreType.DMA((2,)),
        ],
    )
    return pl.pallas_call(
        kern,
        grid_spec=grid_spec,
        out_shape=jax.ShapeDtypeStruct((P, D), F32),
        compiler_params=_cparams(("arbitrary", "arbitrary")),
        name="experts",
    )(blk_e, slot_tok, nused, h2, w_up16, w_up16, w_down16)


def _route_metadata(ids, T, BM):
    expert = ids[:, 0:2].reshape(-1)
    A = 2 * T
    onehot = (expert[:, None] == jnp.arange(N_EXPERTS, dtype=jnp.int32)[None, :]).astype(jnp.int32)
    csum = jnp.cumsum(onehot, axis=0)
    counts = csum[-1]
    rank = jnp.sum((csum - onehot) * onehot, axis=1)
    padded = ((counts + BM - 1) // BM) * BM
    pend = jnp.cumsum(padded)
    pstart = pend - padded
    dest = (pstart[expert] + rank).astype(jnp.int32)
    P = ((A + BM - 1) // BM) * BM + N_EXPERTS * BM
    nblk = P // BM
    tok = jnp.arange(A, dtype=jnp.int32) // 2
    slot_tok = jnp.zeros((P,), jnp.int32).at[dest].set(tok)
    blk_e = jnp.clip(jnp.searchsorted(pend, jnp.arange(nblk, dtype=jnp.int32) * BM, side="right"),
                     0, N_EXPERTS - 1).astype(jnp.int32)
    nused = (pend[-1] // BM).astype(jnp.int32).reshape(1)
    return dest, slot_tok, blk_e, nused


def _final_kernel(dest_ref, yb_hbm, x1_ref, g2_ref, wt_ref, lg_ref, lb_ref, o_ref, buf_ref, sem_ref, *, rows, ntiles):
    i = pl.program_id(0)
    slot = lax.rem(i, 2)

    def row_copy(tile, r, sl):
        d = dest_ref[tile * 2 * rows + r]
        return pltpu.make_async_copy(yb_hbm.at[pl.ds(d, 1)], buf_ref.at[sl, pl.ds(r, 1)], sem_ref.at[sl])

    def start_gather(tile, sl):
        def body(r, carry):
            row_copy(tile, r, sl).start()
            return carry
        lax.fori_loop(0, 2 * rows, body, 0)

    def wait_gather(tile, sl):
        def body(r, carry):
            row_copy(tile, r, sl).wait()
            return carry
        lax.fori_loop(0, 2 * rows, body, 0)

    @pl.when(i == 0)
    def _():
        start_gather(0, 0)

    @pl.when(i + 1 < ntiles)
    def _():
        start_gather(i + 1, 1 - slot)

    wait_gather(i, slot)
    bb, tl, D = x1_ref.shape
    tn = TN_MIX
    sl = lambda t: slice(t * tn, (t + 1) * tn)
    w0 = wt_ref[:, 0:1]
    w1 = wt_ref[:, 1:2]
    s1 = jnp.zeros((bb, tl, 1), F32)
    for t in range(D // tn):
        f = w0 * buf_ref[slot, 0:rows, sl(t)] + w1 * buf_ref[slot, rows:2 * rows, sl(t)]
        v = ALPHA * x1_ref[:, :, sl(t)] + g2_ref[:, :, sl(t)] * f.reshape(bb, tl, tn)
        o_ref[:, :, sl(t)] = v
        s1 = s1 + jnp.sum(v, -1, keepdims=True)
    mu = s1 * (1.0 / D)
    s2 = jnp.zeros((bb, tl, 1), F32)
    for t in range(D // tn):
        d = o_ref[:, :, sl(t)] - mu
        s2 = s2 + jnp.sum(d * d, -1, keepdims=True)
    rstd = lax.rsqrt(s2 * (1.0 / D) + LN_EPS)
    for t in range(D // tn):
        o_ref[:, :, sl(t)] = (o_ref[:, :, sl(t)] - mu) * rstd * lg_ref[:, :, sl(t)] + lb_ref[:, :, sl(t)]


def _final(dest_tiles, yb, x1, mod3, row0, wts, ln_g, ln_b):
    B, L, D = x1.shape
    bb, tl = _row_tiling(B, L, TM_MIX)
    nl = L // tl
    rows = bb * tl
    mrow = row0 // bb
    ntiles = (B // bb) * nl
    kern = functools.partial(_final_kernel, rows=rows, ntiles=ntiles)
    grid_spec = pltpu.PrefetchScalarGridSpec(
        num_scalar_prefetch=1,
        grid=(ntiles,),
        in_specs=[
            pl.BlockSpec(memory_space=pl.ANY),
            pl.BlockSpec((bb, tl, D), lambda i, d: (i // nl, i % nl, 0)),
            pl.BlockSpec((bb, 1, D), lambda i, d: (mrow + i // nl, 0, 5)),
            pl.BlockSpec((rows, LANE), lambda i, d: (i, 0)),
            pl.BlockSpec((1, 1, D), lambda i, d: (0, 0, 0)),
            pl.BlockSpec((1, 1, D), lambda i, d: (0, 0, 0)),
        ],
        out_specs=pl.BlockSpec((bb, tl, D), lambda i, d: (i // nl, i % nl, 0)),
        scratch_shapes=[
            pltpu.VMEM((2, 2 * rows, D), F32),
            pltpu.SemaphoreType.DMA((2,)),
        ],
    )
    return pl.pallas_call(
        kern,
        grid_spec=grid_spec,
        out_shape=jax.ShapeDtypeStruct((B, L, D), F32),
        compiler_params=_cparams(("arbitrary",)),
        name="combine_ln2",
    )(dest_tiles, yb, x1, mod3, wts, ln_g.reshape(1, 1, D), ln_b.reshape(1, 1, D))


def _pad_keys(t, S_pad):
    S = t.shape[1]
    if S == S_pad:
        return t
    return jnp.pad(t, ((0, 0), (0, S_pad - S), (0, 0)))


def _layer(x3, mod3, row0, conv_state, gdn_state, past_k, past_v, past_ik, wp, moe_bm):
    B, L, D = x3.shape
    T = B * L
    pos0 = past_k.shape[1]
    S = pos0 + L
    proj = _inproj(x3, mod3, row0, wp["w_in"])

    oa, new_gdn = _gdn(proj, B, L, conv_state, gdn_state, wp["conv_w"], wp["a_log"], wp["dt_bias"],
                       wp["gdn_norm_w"])
    new_conv = proj[:, C_QKV:C_QKV + GDN_CONV_CH].reshape(B, L, GDN_CONV_CH)[:, L - (CONV_W - 1):]

    qb = min(Q_BLOCK, L)
    q_hm, kf, k16, v16, iq_hm, ikf, ik16 = _rope(proj, B, L, pos0, qb)
    S_pad = ((S + TK_ATT - 1) // TK_ATT) * TK_ATT
    k_all = _pad_keys(jnp.concatenate([past_k.reshape(B, pos0, ATT_KV).astype(BF16), k16.reshape(B, L, ATT_KV)], 1), S_pad)
    v_all = _pad_keys(jnp.concatenate([past_v.reshape(B, pos0, ATT_KV).astype(BF16), v16.reshape(B, L, ATT_KV)], 1), S_pad)
    ik_all = _pad_keys(jnp.concatenate([past_ik.astype(BF16), ik16.reshape(B, L, IDX_HD)], 1), S_pad)
    ob = _dsa(iq_hm, proj, q_hm, ik_all, k_all, v_all, B, L, S, pos0)

    m = _merge(oa, ob, wp["w_br_a"], wp["w_br_b"], proj)
    x1, h2 = _oproj(m, wp["w_o"], x3, mod3, row0, wp["ln1_g"], wp["ln1_b"])

    ids, wts = _router(h2, wp["w_r"], wp["b_r"])
    dest, slot_tok, blk_e, nused = _route_metadata(ids, T, moe_bm)
    yb = _experts(h2, blk_e, slot_tok, nused, wp["w_up"], wp["w_down"], moe_bm)
    bb, tl = _row_tiling(B, L, TM_MIX)
    rows = bb * tl
    dest_tiles = dest.reshape(T // rows, rows, 2).transpose(0, 2, 1).reshape(-1)
    y = _final(dest_tiles, yb, x1, mod3, row0, wts, wp["ln2_g"], wp["ln2_b"])

    kb_out = kf.reshape(B, L, ATT_KV_HEADS, ATT_HD)
    vb_out = proj[:, C_V:C_V + ATT_KV].reshape(B, L, ATT_KV_HEADS, ATT_HD)
    ik_out = ikf.reshape(B, L, IDX_HD)
    return y, new_conv, new_gdn, kb_out, vb_out, ik_out


def _permute_w_in(w_in):
    sizes = (GDN_CONV_CH, GDN_HEADS, GDN_HEADS, GDN_VAL, ATT_Q, ATT_KV, ATT_KV, IDX_HEADS * IDX_HD, IDX_HD,
             IDX_HEADS, 2 * D_MODEL)
    parts, off = [], 0
    for s in sizes:
        parts.append(w_in[:, off:off + s])
        off += s
    qkv, a, b, z, q, k, v, iq, ik, iw, gates = parts
    pad = jnp.zeros((w_in.shape[0], SM_W - 2 * GDN_HEADS - IDX_HEADS), w_in.dtype)
    return jnp.concatenate([qkv, z, iq, q, k, v, ik, a, b, iw, pad, gates], axis=1).astype(BF16)


def kernel(x_prompt, x_sample, c_prompt, c_sample, state_conv, state_gdn, cache_k, cache_v, cache_idx_k,
           w_ada, b_ada, w_in, conv_w, a_log, dt_bias, gdn_norm_w, w_br_a, w_br_b, w_o,
           ln1_g, ln1_b, ln2_g, ln2_b, w_grp, b_grp, w_rtr, b_rtr, w_up, w_down):
    Bp, Lp, D = x_prompt.shape
    Bs, Ls, _ = x_sample.shape
    l = 0
    row0_p = Bs
    nrows = ((Bs + Bp + 7) // 8) * 8
    c_all = jnp.concatenate([c_sample, c_prompt, jnp.zeros((nrows - Bs - Bp, D), F32)], 0)
    mod = _ada(c_all, w_ada[l], b_ada[l])
    mod3 = mod.reshape(nrows, 1, 6 * D)

    nr = LANE - N_GROUPS - N_EXPERTS
    wp = dict(
        w_in=_permute_w_in(w_in[l]),
        conv_w=conv_w[l], a_log=a_log[l], dt_bias=dt_bias[l], gdn_norm_w=gdn_norm_w[l],
        w_br_a=w_br_a[l].astype(BF16), w_br_b=w_br_b[l].astype(BF16), w_o=w_o[l].astype(BF16),
        ln1_g=ln1_g[l], ln1_b=ln1_b[l], ln2_g=ln2_g[l], ln2_b=ln2_b[l],
        w_r=jnp.concatenate([w_grp[l], w_rtr[l], jnp.zeros((D, nr), F32)], 1),
        b_r=jnp.concatenate([b_grp[l], b_rtr[l], jnp.zeros((nr,), F32)]).reshape(1, LANE),
        w_up=w_up[l].astype(BF16), w_down=w_down[l].astype(BF16),
    )

    zc = jnp.zeros((Bp, CONV_W - 1, GDN_CONV_CH), F32)
    zs = jnp.zeros((Bp, GDN_HEADS, GDN_DK, GDN_DV), F32)
    zk = jnp.zeros((Bp, 0, ATT_KV_HEADS, ATT_HD), F32)
    zik = jnp.zeros((Bp, 0, IDX_HD), F32)
    yp, c1, g1, k1, v1, i1 = _layer(x_prompt, mod3, row0_p, zc, zs, zk, zk, zik, wp, 256)
    ys, c2, g2, k2, v2, i2 = _layer(x_sample, mod3, 0, state_conv[l], state_gdn[l], cache_k[l], cache_v[l],
                                    cache_idx_k[l], wp, 128)
    st = lambda t: t[None]
    return (yp, ys, st(c1), st(g1), st(k1), st(v1), st(i1), st(c2), st(g2), st(k2), st(v2), st(i2))
```

```python
import functools

import jax
import jax.numpy as jnp
from jax import lax
from jax.experimental import pallas as pl
from jax.experimental.pallas import tpu as pltpu

F32 = jnp.float32
BF16 = jnp.bfloat16

D_MODEL = 4096
CHUNK = 64
GDN_HEADS = D_MODEL // 256
GDN_DK = 128
GDN_DV = 128
GDN_KEY = GDN_HEADS * GDN_DK
GDN_VAL = GDN_HEADS * GDN_DV
GDN_CONV_CH = 2 * GDN_KEY + GDN_VAL
CONV_W = 4
ATT_HEADS = D_MODEL // 256
ATT_KV_HEADS = ATT_HEADS // 4
ATT_REP = ATT_HEADS // ATT_KV_HEADS
ATT_HD = 128
ATT_Q = ATT_HEADS * ATT_HD
ATT_KV = ATT_KV_HEADS * ATT_HD
IDX_HEADS = D_MODEL // 128
IDX_HD = 128
TOPK_KEYS = 256
Q_BLOCK = 128
ROPE_THETA = 500000.0
ROPE_ROT = ATT_HD // 4
ROPE_HALF = ROPE_ROT // 2
N_GROUPS = 4
EXP_PER_GROUP = 8
N_EXPERTS = N_GROUPS * EXP_PER_GROUP
D_EXPERT = D_MODEL // 4
DEPTH = 1
ALPHA = (2.0 * DEPTH) ** 0.25
LN_EPS = 1e-5
RMS_EPS = 1e-6

C_QKV = 0
C_Z = C_QKV + GDN_CONV_CH
C_IQ = C_Z + GDN_VAL
C_Q = C_IQ + IDX_HEADS * IDX_HD
C_K = C_Q + ATT_Q
C_V = C_K + ATT_KV
C_IK = C_V + ATT_KV
C_SM = C_IK + IDX_HD
SM_W = 384
C_GA = C_SM + SM_W
C_GB = C_GA + D_MODEL
NP = C_GB + D_MODEL
SM_A, SM_B, SM_IW = 0, GDN_HEADS, 2 * GDN_HEADS

LANE = 128
NEG_BIG = -1e30
VMEM_LIMIT = 56 * 1024 * 1024

TN_IN = 512
TM_IN = 512
TM_MIX = 256
TN_MIX = 512
TK_ATT = 512
FC_MOE = 256


def _cparams(sem):
    return pltpu.CompilerParams(dimension_semantics=sem, vmem_limit_bytes=VMEM_LIMIT)


def _dot(a, b):
    return jnp.dot(a, b, preferred_element_type=F32)


def _dot_nt(a, b):
    return lax.dot_general(a, b, (((1,), (1,)), ((), ())), preferred_element_type=F32)


def _dot_tn(a, b):
    return lax.dot_general(a, b, (((0,), (0,)), ((), ())), preferred_element_type=F32)


def _split_bf16(a):
    hi = a.astype(BF16)
    lo = (a - hi.astype(F32)).astype(BF16)
    return hi, lo


def _mm3(a, b, dot=_dot):
    ah, al = _split_bf16(a)
    bh, bl = _split_bf16(b)
    return dot(ah, bh) + (dot(ah, bl) + dot(al, bh))


def _mm1(a, b):
    return _dot(a.astype(BF16), b.astype(BF16))


def _mm1_nt(a, b):
    return _dot_nt(a.astype(BF16), b.astype(BF16))


def _sigmoid(x):
    return 1.0 / (1.0 + jnp.exp(-x))


def _silu(x):
    return x * _sigmoid(x)


def _softplus(x):
    return jnp.maximum(x, 0.0) + jnp.log(1.0 + jnp.exp(-jnp.abs(x)))


def _ada_kernel(c_ref, w_ref, b_ref, o_ref):
    s = _silu(c_ref[...])
    o_ref[...] = _dot(s.astype(BF16), w_ref[...].astype(BF16)) + b_ref[...]


def _ada(c_all, w_ada, b_ada):
    R, D = c_all.shape
    N = w_ada.shape[1]
    tn = 512
    return pl.pallas_call(
        _ada_kernel,
        grid=(N // tn,),
        in_specs=[
            pl.BlockSpec((R, D), lambda j: (0, 0)),
            pl.BlockSpec((D, tn), lambda j: (0, j)),
            pl.BlockSpec((1, tn), lambda j: (0, j)),
        ],
        out_specs=pl.BlockSpec((R, tn), lambda j: (0, j)),
        out_shape=jax.ShapeDtypeStruct((R, N), F32),
        compiler_params=_cparams(("parallel",)),
        name="ada",
    )(c_all, w_ada, b_ada.reshape(1, N))


def _inproj_kernel(x_ref, sc_ref, sh_ref, w_ref, o_ref, h_ref):
    @pl.when(pl.program_id(1) == 0)
    def _():
        h = x_ref[...] * (1.0 + sc_ref[...]) + sh_ref[...]
        h_ref[...] = h.reshape(h_ref.shape).astype(BF16)

    o_ref[...] = _dot(h_ref[...], w_ref[...])


def _row_tiling(B, L, tm):
    if L >= tm:
        return 1, tm
    bb = max(1, min(B, tm // L))
    while B % bb:
        bb -= 1
    return bb, L


def _inproj(x3, mod3, row0, w_perm):
    B, L, D = x3.shape
    bb, tl = _row_tiling(B, L, TM_IN)
    nl = L // tl
    rows = bb * tl
    mrow = row0 // bb
    return pl.pallas_call(
        _inproj_kernel,
        grid=((B // bb) * nl, NP // TN_IN),
        in_specs=[
            pl.BlockSpec((bb, tl, D), lambda i, j: (i // nl, i % nl, 0)),
            pl.BlockSpec((bb, 1, D), lambda i, j: (mrow + i // nl, 0, 1)),
            pl.BlockSpec((bb, 1, D), lambda i, j: (mrow + i // nl, 0, 0)),
            pl.BlockSpec((D, TN_IN), lambda i, j: (0, j)),
        ],
        out_specs=pl.BlockSpec((rows, TN_IN), lambda i, j: (i, j)),
        out_shape=jax.ShapeDtypeStruct((B * L, NP), F32),
        scratch_shapes=[pltpu.VMEM((rows, D), BF16)],
        compiler_params=_cparams(("parallel", "arbitrary")),
        name="inproj",
    )(x3, mod3, mod3, w_perm)


GDN_GROUP = 4


def _split3_bf16(a):
    hi = a.astype(BF16)
    r = a - hi.astype(F32)
    mid = r.astype(BF16)
    lo = (r - mid.astype(F32)).astype(BF16)
    return hi, mid, lo


def _dot_exact01(a, b01, a_is_01=False):
    if a_is_01:
        h, m, l = _split3_bf16(b01)
        return _dot(a, h) + (_dot(a, m) + _dot(a, l))
    h, m, l = _split3_bf16(a)
    return _dot(h, b01) + (_dot(m, b01) + _dot(l, b01))


def _mm3p(a_hl, b_hl, dot=_dot):
    (ah, al), (bh, bl) = a_hl, b_hl
    return dot(ah, bh) + (dot(ah, bl) + dot(al, bh))


def _block_rows_hl(x_hl, nblk, mask01):
    return tuple(jnp.concatenate([p] * nblk, axis=0) * mask01 for p in x_hl)


def _gdn_kernel(qkv_ref, z_ref, sm_ref, cst_ref, cw_ref, alr_ref, dtr_ref, nw_ref, s0_ref,
                o_ref, sfin_ref, s_ref, ext_ref, mt_ref, mk_ref, mx_ref, ml_ref, *, C, nchunks):
    n = pl.program_id(1)
    H = GDN_HEADS
    G = GDN_GROUP
    NG = H // G
    GC = G * C
    PADR = 8
    logc = C.bit_length() - 1

    @pl.when(n == 0)
    def _():
        s_ref[...] = s0_ref[0]
        ext_ref[PADR - (CONV_W - 1):PADR, :] = cst_ref[0]

    ext_ref[PADR:PADR + C, :] = qkv_ref[...]
    cw = cw_ref[...]
    y = ext_ref[PADR - 3:PADR - 3 + C, :] * cw[0:1]
    for jw in range(1, CONV_W):
        y = y + ext_ref[PADR - 3 + jw:PADR - 3 + jw + C, :] * cw[jw:jw + 1]
    y = _silu(y)
    hist = ext_ref[PADR + C - (CONV_W - 1):PADR + C, :]
    ext_ref[PADR - (CONV_W - 1):PADR, :] = hist

    sm = sm_ref[...]
    g_col = -jnp.exp(alr_ref[...]) * _softplus(sm[:, SM_A:SM_A + H] + dtr_ref[...])
    beta_col = _sigmoid(sm[:, SM_B:SM_B + H])

    ii = lax.broadcasted_iota(jnp.int32, (C, C), 0)
    jj = lax.broadcasted_iota(jnp.int32, (C, C), 1)
    tril01 = jnp.where(jj <= ii, 1.0, 0.0).astype(BF16)
    gc_col = _dot_exact01(tril01, g_col, a_is_01=True)
    egc_col = jnp.exp(gc_col)
    gc_last = gc_col[C - 1:C, :]
    ekd_col = jnp.exp(gc_last - gc_col)
    egl = jnp.exp(gc_last)

    W = H * C
    hrow = lax.broadcasted_iota(jnp.int32, (H, W), 0)
    hlane = jnp.right_shift(lax.broadcasted_iota(jnp.int32, (H, W), 1), logc)
    e_seg = jnp.where(hrow == hlane, 1.0, 0.0).astype(BF16)
    gseg = _dot_exact01(gc_col, e_seg)
    ri = lax.broadcasted_iota(jnp.int32, (C, W), 0)
    cj = jnp.bitwise_and(lax.broadcasted_iota(jnp.int32, (C, W), 1), C - 1)
    grow = jnp.sum(jnp.where(ri == cj, gseg, 0.0), axis=0, keepdims=True)
    decay_all = jnp.where(cj <= ri, jnp.exp(gseg - grow), 0.0)

    gi = lax.broadcasted_iota(jnp.int32, (C, GC), 0)
    gj = jnp.bitwise_and(lax.broadcasted_iota(jnp.int32, (C, GC), 1), C - 1)
    strict = gj < gi
    eye = jnp.where(gi == gj, 1.0, 0.0)
    nlev = logc

    def lev_mask(lev):
        return ((jnp.right_shift(gi, lev + 1) == jnp.right_shift(gj, lev + 1))
                & (jnp.bitwise_and(jnp.right_shift(gi, lev), 1) == 1)
                & (jnp.bitwise_and(jnp.right_shift(gj, lev), 1) == 0))

    @pl.when(n == 0)
    def _():
        def own(ncols, col_head):
            r = jnp.right_shift(lax.broadcasted_iota(jnp.int32, (G * C, ncols), 0), logc)
            c = col_head(lax.broadcasted_iota(jnp.int32, (G * C, ncols), 1))
            return jnp.where(r == c, 1.0, 0.0).astype(BF16)

        mt_ref[...] = own(GC, lambda c: jnp.right_shift(c, logc))
        mk_ref[...] = own(G * GDN_DK, lambda c: jnp.right_shift(c, 7))
        mx_ref[...] = own(2 * G * GDN_DK, lambda c: jnp.bitwise_and(jnp.right_shift(c, 7), G - 1))
        for lev in range(1, nlev):
            ml_ref[lev] = jnp.where(lev_mask(lev), 1.0, 0.0).astype(BF16)

    nw = nw_ref[...]
    z = z_ref[...]

    lmats, lm_hls, intras, rhss, qds, kds = [], [], [], [], [], []
    for g in range(NG):
        qs, ks, kbs, vbs, kes = [], [], [], [], []
        for t in range(G):
            h = g * G + t
            qh = y[:, h * GDN_DK:(h + 1) * GDN_DK]
            kh = y[:, GDN_KEY + h * GDN_DK:GDN_KEY + (h + 1) * GDN_DK]
            vh = y[:, 2 * GDN_KEY + h * GDN_DV:2 * GDN_KEY + (h + 1) * GDN_DV]
            q = qh * lax.rsqrt(jnp.sum(qh * qh, -1, keepdims=True) + RMS_EPS) * (GDN_DK ** -0.5)
            k = kh * lax.rsqrt(jnp.sum(kh * kh, -1, keepdims=True) + RMS_EPS)
            beta = beta_col[:, h:h + 1]
            egc = egc_col[:, h:h + 1]
            kb = k * beta
            qs.append(q)
            ks.append(k)
            kbs.append(kb)
            vbs.append(vh * beta)
            kes.append(kb * egc)
            qds.append(q * egc)
            kds.append(k * ekd_col[:, h:h + 1])
        bdk_hl = _block_rows_hl(_split_bf16(jnp.concatenate(ks, axis=-1)), G, mk_ref[...])
        dec = decay_all[:, g * GC:(g + 1) * GC]
        kk = _mm3p(_split_bf16(jnp.concatenate(kbs, axis=-1)), bdk_hl, _dot_nt)
        qk = _dot_nt(jnp.concatenate(qs, axis=-1).astype(BF16), bdk_hl[0])
        lmat = jnp.where(strict, kk * dec, 0.0)
        lmats.append(lmat)
        lm_hls.append(_split_bf16(lmat))
        intras.append((qk * dec).astype(BF16))
        rhss.append(_split_bf16(jnp.concatenate(vbs + kes, axis=-1)))

    tinvs = [eye - jnp.where(lev_mask(0), lmats[g], 0.0) for g in range(NG)]
    for lev in range(1, nlev):
        ml = ml_ref[lev]
        for g in range(NG):
            t_hl = _split_bf16(tinvs[g])
            b_hl = (lm_hls[g][0] * ml, lm_hls[g][1] * ml)
            p = _mm3p(b_hl, _block_rows_hl(t_hl, G, mt_ref[...]))
            tinvs[g] = tinvs[g] - _mm3p(t_hl, _block_rows_hl(_split_bf16(p), G, mt_ref[...]))

    for g in range(NG):
        x = _mm3p(_split_bf16(tinvs[g]), _block_rows_hl(rhss[g], G, mx_ref[...]))
        u_cat = x[:, :G * GDN_DV]
        w_cat = x[:, G * GDN_DV:]
        for t in range(G):
            h = g * G + t
            s = s_ref[h]
            s_hl = _split_bf16(s)
            lanes = slice(t * GDN_DV, (t + 1) * GDN_DV)
            v_new = u_cat[:, lanes] - _mm3p(_split_bf16(w_cat[:, lanes]), s_hl)
            v_hl = _split_bf16(v_new)
            o = _dot(qds[h].astype(BF16), s_hl[0]) + _dot(intras[g][:, t * C:(t + 1) * C], v_hl[0])
            s_ref[h] = s * egl[:, h:h + 1] + _mm3p(_split_bf16(kds[h]), v_hl, _dot_tn)
            zh = z[:, h * GDN_DV:(h + 1) * GDN_DV]
            on = o * lax.rsqrt(jnp.mean(o * o, -1, keepdims=True) + RMS_EPS) * nw * _silu(zh)
            o_ref[:, h * GDN_DV:(h + 1) * GDN_DV] = on.astype(o_ref.dtype)

    @pl.when(n == nchunks - 1)
    def _():
        sfin_ref[0] = s_ref[...]


def _gdn(proj, B, L, conv_state, gdn_state, conv_w, a_log, dt_bias, norm_w):
    C = min(CHUNK, L)
    N = L // C
    H = GDN_HEADS
    kern = functools.partial(_gdn_kernel, C=C, nchunks=N)
    return pl.pallas_call(
        kern,
        grid=(B, N),
        in_specs=[
            pl.BlockSpec((C, GDN_CONV_CH), lambda b, n: (b * N + n, C_QKV // GDN_CONV_CH)),
            pl.BlockSpec((C, GDN_VAL), lambda b, n: (b * N + n, C_Z // GDN_VAL)),
            pl.BlockSpec((C, LANE), lambda b, n: (b * N + n, C_SM // LANE)),
            pl.BlockSpec((1, CONV_W - 1, GDN_CONV_CH), lambda b, n: (b, 0, 0)),
            pl.BlockSpec((CONV_W, GDN_CONV_CH), lambda b, n: (0, 0)),
            pl.BlockSpec((1, H), lambda b, n: (0, 0)),
            pl.BlockSpec((1, H), lambda b, n: (0, 0)),
            pl.BlockSpec((1, GDN_DV), lambda b, n: (0, 0)),
            pl.BlockSpec((1, H, GDN_DK, GDN_DV), lambda b, n: (b, 0, 0, 0)),
        ],
        out_specs=[
            pl.BlockSpec((C, GDN_VAL), lambda b, n: (b * N + n, 0)),
            pl.BlockSpec((1, H, GDN_DK, GDN_DV), lambda b, n: (b, 0, 0, 0)),
        ],
        out_shape=[
            jax.ShapeDtypeStruct((B * L, GDN_VAL), BF16),
            jax.ShapeDtypeStruct((B, H, GDN_DK, GDN_DV), F32),
        ],
        scratch_shapes=[
            pltpu.VMEM((H, GDN_DK, GDN_DV), F32),
            pltpu.VMEM((8 + C, GDN_CONV_CH), F32),
            pltpu.VMEM((GDN_GROUP * C, GDN_GROUP * C), BF16),
            pltpu.VMEM((GDN_GROUP * C, GDN_GROUP * GDN_DK), BF16),
            pltpu.VMEM((GDN_GROUP * C, 2 * GDN_GROUP * GDN_DK), BF16),
            pltpu.VMEM((C.bit_length() - 1, C, GDN_GROUP * C), BF16),
        ],
        compiler_params=_cparams(("parallel", "arbitrary")),
        name="gdn",
    )(proj, proj, proj, conv_state, conv_w, a_log.reshape(1, H), dt_bias.reshape(1, H),
      norm_w.reshape(1, GDN_DV), gdn_state)


def _rope_kernel(q_ref, k_ref, v_ref, iq_ref, ik_ref, cos_ref, sa_ref, sb_ref,
                 qo_ref, kf_ref, kb_ref, vb_ref, iqo_ref, ikf_ref, ikb_ref):
    cosf = cos_ref[...]
    sa = sa_ref[...]
    sb = sb_ref[...]

    def rope(x):
        return (x * cosf + pltpu.roll(x, LANE - ROPE_HALF, 1) * sa + pltpu.roll(x, ROPE_HALF, 1) * sb)

    for h in range(ATT_HEADS):
        qo_ref[0, h] = rope(q_ref[:, h * ATT_HD:(h + 1) * ATT_HD]).astype(BF16)
    for h in range(ATT_KV_HEADS):
        kr = rope(k_ref[:, h * ATT_HD:(h + 1) * ATT_HD])
        kf_ref[:, h * ATT_HD:(h + 1) * ATT_HD] = kr
        kb_ref[:, h * ATT_HD:(h + 1) * ATT_HD] = kr.astype(BF16)
    vb_ref[...] = v_ref[...].astype(BF16)
    for h in range(IDX_HEADS):
        iqo_ref[0, h] = rope(iq_ref[:, h * IDX_HD:(h + 1) * IDX_HD]).astype(BF16)
    ikr = rope(ik_ref[...])
    ikf_ref[...] = ikr
    ikb_ref[...] = ikr.astype(BF16)


def _rope_tables(L, pos0, reps):
    inv = jnp.power(ROPE_THETA, -jnp.arange(ROPE_HALF, dtype=F32) * (2.0 / ROPE_ROT))
    ang = (pos0 + jnp.arange(L)).astype(F32)[:, None] * inv[None, :]
    cos, sin = jnp.cos(ang), jnp.sin(ang)
    z16 = jnp.zeros((L, ROPE_HALF), F32)
    rest0 = jnp.zeros((L, ATT_HD - ROPE_ROT), F32)
    cosf = jnp.concatenate([cos, cos, jnp.ones((L, ATT_HD - ROPE_ROT), F32)], -1)
    sa = jnp.concatenate([-sin, z16, rest0], -1)
    sb = jnp.concatenate([z16, sin, rest0], -1)
    if reps > 1:
        cosf, sa, sb = (jnp.tile(t, (reps, 1)) for t in (cosf, sa, sb))
    return cosf, sa, sb


def _rope(proj, B, L, pos0, qb):
    T = B * L
    reps = 1
    tab_rows = L
    cosf, sa, sb = _rope_tables(L, pos0, reps)
    nq = L // qb
    tspec = pl.BlockSpec((qb, LANE), lambda i: (i % nq, 0))
    outs = pl.pallas_call(
        _rope_kernel,
        grid=(T // qb,),
        in_specs=[
            pl.BlockSpec((qb, ATT_Q), lambda i: (i, C_Q // ATT_Q)),
            pl.BlockSpec((qb, ATT_KV), lambda i: (i, C_K // ATT_KV)),
            pl.BlockSpec((qb, ATT_KV), lambda i: (i, C_V // ATT_KV)),
            pl.BlockSpec((qb, IDX_HEADS * IDX_HD), lambda i: (i, C_IQ // (IDX_HEADS * IDX_HD))),
            pl.BlockSpec((qb, IDX_HD), lambda i: (i, C_IK // IDX_HD)),
            tspec, tspec, tspec,
        ],
        out_specs=[
            pl.BlockSpec((1, ATT_HEADS, qb, ATT_HD), lambda i: (i, 0, 0, 0)),
            pl.BlockSpec((qb, ATT_KV), lambda i: (i, 0)),
            pl.BlockSpec((qb, ATT_KV), lambda i: (i, 0)),
            pl.BlockSpec((qb, ATT_KV), lambda i: (i, 0)),
            pl.BlockSpec((1, IDX_HEADS, qb, IDX_HD), lambda i: (i, 0, 0, 0)),
            pl.BlockSpec((qb, IDX_HD), lambda i: (i, 0)),
            pl.BlockSpec((qb, IDX_HD), lambda i: (i, 0)),
        ],
        out_shape=[
            jax.ShapeDtypeStruct((T // qb, ATT_HEADS, qb, ATT_HD), BF16),
            jax.ShapeDtypeStruct((T, ATT_KV), F32),
            jax.ShapeDtypeStruct((T, ATT_KV), BF16),
            jax.ShapeDtypeStruct((T, ATT_KV), BF16),
            jax.ShapeDtypeStruct((T // qb, IDX_HEADS, qb, IDX_HD), BF16),
            jax.ShapeDtypeStruct((T, IDX_HD), F32),
            jax.ShapeDtypeStruct((T, IDX_HD), BF16),
        ],
        compiler_params=_cparams(("parallel",)),
        name="rope",
    )(proj, proj, proj, proj, proj, cosf, sa, sb)
    return outs


def _dsa_kernel(iq_ref, sm_ref, q_ref, ik_ref, k_ref, v_ref, o_ref, sc_ref, *, qb, S, pos0, n_sel):
    j = pl.program_id(1)
    TK = TK_ATT
    q_last = pos0 + (j + 1) * qb - 1
    lim = jnp.minimum(((q_last // CHUNK) + 1) * CHUNK, S)
    nkt = (lim + TK - 1) // TK
    qpos = pos0 + j * qb + lax.broadcasted_iota(jnp.int32, (qb, 1), 0)
    qlim = jnp.minimum((jnp.right_shift(qpos, CHUNK.bit_length() - 1) + 1) * CHUNK, S)
    iw = sm_ref[:, SM_IW:SM_IW + IDX_HEADS] * ((IDX_HEADS ** -0.5) * (IDX_HD ** -0.5))
    HG = 4

    def score_tile(kt, carry):
        ks = pl.multiple_of(kt * TK, TK)
        ik_t = ik_ref[0, pl.ds(ks, TK), :]
        acc = jnp.zeros((qb, TK), F32)
        for hg in range(IDX_HEADS // HG):
            iq_g = iq_ref[0, hg * HG:(hg + 1) * HG].reshape(HG * qb, IDX_HD)
            r = jnp.maximum(_dot_nt(iq_g, ik_t), 0.0)
            for t in range(HG):
                hh = hg * HG + t
                acc = acc + iw[:, hh:hh + 1] * r[t * qb:(t + 1) * qb]
        kpos = ks + lax.broadcasted_iota(jnp.int32, (qb, TK), 1)
        sc_ref[kt] = jnp.where(kpos < qlim, acc, -jnp.inf)
        return carry

    lax.fori_loop(0, nkt, score_tile, 0)

    def lane_fold(m):
        p = m[:, 0:LANE]
        for t in range(1, TK // LANE):
            p = p + m[:, t * LANE:(t + 1) * LANE]
        return p

    def count_ge(x):
        def body(kt, c):
            return c + lane_fold(jnp.where(sc_ref[kt] >= x, 1.0, 0.0))
        part = lax.fori_loop(0, nkt, body, jnp.zeros((qb, LANE), F32))
        return jnp.sum(part, axis=1, keepdims=True)

    def minmax(kt, c):
        lo, hi = c
        t = sc_ref[kt]
        lo = jnp.minimum(lo, jnp.min(jnp.where(t > -jnp.inf, t, jnp.inf), axis=1, keepdims=True))
        hi = jnp.maximum(hi, jnp.max(t, axis=1, keepdims=True))
        return lo, hi

    lo0, hi0 = lax.fori_loop(0, nkt, minmax,
                             (jnp.full((qb, 1), jnp.inf, F32), jnp.full((qb, 1), -jnp.inf, F32)))
    kf = float(n_sel)
    cnt_all = count_ge(lo0)
    cnt_hi = count_ge(hi0)
    top_tied = cnt_hi >= kf
    fixed = (cnt_all <= kf) | top_tied
    lo_init = jnp.where(top_tied, hi0, lo0)

    def probe(lo, hi, done):
        mid = lo + (hi - lo) * 0.5
        stop = (mid <= lo) | (mid >= hi) | (done > 0.0)
        c = count_ge(mid)
        ge = c >= kf
        lo2 = jnp.where(stop, lo, jnp.where(ge, mid, lo))
        hi2 = jnp.where(stop, hi, jnp.where(ge, hi, mid))
        done2 = jnp.where(stop | (c == kf), 1.0, 0.0)
        return lo2, hi2, done2

    def cond(st):
        return jnp.logical_and(st[3] > 0, st[4] < 128)

    def body(st):
        lo, hi, done, _, it = st
        lo, hi, done = probe(lo, hi, done)
        lo, hi, done = probe(lo, hi, done)
        nact = jnp.sum(jnp.where(done > 0.0, 0, 1))
        return lo, hi, done, nact, it + 1

    done0 = jnp.where(fixed, 1.0, 0.0)
    thr, _, _, _, _ = lax.while_loop(cond, body, (lo_init, hi0, done0, jnp.int32(1), jnp.int32(0)))

    def to_bias(kt, carry):
        sc_ref[kt] = jnp.where(sc_ref[kt] >= thr, 0.0, NEG_BIG)
        return carry

    lax.fori_loop(0, nkt, to_bias, 0)

    R = ATT_REP
    for g in range(ATT_KV_HEADS):
        qg = q_ref[0, g * R:(g + 1) * R].reshape(R * qb, ATT_HD)

        def att_tile(kt, carry, g=g, qg=qg):
            m, l, acc = carry
            ks = pl.multiple_of(kt * TK, TK)
            k_t = k_ref[0, pl.ds(ks, TK), g * ATT_HD:(g + 1) * ATT_HD]
            v_t = v_ref[0, pl.ds(ks, TK), g * ATT_HD:(g + 1) * ATT_HD]
            s = _dot_nt(qg, k_t) * (ATT_HD ** -0.5)
            s = (s.reshape(R, qb, TK) + sc_ref[kt][None]).reshape(R * qb, TK)
            m_new = jnp.maximum(m, jnp.max(s, axis=1, keepdims=True))
            p = jnp.exp(s - m_new)
            a = jnp.exp(m - m_new)
            l = a * l + jnp.sum(p, axis=1, keepdims=True)
            acc = a * acc + _dot(p.astype(BF16), v_t)
            return m_new, l, acc

        m0 = jnp.full((R * qb, 1), NEG_BIG, F32)
        l0 = jnp.zeros((R * qb, 1), F32)
        a0 = jnp.zeros((R * qb, ATT_HD), F32)
        _, l, acc = lax.fori_loop(0, nkt, att_tile, (m0, l0, a0))
        out = acc / l
        for r in range(R):
            hh = g * R + r
            o_ref[:, hh * ATT_HD:(hh + 1) * ATT_HD] = out[r * qb:(r + 1) * qb].astype(o_ref.dtype)


def _dsa(iq_hm, proj, q_hm, ik_all, k_all, v_all, B, L, S, pos0):
    qb = min(Q_BLOCK, L)
    nq = L // qb
    S_pad = ik_all.shape[1]
    n_sel = min(TOPK_KEYS, S // 4)
    kern = functools.partial(_dsa_kernel, qb=qb, S=S, pos0=pos0, n_sel=n_sel)
    return pl.pallas_call(
        kern,
        grid=(B, nq),
        in_specs=[
            pl.BlockSpec((1, IDX_HEADS, qb, IDX_HD), lambda b, j: (b * nq + j, 0, 0, 0)),
            pl.BlockSpec((qb, LANE), lambda b, j: (b * nq + j, C_SM // LANE)),
            pl.BlockSpec((1, ATT_HEADS, qb, ATT_HD), lambda b, j: (b * nq + j, 0, 0, 0)),
            pl.BlockSpec((1, S_pad, IDX_HD), lambda b, j: (b, 0, 0)),
            pl.BlockSpec((1, S_pad, ATT_KV), lambda b, j: (b, 0, 0)),
            pl.BlockSpec((1, S_pad, ATT_KV), lambda b, j: (b, 0, 0)),
        ],
        out_specs=pl.BlockSpec((qb, ATT_Q), lambda b, j: (b * nq + j, 0)),
        out_shape=jax.ShapeDtypeStruct((B * L, ATT_Q), BF16),
        scratch_shapes=[pltpu.VMEM((S_pad // TK_ATT, qb, TK_ATT), F32)],
        compiler_params=_cparams(("parallel", "arbitrary")),
        name="dsa",
    )(iq_hm, proj, q_hm, ik_all, k_all, v_all)


def _merge_kernel(oa_ref, ob_ref, wa_ref, wb_ref, ga_ref, gb_ref, o_ref):
    ya = _dot(oa_ref[...], wa_ref[...])
    yb = _dot(ob_ref[...], wb_ref[...])
    o_ref[...] = (_sigmoid(ga_ref[...]) * ya + _sigmoid(gb_ref[...]) * yb).astype(o_ref.dtype)


def _merge(oa, ob, wa, wb, proj):
    T = oa.shape[0]
    tm = min(512, T)
    tn = TN_MIX
    return pl.pallas_call(
        _merge_kernel,
        grid=(T // tm, D_MODEL // tn),
        in_specs=[
            pl.BlockSpec((tm, GDN_VAL), lambda i, j: (i, 0)),
            pl.BlockSpec((tm, ATT_Q), lambda i, j: (i, 0)),
            pl.BlockSpec((GDN_VAL, tn), lambda i, j: (0, j)),
            pl.BlockSpec((ATT_Q, tn), lambda i, j: (0, j)),
            pl.BlockSpec((tm, tn), lambda i, j: (i, C_GA // tn + j)),
            pl.BlockSpec((tm, tn), lambda i, j: (i, C_GB // tn + j)),
        ],
        out_specs=pl.BlockSpec((tm, tn), lambda i, j: (i, j)),
        out_shape=jax.ShapeDtypeStruct((T, D_MODEL), BF16),
        compiler_params=_cparams(("parallel", "arbitrary")),
        name="merge",
    )(oa, ob, wa, wb, proj, proj)


def _oproj_kernel(m_ref, w_ref, x_ref, g1_ref, sc2_ref, sh2_ref, lg_ref, lb_ref, x1_ref, h2_ref, acc_ref, *, nj):
    j = pl.program_id(1)
    acc_ref[j] = _dot(m_ref[...], w_ref[...])

    @pl.when(j == nj - 1)
    def _():
        bb, tl, D = x_ref.shape
        tn = acc_ref.shape[2]
        sl = lambda t: slice(t * tn, (t + 1) * tn)
        s1 = jnp.zeros((bb, tl, 1), F32)
        for t in range(nj):
            v = ALPHA * x_ref[:, :, sl(t)] + g1_ref[:, :, sl(t)] * acc_ref[t].reshape(bb, tl, tn)
            acc_ref[t] = v.reshape(bb * tl, tn)
            s1 = s1 + jnp.sum(v, -1, keepdims=True)
        mu = s1 * (1.0 / D)
        s2 = jnp.zeros((bb, tl, 1), F32)
        for t in range(nj):
            d = acc_ref[t].reshape(bb, tl, tn) - mu
            s2 = s2 + jnp.sum(d * d, -1, keepdims=True)
        rstd = lax.rsqrt(s2 * (1.0 / D) + LN_EPS)
        for t in range(nj):
            x1 = (acc_ref[t].reshape(bb, tl, tn) - mu) * rstd * lg_ref[:, :, sl(t)] + lb_ref[:, :, sl(t)]
            x1_ref[:, :, sl(t)] = x1
            h2_ref[:, sl(t)] = (x1 * (1.0 + sc2_ref[:, :, sl(t)]) + sh2_ref[:, :, sl(t)]).reshape(bb * tl, tn)


def _oproj(m, w_o, x3, mod3, row0, ln_g, ln_b):
    B, L, D = x3.shape
    bb, tl = _row_tiling(B, L, TM_MIX)
    nl = L // tl
    rows = bb * tl
    mrow = row0 // bb
    tn = TN_MIX
    nj = D // tn
    kern = functools.partial(_oproj_kernel, nj=nj)
    modspec = lambda c: pl.BlockSpec((bb, 1, D), lambda i, j: (mrow + i // nl, 0, c))
    return pl.pallas_call(
        kern,
        grid=((B // bb) * nl, nj),
        in_specs=[
            pl.BlockSpec((rows, D), lambda i, j: (i, 0)),
            pl.BlockSpec((D, tn), lambda i, j: (0, j)),
            pl.BlockSpec((bb, tl, D), lambda i, j: (i // nl, i % nl, 0)),
            modspec(2), modspec(4), modspec(3),
            pl.BlockSpec((1, 1, D), lambda i, j: (0, 0, 0)),
            pl.BlockSpec((1, 1, D), lambda i, j: (0, 0, 0)),
        ],
        out_specs=[
            pl.BlockSpec((bb, tl, D), lambda i, j: (i // nl, i % nl, 0)),
            pl.BlockSpec((rows, D), lambda i, j: (i, 0)),
        ],
        out_shape=[
            jax.ShapeDtypeStruct((B, L, D), F32),
            jax.ShapeDtypeStruct((B * L, D), F32),
        ],
        scratch_shapes=[pltpu.VMEM((nj, rows, tn), F32)],
        compiler_params=_cparams(("parallel", "arbitrary")),
        name="oproj_ln1",
    )(m, w_o, x3, mod3, mod3, mod3, ln_g.reshape(1, 1, D), ln_b.reshape(1, 1, D))


def _router_kernel(h_ref, w_ref, b_ref, id_ref, wt_ref):
    logits = jnp.dot(h_ref[...], w_ref[...], precision=lax.Precision.HIGHEST,
                     preferred_element_type=F32) + b_ref[...]
    rows = logits.shape[0]
    lane = lax.broadcasted_iota(jnp.int32, (rows, LANE), 1)
    is_g = lane < N_GROUPS
    gl = jnp.where(is_g, logits, -jnp.inf)
    gmax = jnp.max(gl, axis=1, keepdims=True)
    g_sel = jnp.min(jnp.where(gl == gmax, lane, LANE), axis=1, keepdims=True)
    p_grp = 1.0 / jnp.sum(jnp.where(is_g, jnp.exp(gl - gmax), 0.0), axis=1, keepdims=True)
    e_lo = N_GROUPS + g_sel * EXP_PER_GROUP
    in_g = (lane >= e_lo) & (lane < e_lo + EXP_PER_GROUP)
    el = jnp.where(in_g, logits, -jnp.inf)
    m1 = jnp.max(el, axis=1, keepdims=True)
    i1 = jnp.min(jnp.where(el == m1, lane, LANE), axis=1, keepdims=True)
    el2 = jnp.where(lane == i1, -jnp.inf, el)
    m2 = jnp.max(el2, axis=1, keepdims=True)
    i2 = jnp.min(jnp.where(el2 == m2, lane, LANE), axis=1, keepdims=True)
    e21 = jnp.exp(m2 - m1)
    w1 = p_grp / (1.0 + e21)
    w2 = p_grp * e21 / (1.0 + e21)
    id_ref[...] = jnp.where(lane == 0, i1 - N_GROUPS, jnp.where(lane == 1, i2 - N_GROUPS, 0))
    wt_ref[...] = jnp.where(lane == 0, w1, jnp.where(lane == 1, w2, 0.0))


def _router(h2, w_r, b_r):
    T, D = h2.shape
    tm = min(512, T)
    return pl.pallas_call(
        _router_kernel,
        grid=(T // tm,),
        in_specs=[
            pl.BlockSpec((tm, D), lambda i: (i, 0)),
            pl.BlockSpec((D, LANE), lambda i: (0, 0)),
            pl.BlockSpec((1, LANE), lambda i: (0, 0)),
        ],
        out_specs=[pl.BlockSpec((tm, LANE), lambda i: (i, 0)), pl.BlockSpec((tm, LANE), lambda i: (i, 0))],
        out_shape=[jax.ShapeDtypeStruct((T, LANE), jnp.int32), jax.ShapeDtypeStruct((T, LANE), F32)],
        compiler_params=_cparams(("parallel",)),
        name="router",
    )(h2, w_r, b_r)


def _expert_kernel(blk_e_ref, tok_ref, nused_ref, h_hbm, wg_ref, wu_ref, wd_ref, o_ref,
                   xbuf_ref, xb16_ref, acc_ref, sem_ref, *, BM, nfc):
    i = pl.program_id(0)
    c = pl.program_id(1)
    nused = nused_ref[0]
    slot = lax.rem(i, 2)

    def row_copy(blk, r, sl):
        tok = tok_ref[blk * BM + r]
        return pltpu.make_async_copy(h_hbm.at[pl.ds(tok, 1)], xbuf_ref.at[sl, pl.ds(r, 1)], sem_ref.at[sl])

    def start_gather(blk, sl):
        def body(r, carry):
            row_copy(blk, r, sl).start()
            return carry
        lax.fori_loop(0, BM, body, 0)

    def wait_gather(blk, sl):
        def body(r, carry):
            row_copy(blk, r, sl).wait()
            return carry
        lax.fori_loop(0, BM, body, 0)

    @pl.when(jnp.logical_and(c == 0, i < nused))
    def _():
        @pl.when(i == 0)
        def _():
            start_gather(0, 0)

        @pl.when(i + 1 < nused)
        def _():
            start_gather(i + 1, 1 - slot)

        wait_gather(i, slot)
        xb16_ref[...] = xbuf_ref[slot].astype(BF16)

    @pl.when(i < nused)
    def _():
        x = xb16_ref[...]
        gate = _dot(x, wg_ref[0])
        up = _dot(x, wu_ref[0])
        act = (_silu(gate) * up).astype(BF16)
        part = _dot(act, wd_ref[0])

        @pl.when(c == 0)
        def _():
            acc_ref[...] = part

        @pl.when(c > 0)
        def _():
            acc_ref[...] += part

        @pl.when(c == nfc - 1)
        def _():
            o_ref[...] = acc_ref[...]

    @pl.when(jnp.logical_and(i >= nused, c == nfc - 1))
    def _():
        o_ref[...] = jnp.zeros_like(o_ref)


def _experts(h2, blk_e, slot_tok, nused, w_up16, w_down16, BM):
    T, D = h2.shape
    P = slot_tok.shape[0]
    nblk = P // BM
    fc = FC_MOE
    nfc = D_EXPERT // fc
    kern = functools.partial(_expert_kernel, BM=BM, nfc=nfc)

    def eidx(i, nu):
        return jnp.minimum(i, nu[0] - 1)

    def cidx(i, c, nu):
        return jnp.where(i < nu[0], c, nfc - 1)

    grid_spec = pltpu.PrefetchScalarGridSpec(
        num_scalar_prefetch=3,
        grid=(nblk, nfc),
        in_specs=[
            pl.BlockSpec(memory_space=pl.ANY),
            pl.BlockSpec((1, D, fc), lambda i, c, be, tk, nu: (be[eidx(i, nu)], 0, cidx(i, c, nu))),
            pl.BlockSpec((1, D, fc), lambda i, c, be, tk, nu: (be[eidx(i, nu)], 0, nfc + cidx(i, c, nu))),
            pl.BlockSpec((1, fc, D), lambda i, c, be, tk, nu: (be[eidx(i, nu)], cidx(i, c, nu), 0)),
        ],
        out_specs=pl.BlockSpec((BM, D), lambda i, c, be, tk, nu: (i, 0)),
        scratch_shapes=[
            pltpu.VMEM((2, BM, D), F32),
            pltpu.VMEM((BM, D), BF16),
            pltpu.VMEM((BM, D), F32),
            pltpu.SemaphoreType.DMA((2,)),
        ],
    )
    return pl.pallas_call(
        kern,
        grid_spec=grid_spec,
        out_shape=jax.ShapeDtypeStruct((P, D), F32),
        compiler_params=_cparams(("arbitrary", "arbitrary")),
        name="experts",
    )(blk_e, slot_tok, nused, h2, w_up16, w_up16, w_down16)


def _route_metadata(ids, T, BM):
    expert = ids[:, 0:2].reshape(-1)
    A = 2 * T
    onehot = (expert[:, None] == jnp.arange(N_EXPERTS, dtype=jnp.int32)[None, :]).astype(jnp.int32)
    csum = jnp.cumsum(onehot, axis=0)
    counts = csum[-1]
    rank = jnp.sum((csum - onehot) * onehot, axis=1)
    padded = ((counts + BM - 1) // BM) * BM
    pend = jnp.cumsum(padded)
    pstart = pend - padded
    dest = (pstart[expert] + rank).astype(jnp.int32)
    P = ((A + BM - 1) // BM) * BM + N_EXPERTS * BM
    nblk = P // BM
    tok = jnp.arange(A, dtype=jnp.int32) // 2
    slot_tok = jnp.zeros((P,), jnp.int32).at[dest].set(tok)
    blk_e = jnp.clip(jnp.searchsorted(pend, jnp.arange(nblk, dtype=jnp.int32) * BM, side="right"),
                     0, N_EXPERTS - 1).astype(jnp.int32)
    nused = (pend[-1] // BM).astype(jnp.int32).reshape(1)
    return dest, slot_tok, blk_e, nused


def _final_kernel(dest_ref, yb_hbm, x1_ref, g2_ref, wt_ref, lg_ref, lb_ref, o_ref, buf_ref, sem_ref, *, rows, ntiles):
    i = pl.program_id(0)
    slot = lax.rem(i, 2)

    def row_copy(tile, r, sl):
        d = dest_ref[tile * 2 * rows + r]
        return pltpu.make_async_copy(yb_hbm.at[pl.ds(d, 1)], buf_ref.at[sl, pl.ds(r, 1)], sem_ref.at[sl])

    def start_gather(tile, sl):
        def body(r, carry):
            row_copy(tile, r, sl).start()
            return carry
        lax.fori_loop(0, 2 * rows, body, 0)

    def wait_gather(tile, sl):
        def body(r, carry):
            row_copy(tile, r, sl).wait()
            return carry
        lax.fori_loop(0, 2 * rows, body, 0)

    @pl.when(i == 0)
    def _():
        start_gather(0, 0)

    @pl.when(i + 1 < ntiles)
    def _():
        start_gather(i + 1, 1 - slot)

    wait_gather(i, slot)
    bb, tl, D = x1_ref.shape
    tn = TN_MIX
    sl = lambda t: slice(t * tn, (t + 1) * tn)
    w0 = wt_ref[:, 0:1]
    w1 = wt_ref[:, 1:2]
    s1 = jnp.zeros((bb, tl, 1), F32)
    for t in range(D // tn):
        f = w0 * buf_ref[slot, 0:rows, sl(t)] + w1 * buf_ref[slot, rows:2 * rows, sl(t)]
        v = ALPHA * x1_ref[:, :, sl(t)] + g2_ref[:, :, sl(t)] * f.reshape(bb, tl, tn)
        o_ref[:, :, sl(t)] = v
        s1 = s1 + jnp.sum(v, -1, keepdims=True)
    mu = s1 * (1.0 / D)
    s2 = jnp.zeros((bb, tl, 1), F32)
    for t in range(D // tn):
        d = o_ref[:, :, sl(t)] - mu
        s2 = s2 + jnp.sum(d * d, -1, keepdims=True)
    rstd = lax.rsqrt(s2 * (1.0 / D) + LN_EPS)
    for t in range(D // tn):
        o_ref[:, :, sl(t)] = (o_ref[:, :, sl(t)] - mu) * rstd * lg_ref[:, :, sl(t)] + lb_ref[:, :, sl(t)]


def _final(dest_tiles, yb, x1, mod3, row0, wts, ln_g, ln_b):
    B, L, D = x1.shape
    bb, tl = _row_tiling(B, L, TM_MIX)
    nl = L // tl
    rows = bb * tl
    mrow = row0 // bb
    ntiles = (B // bb) * nl
    kern = functools.partial(_final_kernel, rows=rows, ntiles=ntiles)
    grid_spec = pltpu.PrefetchScalarGridSpec(
        num_scalar_prefetch=1,
        grid=(ntiles,),
        in_specs=[
            pl.BlockSpec(memory_space=pl.ANY),
            pl.BlockSpec((bb, tl, D), lambda i, d: (i // nl, i % nl, 0)),
            pl.BlockSpec((bb, 1, D), lambda i, d: (mrow + i // nl, 0, 5)),
            pl.BlockSpec((rows, LANE), lambda i, d: (i, 0)),
            pl.BlockSpec((1, 1, D), lambda i, d: (0, 0, 0)),
            pl.BlockSpec((1, 1, D), lambda i, d: (0, 0, 0)),
        ],
        out_specs=pl.BlockSpec((bb, tl, D), lambda i, d: (i // nl, i % nl, 0)),
        scratch_shapes=[
            pltpu.VMEM((2, 2 * rows, D), F32),
            pltpu.SemaphoreType.DMA((2,)),
        ],
    )
    return pl.pallas_call(
        kern,
        grid_spec=grid_spec,
        out_shape=jax.ShapeDtypeStruct((B, L, D), F32),
        compiler_params=_cparams(("arbitrary",)),
        name="combine_ln2",
    )(dest_tiles, yb, x1, mod3, wts, ln_g.reshape(1, 1, D), ln_b.reshape(1, 1, D))


def _pad_keys(t, S_pad):
    S = t.shape[1]
    if S == S_pad:
        return t
    return jnp.pad(t, ((0, 0), (0, S_pad - S), (0, 0)))


def _layer(x3, mod3, row0, conv_state, gdn_state, past_k, past_v, past_ik, wp, moe_bm):
    B, L, D = x3.shape
    T = B * L
    pos0 = past_k.shape[1]
    S = pos0 + L
    proj = _inproj(x3, mod3, row0, wp["w_in"])

    oa, new_gdn = _gdn(proj, B, L, conv_state, gdn_state, wp["conv_w"], wp["a_log"], wp["dt_bias"],
                       wp["gdn_norm_w"])
    new_conv = proj[:, C_QKV:C_QKV + GDN_CONV_CH].reshape(B, L, GDN_CONV_CH)[:, L - (CONV_W - 1):]

    qb = min(Q_BLOCK, L)
    q_hm, kf, k16, v16, iq_hm, ikf, ik16 = _rope(proj, B, L, pos0, qb)
    S_pad = ((S + TK_ATT - 1) // TK_ATT) * TK_ATT
    k_all = _pad_keys(jnp.concatenate([past_k.reshape(B, pos0, ATT_KV).astype(BF16), k16.reshape(B, L, ATT_KV)], 1), S_pad)
    v_all = _pad_keys(jnp.concatenate([past_v.reshape(B, pos0, ATT_KV).astype(BF16), v16.reshape(B, L, ATT_KV)], 1), S_pad)
    ik_all = _pad_keys(jnp.concatenate([past_ik.astype(BF16), ik16.reshape(B, L, IDX_HD)], 1), S_pad)
    ob = _dsa(iq_hm, proj, q_hm, ik_all, k_all, v_all, B, L, S, pos0)

    m = _merge(oa, ob, wp["w_br_a"], wp["w_br_b"], proj)
    x1, h2 = _oproj(m, wp["w_o"], x3, mod3, row0, wp["ln1_g"], wp["ln1_b"])

    ids, wts = _router(h2, wp["w_r"], wp["b_r"])
    dest, slot_tok, blk_e, nused = _route_metadata(ids, T, moe_bm)
    yb = _experts(h2, blk_e, slot_tok, nused, wp["w_up"], wp["w_down"], moe_bm)
    bb, tl = _row_tiling(B, L, TM_MIX)
    rows = bb * tl
    dest_tiles = dest.reshape(T // rows, rows, 2).transpose(0, 2, 1).reshape(-1)
    y = _final(dest_tiles, yb, x1, mod3, row0, wts, wp["ln2_g"], wp["ln2_b"])

    kb_out = kf.reshape(B, L, ATT_KV_HEADS, ATT_HD)
    vb_out = proj[:, C_V:C_V + ATT_KV].reshape(B, L, ATT_KV_HEADS, ATT_HD)
    ik_out = ikf.reshape(B, L, IDX_HD)
    return y, new_conv, new_gdn, kb_out, vb_out, ik_out


def _permute_w_in(w_in):
    sizes = (GDN_CONV_CH, GDN_HEADS, GDN_HEADS, GDN_VAL, ATT_Q, ATT_KV, ATT_KV, IDX_HEADS * IDX_HD, IDX_HD,
             IDX_HEADS, 2 * D_MODEL)
    parts, off = [], 0
    for s in sizes:
        parts.append(w_in[:, off:off + s])
        off += s
    qkv, a, b, z, q, k, v, iq, ik, iw, gates = parts
    pad = jnp.zeros((w_in.shape[0], SM_W - 2 * GDN_HEADS - IDX_HEADS), w_in.dtype)
    return jnp.concatenate([qkv, z, iq, q, k, v, ik, a, b, iw, pad, gates], axis=1).astype(BF16)


def kernel(x_prompt, x_sample, c_prompt, c_sample, state_conv, state_gdn, cache_k, cache_v, cache_idx_k,
           w_ada, b_ada, w_in, conv_w, a_log, dt_bias, gdn_norm_w, w_br_a, w_br_b, w_o,
           ln1_g, ln1_b, ln2_g, ln2_b, w_grp, b_grp, w_rtr, b_rtr, w_up, w_down):
    Bp, Lp, D = x_prompt.shape
    Bs, Ls, _ = x_sample.shape
    l = 0
    row0_p = Bs
    nrows = ((Bs + Bp + 7) // 8) * 8
    c_all = jnp.concatenate([c_sample, c_prompt, jnp.zeros((nrows - Bs - Bp, D), F32)], 0)
    mod = _ada(c_all, w_ada[l], b_ada[l])
    mod3 = mod.reshape(nrows, 1, 6 * D)

    nr = LANE - N_GROUPS - N_EXPERTS
    wp = dict(
        w_in=_permute_w_in(w_in[l]),
        conv_w=conv_w[l], a_log=a_log[l], dt_bias=dt_bias[l], gdn_norm_w=gdn_norm_w[l],
        w_br_a=w_br_a[l].astype(BF16), w_br_b=w_br_b[l].astype(BF16), w_o=w_o[l].astype(BF16),
        ln1_g=ln1_g[l], ln1_b=ln1_b[l], ln2_g=ln2_g[l], ln2_b=ln2_b[l],
        w_r=jnp.concatenate([w_grp[l], w_rtr[l], jnp.zeros((D, nr), F32)], 1),
        b_r=jnp.concatenate([b_grp[l], b_rtr[l], jnp.zeros((nr,), F32)]).reshape(1, LANE),
        w_up=w_up[l].astype(BF16), w_down=w_down[l].astype(BF16),
    )

    zc = jnp.zeros((Bp, CONV_W - 1, GDN_CONV_CH), F32)
    zs = jnp.zeros((Bp, GDN_HEADS, GDN_DK, GDN_DV), F32)
    zk = jnp.zeros((Bp, 0, ATT_KV_HEADS, ATT_HD), F32)
    zik = jnp.zeros((Bp, 0, IDX_HD), F32)
    yp, c1, g1, k1, v1, i1 = _layer(x_prompt, mod3, row0_p, zc, zs, zk, zk, zik, wp, 256)
    ys, c2, g2, k2, v2, i2 = _layer(x_sample, mod3, 0, state_conv[l], state_gdn[l], cache_k[l], cache_v[l],
                                    cache_idx_k[l], wp, 128)
    st = lambda t: t[None]
    return (yp, ys, st(c1), st(g1), st(k1), st(v1), st(i1), st(c2), st(g2), st(k2), st(v2), st(i2))
```

```python
import functools

import jax
import jax.numpy as jnp
from jax import lax
from jax.experimental import pallas as pl
from jax.experimental.pallas import tpu as pltpu

F32 = jnp.float32
BF16 = jnp.bfloat16

D_MODEL = 4096
CHUNK = 64
GDN_HEADS = D_MODEL // 256
GDN_DK = 128
GDN_DV = 128
GDN_KEY = GDN_HEADS * GDN_DK
GDN_VAL = GDN_HEADS * GDN_DV
GDN_CONV_CH = 2 * GDN_KEY + GDN_VAL
CONV_W = 4
ATT_HEADS = D_MODEL // 256
ATT_KV_HEADS = ATT_HEADS // 4
ATT_REP = ATT_HEADS // ATT_KV_HEADS
ATT_HD = 128
ATT_Q = ATT_HEADS * ATT_HD
ATT_KV = ATT_KV_HEADS * ATT_HD
IDX_HEADS = D_MODEL // 128
IDX_HD = 128
TOPK_KEYS = 256
Q_BLOCK = 128
ROPE_THETA = 500000.0
ROPE_ROT = ATT_HD // 4
ROPE_HALF = ROPE_ROT // 2
N_GROUPS = 4
EXP_PER_GROUP = 8
N_EXPERTS = N_GROUPS * EXP_PER_GROUP
D_EXPERT = D_MODEL // 4
DEPTH = 1
ALPHA = (2.0 * DEPTH) ** 0.25
LN_EPS = 1e-5
RMS_EPS = 1e-6

C_QKV = 0
C_Z = C_QKV + GDN_CONV_CH
C_IQ = C_Z + GDN_VAL
C_Q = C_IQ + IDX_HEADS * IDX_HD
C_K = C_Q + ATT_Q
C_V = C_K + ATT_KV
C_IK = C_V + ATT_KV
C_SM = C_IK + IDX_HD
SM_W = 384
C_GA = C_SM + SM_W
C_GB = C_GA + D_MODEL
NP = C_GB + D_MODEL
SM_A, SM_B, SM_IW = 0, GDN_HEADS, 2 * GDN_HEADS

LANE = 128
NEG_BIG = -1e30
VMEM_LIMIT = 56 * 1024 * 1024

TN_IN = 512
TM_IN = 1024
TM_MIX = 256
TN_MIX = 512
TK_ATT = 512
FC_MOE = 256
MOE_BM = 256


def _cparams(sem):
    return pltpu.CompilerParams(dimension_semantics=sem, vmem_limit_bytes=VMEM_LIMIT)


def _dot(a, b):
    return jnp.dot(a, b, preferred_element_type=F32)


def _dot_nt(a, b):
    return lax.dot_general(a, b, (((1,), (1,)), ((), ())), preferred_element_type=F32)


def _dot_tn(a, b):
    return lax.dot_general(a, b, (((0,), (0,)), ((), ())), preferred_element_type=F32)


def _split_bf16(a):
    hi = a.astype(BF16)
    lo = (a - hi.astype(F32)).astype(BF16)
    return hi, lo


def _mm3(a, b, dot=_dot):
    ah, al = _split_bf16(a)
    bh, bl = _split_bf16(b)
    return dot(ah, bh) + (dot(ah, bl) + dot(al, bh))


def _mm1(a, b):
    return _dot(a.astype(BF16), b.astype(BF16))


def _mm1_nt(a, b):
    return _dot_nt(a.astype(BF16), b.astype(BF16))


def _sigmoid(x):
    return 1.0 / (1.0 + jnp.exp(-x))


def _silu(x):
    return x * _sigmoid(x)


def _softplus(x):
    return jnp.maximum(x, 0.0) + jnp.log(1.0 + jnp.exp(-jnp.abs(x)))


def _ada_kernel(c_ref, w_ref, b_ref, o_ref):
    s = _silu(c_ref[...])
    o_ref[...] = _dot(s.astype(BF16), w_ref[...].astype(BF16)) + b_ref[...]


def _ada(c_all, w_ada, b_ada):
    R, D = c_all.shape
    N = w_ada.shape[1]
    tn = 512
    return pl.pallas_call(
        _ada_kernel,
        grid=(N // tn,),
        in_specs=[
            pl.BlockSpec((R, D), lambda j: (0, 0)),
            pl.BlockSpec((D, tn), lambda j: (0, j)),
            pl.BlockSpec((1, tn), lambda j: (0, j)),
        ],
        out_specs=pl.BlockSpec((R, tn), lambda j: (0, j)),
        out_shape=jax.ShapeDtypeStruct((R, N), F32),
        compiler_params=_cparams(("parallel",)),
        name="ada",
    )(c_all, w_ada, b_ada.reshape(1, N))


def _inproj_kernel(x_ref, sc_ref, sh_ref, w_ref, o_ref, h_ref):
    @pl.when(pl.program_id(1) == 0)
    def _():
        h = x_ref[...] * (1.0 + sc_ref[...]) + sh_ref[...]
        h_ref[...] = h.reshape(h_ref.shape).astype(BF16)

    o_ref[...] = _dot(h_ref[...], w_ref[...])


def _row_tiling(B, L, tm):
    if L >= tm:
        return 1, tm
    bb = max(1, min(B, tm // L))
    while B % bb:
        bb -= 1
    return bb, L


def _inproj(x3, mod3, row0, w_perm):
    B, L, D = x3.shape
    bb, tl = _row_tiling(B, L, TM_IN)
    nl = L // tl
    rows = bb * tl
    mrow = row0 // bb
    return pl.pallas_call(
        _inproj_kernel,
        grid=((B // bb) * nl, NP // TN_IN),
        in_specs=[
            pl.BlockSpec((bb, tl, D), lambda i, j: (i // nl, i % nl, 0), pipeline_mode=pl.Buffered(1)),
            pl.BlockSpec((bb, 1, D), lambda i, j: (mrow + i // nl, 0, 1)),
            pl.BlockSpec((bb, 1, D), lambda i, j: (mrow + i // nl, 0, 0)),
            pl.BlockSpec((D, TN_IN), lambda i, j: (0, j)),
        ],
        out_specs=pl.BlockSpec((rows, TN_IN), lambda i, j: (i, j)),
        out_shape=jax.ShapeDtypeStruct((B * L, NP), F32),
        scratch_shapes=[pltpu.VMEM((rows, D), BF16)],
        compiler_params=_cparams(("parallel", "arbitrary")),
        name="inproj",
    )(x3, mod3, mod3, w_perm)


GDN_GROUP = 4


def _split3_bf16(a):
    hi = a.astype(BF16)
    r = a - hi.astype(F32)
    mid = r.astype(BF16)
    lo = (r - mid.astype(F32)).astype(BF16)
    return hi, mid, lo


def _dot_exact01(a, b01, a_is_01=False):
    if a_is_01:
        h, m, l = _split3_bf16(b01)
        return _dot(a, h) + (_dot(a, m) + _dot(a, l))
    h, m, l = _split3_bf16(a)
    return _dot(h, b01) + (_dot(m, b01) + _dot(l, b01))


def _mm3p(a_hl, b_hl, dot=_dot):
    (ah, al), (bh, bl) = a_hl, b_hl
    return dot(ah, bh) + (dot(ah, bl) + dot(al, bh))


def _block_rows_hl(x_hl, nblk, mask01):
    return tuple(jnp.concatenate([p] * nblk, axis=0) * mask01 for p in x_hl)


def _gdn_kernel(qkv_ref, z_ref, sm_ref, cst_ref, cw_ref, alr_ref, dtr_ref, nw_ref, s0_ref,
                o_ref, sfin_ref, s_ref, ext_ref, mt_ref, mk_ref, mx_ref, ml_ref, *, C, nchunks):
    n = pl.program_id(1)
    H = GDN_HEADS
    G = GDN_GROUP
    NG = H // G
    GC = G * C
    PADR = 8
    logc = C.bit_length() - 1

    @pl.when(n == 0)
    def _():
        s_ref[...] = s0_ref[0]
        ext_ref[PADR - (CONV_W - 1):PADR, :] = cst_ref[0]

    ext_ref[PADR:PADR + C, :] = qkv_ref[...]
    cw = cw_ref[...]
    y = ext_ref[PADR - 3:PADR - 3 + C, :] * cw[0:1]
    for jw in range(1, CONV_W):
        y = y + ext_ref[PADR - 3 + jw:PADR - 3 + jw + C, :] * cw[jw:jw + 1]
    y = _silu(y)
    hist = ext_ref[PADR + C - (CONV_W - 1):PADR + C, :]
    ext_ref[PADR - (CONV_W - 1):PADR, :] = hist

    sm = sm_ref[...]
    g_col = -jnp.exp(alr_ref[...]) * _softplus(sm[:, SM_A:SM_A + H] + dtr_ref[...])
    beta_col = _sigmoid(sm[:, SM_B:SM_B + H])

    ii = lax.broadcasted_iota(jnp.int32, (C, C), 0)
    jj = lax.broadcasted_iota(jnp.int32, (C, C), 1)
    tril01 = jnp.where(jj <= ii, 1.0, 0.0).astype(BF16)
    gc_col = _dot_exact01(tril01, g_col, a_is_01=True)
    egc_col = jnp.exp(gc_col)
    gc_last = gc_col[C - 1:C, :]
    ekd_col = jnp.exp(gc_last - gc_col)
    egl = jnp.exp(gc_last)

    W = H * C
    hrow = lax.broadcasted_iota(jnp.int32, (H, W), 0)
    hlane = jnp.right_shift(lax.broadcasted_iota(jnp.int32, (H, W), 1), logc)
    e_seg = jnp.where(hrow == hlane, 1.0, 0.0).astype(BF16)
    gseg = _dot_exact01(gc_col, e_seg)
    ri = lax.broadcasted_iota(jnp.int32, (C, W), 0)
    cj = jnp.bitwise_and(lax.broadcasted_iota(jnp.int32, (C, W), 1), C - 1)
    grow = jnp.sum(jnp.where(ri == cj, gseg, 0.0), axis=0, keepdims=True)
    decay_all = jnp.where(cj <= ri, jnp.exp(gseg - grow), 0.0)

    gi = lax.broadcasted_iota(jnp.int32, (C, GC), 0)
    gj = jnp.bitwise_and(lax.broadcasted_iota(jnp.int32, (C, GC), 1), C - 1)
    strict = gj < gi
    eye = jnp.where(gi == gj, 1.0, 0.0)
    nlev = logc

    def lev_mask(lev):
        return ((jnp.right_shift(gi, lev + 1) == jnp.right_shift(gj, lev + 1))
                & (jnp.bitwise_and(jnp.right_shift(gi, lev), 1) == 1)
                & (jnp.bitwise_and(jnp.right_shift(gj, lev), 1) == 0))

    @pl.when(n == 0)
    def _():
        def own(ncols, col_head):
            r = jnp.right_shift(lax.broadcasted_iota(jnp.int32, (G * C, ncols), 0), logc)
            c = col_head(lax.broadcasted_iota(jnp.int32, (G * C, ncols), 1))
            return jnp.where(r == c, 1.0, 0.0).astype(BF16)

        mt_ref[...] = own(GC, lambda c: jnp.right_shift(c, logc))
        mk_ref[...] = own(G * GDN_DK, lambda c: jnp.right_shift(c, 7))
        mx_ref[...] = own(2 * G * GDN_DK, lambda c: jnp.bitwise_and(jnp.right_shift(c, 7), G - 1))
        for lev in range(1, nlev):
            ml_ref[lev] = jnp.where(lev_mask(lev), 1.0, 0.0).astype(BF16)

    nw = nw_ref[...]
    z = z_ref[...]

    lmats, lm_hls, intras, rhss, qds, kds = [], [], [], [], [], []
    for g in range(NG):
        qs, ks, kbs, vbs, kes = [], [], [], [], []
        for t in range(G):
            h = g * G + t
            qh = y[:, h * GDN_DK:(h + 1) * GDN_DK]
            kh = y[:, GDN_KEY + h * GDN_DK:GDN_KEY + (h + 1) * GDN_DK]
            vh = y[:, 2 * GDN_KEY + h * GDN_DV:2 * GDN_KEY + (h + 1) * GDN_DV]
            q = qh * lax.rsqrt(jnp.sum(qh * qh, -1, keepdims=True) + RMS_EPS) * (GDN_DK ** -0.5)
            k = kh * lax.rsqrt(jnp.sum(kh * kh, -1, keepdims=True) + RMS_EPS)
            beta = beta_col[:, h:h + 1]
            egc = egc_col[:, h:h + 1]
            kb = k * beta
            qs.append(q)
            ks.append(k)
            kbs.append(kb)
            vbs.append(vh * beta)
            kes.append(kb * egc)
            qds.append(q * egc)
            kds.append(k * ekd_col[:, h:h + 1])
        bdk_hl = _block_rows_hl(_split_bf16(jnp.concatenate(ks, axis=-1)), G, mk_ref[...])
        dec = decay_all[:, g * GC:(g + 1) * GC]
        kk = _mm3p(_split_bf16(jnp.concatenate(kbs, axis=-1)), bdk_hl, _dot_nt)
        qk = _dot_nt(jnp.concatenate(qs, axis=-1).astype(BF16), bdk_hl[0])
        lmat = jnp.where(strict, kk * dec, 0.0)
        lmats.append(lmat)
        lm_hls.append(_split_bf16(lmat))
        intras.append((qk * dec).astype(BF16))
        rhss.append(_split_bf16(jnp.concatenate(vbs + kes, axis=-1)))

    tinvs = [eye - jnp.where(lev_mask(0), lmats[g], 0.0) for g in range(NG)]
    for lev in range(1, nlev):
        ml = ml_ref[lev]
        for g in range(NG):
            t_hl = _split_bf16(tinvs[g])
            b_hl = (lm_hls[g][0] * ml, lm_hls[g][1] * ml)
            p = _mm3p(b_hl, _block_rows_hl(t_hl, G, mt_ref[...]))
            tinvs[g] = tinvs[g] - _mm3p(t_hl, _block_rows_hl(_split_bf16(p), G, mt_ref[...]))

    for g in range(NG):
        x = _mm3p(_split_bf16(tinvs[g]), _block_rows_hl(rhss[g], G, mx_ref[...]))
        u_cat = x[:, :G * GDN_DV]
        w_cat = x[:, G * GDN_DV:]
        for t in range(G):
            h = g * G + t
            s = s_ref[h]
            s_hl = _split_bf16(s)
            lanes = slice(t * GDN_DV, (t + 1) * GDN_DV)
            v_new = u_cat[:, lanes] - _mm3p(_split_bf16(w_cat[:, lanes]), s_hl)
            v_hl = _split_bf16(v_new)
            o = _dot(qds[h].astype(BF16), s_hl[0]) + _dot(intras[g][:, t * C:(t + 1) * C], v_hl[0])
            s_ref[h] = s * egl[:, h:h + 1] + _mm3p(_split_bf16(kds[h]), v_hl, _dot_tn)
            zh = z[:, h * GDN_DV:(h + 1) * GDN_DV]
            on = o * lax.rsqrt(jnp.mean(o * o, -1, keepdims=True) + RMS_EPS) * nw * _silu(zh)
            o_ref[:, h * GDN_DV:(h + 1) * GDN_DV] = on.astype(o_ref.dtype)

    @pl.when(n == nchunks - 1)
    def _():
        sfin_ref[0] = s_ref[...]


def _gdn(proj, B, L, conv_state, gdn_state, conv_w, a_log, dt_bias, norm_w):
    C = min(CHUNK, L)
    N = L // C
    H = GDN_HEADS
    kern = functools.partial(_gdn_kernel, C=C, nchunks=N)
    return pl.pallas_call(
        kern,
        grid=(B, N),
        in_specs=[
            pl.BlockSpec((C, GDN_CONV_CH), lambda b, n: (b * N + n, C_QKV // GDN_CONV_CH)),
            pl.BlockSpec((C, GDN_VAL), lambda b, n: (b * N + n, C_Z // GDN_VAL)),
            pl.BlockSpec((C, LANE), lambda b, n: (b * N + n, C_SM // LANE)),
            pl.BlockSpec((1, CONV_W - 1, GDN_CONV_CH), lambda b, n: (b, 0, 0)),
            pl.BlockSpec((CONV_W, GDN_CONV_CH), lambda b, n: (0, 0)),
            pl.BlockSpec((1, H), lambda b, n: (0, 0)),
            pl.BlockSpec((1, H), lambda b, n: (0, 0)),
            pl.BlockSpec((1, GDN_DV), lambda b, n: (0, 0)),
            pl.BlockSpec((1, H, GDN_DK, GDN_DV), lambda b, n: (b, 0, 0, 0)),
        ],
        out_specs=[
            pl.BlockSpec((C, GDN_VAL), lambda b, n: (b * N + n, 0)),
            pl.BlockSpec((1, H, GDN_DK, GDN_DV), lambda b, n: (b, 0, 0, 0)),
        ],
        out_shape=[
            jax.ShapeDtypeStruct((B * L, GDN_VAL), BF16),
            jax.ShapeDtypeStruct((B, H, GDN_DK, GDN_DV), F32),
        ],
        scratch_shapes=[
            pltpu.VMEM((H, GDN_DK, GDN_DV), F32),
            pltpu.VMEM((8 + C, GDN_CONV_CH), F32),
            pltpu.VMEM((GDN_GROUP * C, GDN_GROUP * C), BF16),
            pltpu.VMEM((GDN_GROUP * C, GDN_GROUP * GDN_DK), BF16),
            pltpu.VMEM((GDN_GROUP * C, 2 * GDN_GROUP * GDN_DK), BF16),
            pltpu.VMEM((C.bit_length() - 1, C, GDN_GROUP * C), BF16),
        ],
        compiler_params=_cparams(("parallel", "arbitrary")),
        name="gdn",
    )(proj, proj, proj, conv_state, conv_w, a_log.reshape(1, H), dt_bias.reshape(1, H),
      norm_w.reshape(1, GDN_DV), gdn_state)


def _rope_kernel(q_ref, k_ref, v_ref, iq_ref, ik_ref, cos_ref, sa_ref, sb_ref,
                 qo_ref, kf_ref, kb_ref, vf_ref, vb_ref, iqo_ref, ikf_ref, ikb_ref):
    cosf = cos_ref[...]
    sa = sa_ref[...]
    sb = sb_ref[...]

    def rope(x):
        return (x * cosf + pltpu.roll(x, LANE - ROPE_HALF, 1) * sa + pltpu.roll(x, ROPE_HALF, 1) * sb)

    for h in range(ATT_HEADS):
        qo_ref[0, h] = rope(q_ref[:, h * ATT_HD:(h + 1) * ATT_HD]).astype(BF16)
    for h in range(ATT_KV_HEADS):
        kr = rope(k_ref[:, h * ATT_HD:(h + 1) * ATT_HD])
        kf_ref[:, h * ATT_HD:(h + 1) * ATT_HD] = kr
        kb_ref[:, h * ATT_HD:(h + 1) * ATT_HD] = kr.astype(BF16)
    v = v_ref[...]
    vf_ref[...] = v
    vb_ref[...] = v.astype(BF16)
    for h in range(IDX_HEADS):
        iqo_ref[0, h] = rope(iq_ref[:, h * IDX_HD:(h + 1) * IDX_HD]).astype(BF16)
    ikr = rope(ik_ref[...])
    ikf_ref[...] = ikr
    ikb_ref[...] = ikr.astype(BF16)


def _rope_tables(L, pos0, reps):
    inv = jnp.power(ROPE_THETA, -jnp.arange(ROPE_HALF, dtype=F32) * (2.0 / ROPE_ROT))
    ang = (pos0 + jnp.arange(L)).astype(F32)[:, None] * inv[None, :]
    cos, sin = jnp.cos(ang), jnp.sin(ang)
    z16 = jnp.zeros((L, ROPE_HALF), F32)
    rest0 = jnp.zeros((L, ATT_HD - ROPE_ROT), F32)
    cosf = jnp.concatenate([cos, cos, jnp.ones((L, ATT_HD - ROPE_ROT), F32)], -1)
    sa = jnp.concatenate([-sin, z16, rest0], -1)
    sb = jnp.concatenate([z16, sin, rest0], -1)
    if reps > 1:
        cosf, sa, sb = (jnp.tile(t, (reps, 1)) for t in (cosf, sa, sb))
    return cosf, sa, sb


def _rope(proj, B, L, pos0, qb):
    T = B * L
    reps = 1
    tab_rows = L
    cosf, sa, sb = _rope_tables(L, pos0, reps)
    nq = L // qb
    tspec = pl.BlockSpec((qb, LANE), lambda i: (i % nq, 0))
    outs = pl.pallas_call(
        _rope_kernel,
        grid=(T // qb,),
        in_specs=[
            pl.BlockSpec((qb, ATT_Q), lambda i: (i, C_Q // ATT_Q)),
            pl.BlockSpec((qb, ATT_KV), lambda i: (i, C_K // ATT_KV)),
            pl.BlockSpec((qb, ATT_KV), lambda i: (i, C_V // ATT_KV)),
            pl.BlockSpec((qb, IDX_HEADS * IDX_HD), lambda i: (i, C_IQ // (IDX_HEADS * IDX_HD))),
            pl.BlockSpec((qb, IDX_HD), lambda i: (i, C_IK // IDX_HD)),
            tspec, tspec, tspec,
        ],
        out_specs=[
            pl.BlockSpec((1, ATT_HEADS, qb, ATT_HD), lambda i: (i, 0, 0, 0)),
            pl.BlockSpec((qb, ATT_KV), lambda i: (i, 0)),
            pl.BlockSpec((qb, ATT_KV), lambda i: (i, 0)),
            pl.BlockSpec((qb, ATT_KV), lambda i: (i, 0)),
            pl.BlockSpec((qb, ATT_KV), lambda i: (i, 0)),
            pl.BlockSpec((1, IDX_HEADS, qb, IDX_HD), lambda i: (i, 0, 0, 0)),
            pl.BlockSpec((qb, IDX_HD), lambda i: (i, 0)),
            pl.BlockSpec((qb, IDX_HD), lambda i: (i, 0)),
        ],
        out_shape=[
            jax.ShapeDtypeStruct((T // qb, ATT_HEADS, qb, ATT_HD), BF16),
            jax.ShapeDtypeStruct((T, ATT_KV), F32),
            jax.ShapeDtypeStruct((T, ATT_KV), BF16),
            jax.ShapeDtypeStruct((T, ATT_KV), F32),
            jax.ShapeDtypeStruct((T, ATT_KV), BF16),
            jax.ShapeDtypeStruct((T // qb, IDX_HEADS, qb, IDX_HD), BF16),
            jax.ShapeDtypeStruct((T, IDX_HD), F32),
            jax.ShapeDtypeStruct((T, IDX_HD), BF16),
        ],
        compiler_params=_cparams(("parallel",)),
        name="rope",
    )(proj, proj, proj, proj, proj, cosf, sa, sb)
    return outs


def _dsa_kernel(iq_ref, sm_ref, q_ref, ik_ref, k_ref, v_ref, o_ref, sc_ref, *, qb, S, pos0, n_sel):
    j = pl.program_id(1)
    TK = TK_ATT
    q_last = pos0 + (j + 1) * qb - 1
    lim = jnp.minimum(((q_last // CHUNK) + 1) * CHUNK, S)
    nkt = (lim + TK - 1) // TK
    qpos = pos0 + j * qb + lax.broadcasted_iota(jnp.int32, (qb, 1), 0)
    qlim = jnp.minimum((jnp.right_shift(qpos, CHUNK.bit_length() - 1) + 1) * CHUNK, S)
    iw = sm_ref[:, SM_IW:SM_IW + IDX_HEADS] * ((IDX_HEADS ** -0.5) * (IDX_HD ** -0.5))
    HG = 4

    def score_tile(kt, carry):
        ks = pl.multiple_of(kt * TK, TK)
        ik_t = ik_ref[0, pl.ds(ks, TK), :]
        acc = jnp.zeros((qb, TK), F32)
        for hg in range(IDX_HEADS // HG):
            iq_g = iq_ref[0, hg * HG:(hg + 1) * HG].reshape(HG * qb, IDX_HD)
            r = jnp.maximum(_dot_nt(iq_g, ik_t), 0.0)
            for t in range(HG):
                hh = hg * HG + t
                acc = acc + iw[:, hh:hh + 1] * r[t * qb:(t + 1) * qb]
        kpos = ks + lax.broadcasted_iota(jnp.int32, (qb, TK), 1)
        sc_ref[kt] = jnp.where(kpos < qlim, acc, -jnp.inf)
        return carry

    lax.fori_loop(0, nkt, score_tile, 0)

    def lane_fold(m):
        p = m[:, 0:LANE]
        for t in range(1, TK // LANE):
            p = p + m[:, t * LANE:(t + 1) * LANE]
        return p

    def count_ge(x):
        def body(kt, c):
            return c + lane_fold(jnp.where(sc_ref[kt] >= x, 1.0, 0.0))
        part = lax.fori_loop(0, nkt, body, jnp.zeros((qb, LANE), F32))
        return jnp.sum(part, axis=1, keepdims=True)

    def minmax(kt, c):
        lo, hi = c
        t = sc_ref[kt]
        lo = jnp.minimum(lo, jnp.min(jnp.where(t > -jnp.inf, t, jnp.inf), axis=1, keepdims=True))
        hi = jnp.maximum(hi, jnp.max(t, axis=1, keepdims=True))
        return lo, hi

    lo0, hi0 = lax.fori_loop(0, nkt, minmax,
                             (jnp.full((qb, 1), jnp.inf, F32), jnp.full((qb, 1), -jnp.inf, F32)))
    kf = float(n_sel)
    cnt_all = count_ge(lo0)
    cnt_hi = count_ge(hi0)
    top_tied = cnt_hi >= kf
    fixed = (cnt_all <= kf) | top_tied
    lo_init = jnp.where(top_tied, hi0, lo0)

    def probe(lo, hi, done):
        mid = lo + (hi - lo) * 0.5
        stop = (mid <= lo) | (mid >= hi) | (done > 0.0)
        c = count_ge(mid)
        ge = c >= kf
        lo2 = jnp.where(stop, lo, jnp.where(ge, mid, lo))
        hi2 = jnp.where(stop, hi, jnp.where(ge, hi, mid))
        done2 = jnp.where(stop | (c == kf), 1.0, 0.0)
        return lo2, hi2, done2

    def cond(st):
        return jnp.logical_and(st[3] > 0, st[4] < 128)

    def body(st):
        lo, hi, done, _, it = st
        lo, hi, done = probe(lo, hi, done)
        lo, hi, done = probe(lo, hi, done)
        nact = jnp.sum(jnp.where(done > 0.0, 0, 1))
        return lo, hi, done, nact, it + 1

    done0 = jnp.where(fixed, 1.0, 0.0)
    thr, _, _, _, _ = lax.while_loop(cond, body, (lo_init, hi0, done0, jnp.int32(1), jnp.int32(0)))

    def to_bias(kt, carry):
        sc_ref[kt] = jnp.where(sc_ref[kt] >= thr, 0.0, NEG_BIG)
        return carry

    lax.fori_loop(0, nkt, to_bias, 0)

    R = ATT_REP
    for g in range(ATT_KV_HEADS):
        qg = q_ref[0, g * R:(g + 1) * R].reshape(R * qb, ATT_HD)

        def att_tile(kt, carry, g=g, qg=qg):
            m, l, acc = carry
            ks = pl.multiple_of(kt * TK, TK)
            k_t = k_ref[0, pl.ds(ks, TK), g * ATT_HD:(g + 1) * ATT_HD]
            v_t = v_ref[0, pl.ds(ks, TK), g * ATT_HD:(g + 1) * ATT_HD]
            s = _dot_nt(qg, k_t) * (ATT_HD ** -0.5)
            s = (s.reshape(R, qb, TK) + sc_ref[kt][None]).reshape(R * qb, TK)
            m_new = jnp.maximum(m, jnp.max(s, axis=1, keepdims=True))
            p = jnp.exp(s - m_new)
            a = jnp.exp(m - m_new)
            l = a * l + jnp.sum(p, axis=1, keepdims=True)
            acc = a * acc + _dot(p.astype(BF16), v_t)
            return m_new, l, acc

        m0 = jnp.full((R * qb, 1), NEG_BIG, F32)
        l0 = jnp.zeros((R * qb, 1), F32)
        a0 = jnp.zeros((R * qb, ATT_HD), F32)
        _, l, acc = lax.fori_loop(0, nkt, att_tile, (m0, l0, a0))
        out = acc / l
        for r in range(R):
            hh = g * R + r
            o_ref[:, hh * ATT_HD:(hh + 1) * ATT_HD] = out[r * qb:(r + 1) * qb].astype(o_ref.dtype)


def _dsa(iq_hm, proj, q_hm, ik_all, k_all, v_all, B, L, S, pos0):
    qb = min(Q_BLOCK, L)
    nq = L // qb
    S_pad = ik_all.shape[1]
    n_sel = min(TOPK_KEYS, S // 4)
    kern = functools.partial(_dsa_kernel, qb=qb, S=S, pos0=pos0, n_sel=n_sel)
    return pl.pallas_call(
        kern,
        grid=(B, nq),
        in_specs=[
            pl.BlockSpec((1, IDX_HEADS, qb, IDX_HD), lambda b, j: (b * nq + j, 0, 0, 0)),
            pl.BlockSpec((qb, LANE), lambda b, j: (b * nq + j, C_SM // LANE)),
            pl.BlockSpec((1, ATT_HEADS, qb, ATT_HD), lambda b, j: (b * nq + j, 0, 0, 0)),
            pl.BlockSpec((1, S_pad, IDX_HD), lambda b, j: (b, 0, 0)),
            pl.BlockSpec((1, S_pad, ATT_KV), lambda b, j: (b, 0, 0)),
            pl.BlockSpec((1, S_pad, ATT_KV), lambda b, j: (b, 0, 0)),
        ],
        out_specs=pl.BlockSpec((qb, ATT_Q), lambda b, j: (b * nq + j, 0)),
        out_shape=jax.ShapeDtypeStruct((B * L, ATT_Q), BF16),
        scratch_shapes=[pltpu.VMEM((S_pad // TK_ATT, qb, TK_ATT), F32)],
        compiler_params=_cparams(("parallel", "arbitrary")),
        name="dsa",
    )(iq_hm, proj, q_hm, ik_all, k_all, v_all)


def _merge_kernel(oa_ref, ob_ref, wa_ref, wb_ref, ga_ref, gb_ref, o_ref):
    ya = _dot(oa_ref[...], wa_ref[...])
    yb = _dot(ob_ref[...], wb_ref[...])
    o_ref[...] = (_sigmoid(ga_ref[...]) * ya + _sigmoid(gb_ref[...]) * yb).astype(o_ref.dtype)


def _merge(oa, ob, wa, wb, proj):
    T = oa.shape[0]
    tm = min(512, T)
    tn = TN_MIX
    return pl.pallas_call(
        _merge_kernel,
        grid=(T // tm, D_MODEL // tn),
        in_specs=[
            pl.BlockSpec((tm, GDN_VAL), lambda i, j: (i, 0)),
            pl.BlockSpec((tm, ATT_Q), lambda i, j: (i, 0)),
            pl.BlockSpec((GDN_VAL, tn), lambda i, j: (0, j)),
            pl.BlockSpec((ATT_Q, tn), lambda i, j: (0, j)),
            pl.BlockSpec((tm, tn), lambda i, j: (i, C_GA // tn + j)),
            pl.BlockSpec((tm, tn), lambda i, j: (i, C_GB // tn + j)),
        ],
        out_specs=pl.BlockSpec((tm, tn), lambda i, j: (i, j)),
        out_shape=jax.ShapeDtypeStruct((T, D_MODEL), BF16),
        compiler_params=_cparams(("parallel", "arbitrary")),
        name="merge",
    )(oa, ob, wa, wb, proj, proj)


def _oproj_kernel(m_ref, w_ref, x_ref, g1_ref, sc2_ref, sh2_ref, lg_ref, lb_ref, x1_ref, h2_ref, acc_ref, *, nj):
    j = pl.program_id(1)
    acc_ref[j] = _dot(m_ref[...], w_ref[...])

    @pl.when(j == nj - 1)
    def _():
        bb, tl, D = x_ref.shape
        tn = acc_ref.shape[2]
        sl = lambda t: slice(t * tn, (t + 1) * tn)
        s1 = jnp.zeros((bb, tl, 1), F32)
        for t in range(nj):
            v = ALPHA * x_ref[:, :, sl(t)] + g1_ref[:, :, sl(t)] * acc_ref[t].reshape(bb, tl, tn)
            acc_ref[t] = v.reshape(bb * tl, tn)
            s1 = s1 + jnp.sum(v, -1, keepdims=True)
        mu = s1 * (1.0 / D)
        s2 = jnp.zeros((bb, tl, 1), F32)
        for t in range(nj):
            d = acc_ref[t].reshape(bb, tl, tn) - mu
            s2 = s2 + jnp.sum(d * d, -1, keepdims=True)
        rstd = lax.rsqrt(s2 * (1.0 / D) + LN_EPS)
        for t in range(nj):
            x1 = (acc_ref[t].reshape(bb, tl, tn) - mu) * rstd * lg_ref[:, :, sl(t)] + lb_ref[:, :, sl(t)]
            x1_ref[:, :, sl(t)] = x1
            h2_ref[:, sl(t)] = (x1 * (1.0 + sc2_ref[:, :, sl(t)]) + sh2_ref[:, :, sl(t)]).reshape(bb * tl, tn)


def _oproj_kernel_aliased(m_ref, w_ref, x_ref, g1_ref, sc2_ref, sh2_ref, lg_ref, lb_ref, h2_all_ref,
                          x1_ref, h2_ref, acc_ref, *, nj):
    del h2_all_ref
    _oproj_kernel(m_ref, w_ref, x_ref, g1_ref, sc2_ref, sh2_ref, lg_ref, lb_ref, x1_ref, h2_ref, acc_ref, nj=nj)


def _oproj(m, w_o, x3, mod3, row0, ln_g, ln_b, h2_all, t_all, t_off):
    B, L, D = x3.shape
    bb, tl = _row_tiling(B, L, TM_MIX)
    nl = L // tl
    rows = bb * tl
    mrow = row0 // bb
    tn = TN_MIX
    nj = D // tn
    assert t_off % rows == 0
    boff = t_off // rows
    modspec = lambda c: pl.BlockSpec((bb, 1, D), lambda i, j: (mrow + i // nl, 0, c))
    in_specs = [
        pl.BlockSpec((rows, D), lambda i, j: (i, 0)),
        pl.BlockSpec((D, tn), lambda i, j: (0, j)),
        pl.BlockSpec((bb, tl, D), lambda i, j: (i // nl, i % nl, 0)),
        modspec(2), modspec(4), modspec(3),
        pl.BlockSpec((1, 1, D), lambda i, j: (0, 0, 0)),
        pl.BlockSpec((1, 1, D), lambda i, j: (0, 0, 0)),
    ]
    args = [m, w_o, x3, mod3, mod3, mod3, ln_g.reshape(1, 1, D), ln_b.reshape(1, 1, D)]
    aliases = {}
    kern = functools.partial(_oproj_kernel, nj=nj)
    if h2_all is not None:
        in_specs.append(pl.BlockSpec(memory_space=pl.ANY))
        args.append(h2_all)
        aliases = {len(args) - 1: 1}
        kern = functools.partial(_oproj_kernel_aliased, nj=nj)
    return pl.pallas_call(
        kern,
        grid=((B // bb) * nl, nj),
        in_specs=in_specs,
        out_specs=[
            pl.BlockSpec((bb, tl, D), lambda i, j: (i // nl, i % nl, 0)),
            pl.BlockSpec((rows, D), lambda i, j: (boff + i, 0)),
        ],
        out_shape=[
            jax.ShapeDtypeStruct((B, L, D), F32),
            jax.ShapeDtypeStruct((t_all, D), F32),
        ],
        scratch_shapes=[pltpu.VMEM((nj, rows, tn), F32)],
        input_output_aliases=aliases,
        compiler_params=_cparams(("parallel", "arbitrary")),
        name="oproj_ln1",
    )(*args)


def _router_kernel(h_ref, w_ref, b_ref, id_ref, wt_ref):
    logits = jnp.dot(h_ref[...], w_ref[...], precision=lax.Precision.HIGHEST,
                     preferred_element_type=F32) + b_ref[...]
    rows = logits.shape[0]
    lane = lax.broadcasted_iota(jnp.int32, (rows, LANE), 1)
    is_g = lane < N_GROUPS
    gl = jnp.where(is_g, logits, -jnp.inf)
    gmax = jnp.max(gl, axis=1, keepdims=True)
    g_sel = jnp.min(jnp.where(gl == gmax, lane, LANE), axis=1, keepdims=True)
    p_grp = 1.0 / jnp.sum(jnp.where(is_g, jnp.exp(gl - gmax), 0.0), axis=1, keepdims=True)
    e_lo = N_GROUPS + g_sel * EXP_PER_GROUP
    in_g = (lane >= e_lo) & (lane < e_lo + EXP_PER_GROUP)
    el = jnp.where(in_g, logits, -jnp.inf)
    m1 = jnp.max(el, axis=1, keepdims=True)
    i1 = jnp.min(jnp.where(el == m1, lane, LANE), axis=1, keepdims=True)
    el2 = jnp.where(lane == i1, -jnp.inf, el)
    m2 = jnp.max(el2, axis=1, keepdims=True)
    i2 = jnp.min(jnp.where(el2 == m2, lane, LANE), axis=1, keepdims=True)
    e21 = jnp.exp(m2 - m1)
    w1 = p_grp / (1.0 + e21)
    w2 = p_grp * e21 / (1.0 + e21)
    id_ref[...] = jnp.where(lane == 0, i1 - N_GROUPS, jnp.where(lane == 1, i2 - N_GROUPS, 0))
    wt_ref[...] = jnp.where(lane == 0, w1, jnp.where(lane == 1, w2, 0.0))


def _router(h2, w_r, b_r):
    T, D = h2.shape
    tm = 512
    while T % tm:
        tm //= 2
    return pl.pallas_call(
        _router_kernel,
        grid=(T // tm,),
        in_specs=[
            pl.BlockSpec((tm, D), lambda i: (i, 0)),
            pl.BlockSpec((D, LANE), lambda i: (0, 0)),
            pl.BlockSpec((1, LANE), lambda i: (0, 0)),
        ],
        out_specs=[pl.BlockSpec((tm, LANE), lambda i: (i, 0)), pl.BlockSpec((tm, LANE), lambda i: (i, 0))],
        out_shape=[jax.ShapeDtypeStruct((T, LANE), jnp.int32), jax.ShapeDtypeStruct((T, LANE), F32)],
        compiler_params=_cparams(("parallel",)),
        name="router",
    )(h2, w_r, b_r)


def _expert_kernel(blk_e_ref, tok_ref, nused_ref, h_hbm, wg_ref, wu_ref, wd_ref, o_ref,
                   xbuf_ref, xb16_ref, act_ref, sem_ref, *, BM, nfc):
    i = pl.program_id(0)
    c = pl.program_id(1)
    nused = nused_ref[0]
    slot = lax.rem(i, 2)

    def start_gather(blk, sl):
        def body(r, carry):
            tok = tok_ref[blk * BM + r]
            pltpu.make_async_copy(h_hbm.at[pl.ds(tok, 1)], xbuf_ref.at[sl, pl.ds(r, 1)], sem_ref.at[sl]).start()
            return carry
        lax.fori_loop(0, BM, body, 0, unroll=8)

    def wait_gather(sl):
        pltpu.make_async_copy(xbuf_ref.at[sl], xbuf_ref.at[sl], sem_ref.at[sl]).wait()

    @pl.when(jnp.logical_and(c == 0, i < nused))
    def _():
        @pl.when(i == 0)
        def _():
            start_gather(0, 0)

        @pl.when(i + 1 < nused)
        def _():
            start_gather(i + 1, 1 - slot)

        wait_gather(slot)
        xb16_ref[...] = xbuf_ref[slot].astype(BF16)

    @pl.when(i < nused)
    def _():
        x = xb16_ref[...]
        gate = _dot(x, wg_ref[0])
        up = _dot(x, wu_ref[0])
        act_ref[c] = (_silu(gate) * up).astype(BF16)

        @pl.when(c == nfc - 1)
        def _():
            act = jnp.concatenate([act_ref[t] for t in range(nfc)], axis=-1)
            o_ref[...] = _dot(act, wd_ref[0])

    @pl.when(jnp.logical_and(i >= nused, c == nfc - 1))
    def _():
        o_ref[...] = jnp.zeros_like(o_ref)


def _experts(h2, blk_e, slot_tok, nused, w_up16, w_down16, BM):
    T, D = h2.shape
    P = slot_tok.shape[0]
    nblk = P // BM
    fc = FC_MOE
    nfc = D_EXPERT // fc
    kern = functools.partial(_expert_kernel, BM=BM, nfc=nfc)

    def eidx(i, nu):
        return jnp.minimum(i, nu[0] - 1)

    def cidx(i, c, nu):
        return jnp.where(i < nu[0], c, nfc - 1)

    grid_spec = pltpu.PrefetchScalarGridSpec(
        num_scalar_prefetch=3,
        grid=(nblk, nfc),
        in_specs=[
            pl.BlockSpec(memory_space=pl.ANY),
            pl.BlockSpec((1, D, fc), lambda i, c, be, tk, nu: (be[eidx(i, nu)], 0, cidx(i, c, nu))),
            pl.BlockSpec((1, D, fc), lambda i, c, be, tk, nu: (be[eidx(i, nu)], 0, nfc + cidx(i, c, nu))),
            pl.BlockSpec((1, D_EXPERT, D), lambda i, c, be, tk, nu: (be[eidx(i, nu)], 0, 0)),
        ],
        out_specs=pl.BlockSpec((BM, D), lambda i, c, be, tk, nu: (i, 0)),
        scratch_shapes=[
            pltpu.VMEM((2, BM, D), F32),
            pltpu.VMEM((BM, D), BF16),
            pltpu.VMEM((nfc, BM, fc), BF16),
            pltpu.SemaphoreType.DMA((2,)),
        ],
    )
    return pl.pallas_call(
        kern,
        grid_spec=grid_spec,
        out_shape=jax.ShapeDtypeStruct((P, D), F32),
        compiler_params=_cparams(("arbitrary", "arbitrary")),
        name="experts",
    )(blk_e, slot_tok, nused, h2, w_up16, w_up16, w_down16)


def _route_metadata(ids, T, BM):
    expert = ids[:, 0:2].reshape(-1)
    A = 2 * T
    onehot = (expert[:, None] == jnp.arange(N_EXPERTS, dtype=jnp.int32)[None, :]).astype(jnp.int32)
    csum = jnp.cumsum(onehot, axis=0)
    counts = csum[-1]
    rank = jnp.sum((csum - onehot) * onehot, axis=1)
    padded = ((counts + BM - 1) // BM) * BM
    pend = jnp.cumsum(padded)
    pstart = pend - padded
    dest = (pstart[expert] + rank).astype(jnp.int32)
    P = ((A + BM - 1) // BM) * BM + N_EXPERTS * BM
    nblk = P // BM
    tok = jnp.arange(A, dtype=jnp.int32) // 2
    slot_tok = jnp.zeros((P,), jnp.int32).at[dest].set(tok)
    blk_e = jnp.clip(jnp.searchsorted(pend, jnp.arange(nblk, dtype=jnp.int32) * BM, side="right"),
                     0, N_EXPERTS - 1).astype(jnp.int32)
    nused = (pend[-1] // BM).astype(jnp.int32).reshape(1)
    return dest, slot_tok, blk_e, nused


def _final_kernel(dest_ref, yb_hbm, x1_ref, g2_ref, wt_ref, lg_ref, lb_ref, o_ref, buf_ref, sem_ref, *, rows, ntiles):
    i = pl.program_id(0)
    slot = lax.rem(i, 2)

    def start_gather(tile, sl):
        def body(r, carry):
            d = dest_ref[tile * 2 * rows + r]
            pltpu.make_async_copy(yb_hbm.at[pl.ds(d, 1)], buf_ref.at[sl, pl.ds(r, 1)], sem_ref.at[sl]).start()
            return carry
        lax.fori_loop(0, 2 * rows, body, 0, unroll=8)

    def wait_gather(sl):
        pltpu.make_async_copy(buf_ref.at[sl], buf_ref.at[sl], sem_ref.at[sl]).wait()

    @pl.when(i == 0)
    def _():
        start_gather(0, 0)

    @pl.when(i + 1 < ntiles)
    def _():
        start_gather(i + 1, 1 - slot)

    wait_gather(slot)
    bb, tl, D = x1_ref.shape
    tn = TN_MIX
    sl = lambda t: slice(t * tn, (t + 1) * tn)
    w0 = wt_ref[:, 0:1]
    w1 = wt_ref[:, 1:2]
    s1 = jnp.zeros((bb, tl, 1), F32)
    for t in range(D // tn):
        f = w0 * buf_ref[slot, 0:rows, sl(t)] + w1 * buf_ref[slot, rows:2 * rows, sl(t)]
        v = ALPHA * x1_ref[:, :, sl(t)] + g2_ref[:, :, sl(t)] * f.reshape(bb, tl, tn)
        o_ref[:, :, sl(t)] = v
        s1 = s1 + jnp.sum(v, -1, keepdims=True)
    mu = s1 * (1.0 / D)
    s2 = jnp.zeros((bb, tl, 1), F32)
    for t in range(D // tn):
        d = o_ref[:, :, sl(t)] - mu
        s2 = s2 + jnp.sum(d * d, -1, keepdims=True)
    rstd = lax.rsqrt(s2 * (1.0 / D) + LN_EPS)
    for t in range(D // tn):
        o_ref[:, :, sl(t)] = (o_ref[:, :, sl(t)] - mu) * rstd * lg_ref[:, :, sl(t)] + lb_ref[:, :, sl(t)]


def _final(dest, yb, x1, mod3, row0, wts, t_off, ln_g, ln_b):
    B, L, D = x1.shape
    T = B * L
    bb, tl = _row_tiling(B, L, TM_MIX)
    nl = L // tl
    rows = bb * tl
    mrow = row0 // bb
    ntiles = (B // bb) * nl
    woff = t_off // rows
    dest_tiles = dest[2 * t_off:2 * (t_off + T)].reshape(T // rows, rows, 2).transpose(0, 2, 1).reshape(-1)
    kern = functools.partial(_final_kernel, rows=rows, ntiles=ntiles)
    grid_spec = pltpu.PrefetchScalarGridSpec(
        num_scalar_prefetch=1,
        grid=(ntiles,),
        in_specs=[
            pl.BlockSpec(memory_space=pl.ANY),
            pl.BlockSpec((bb, tl, D), lambda i, d: (i // nl, i % nl, 0)),
            pl.BlockSpec((bb, 1, D), lambda i, d: (mrow + i // nl, 0, 5)),
            pl.BlockSpec((rows, LANE), lambda i, d: (woff + i, 0)),
            pl.BlockSpec((1, 1, D), lambda i, d: (0, 0, 0)),
            pl.BlockSpec((1, 1, D), lambda i, d: (0, 0, 0)),
        ],
        out_specs=pl.BlockSpec((bb, tl, D), lambda i, d: (i // nl, i % nl, 0)),
        scratch_shapes=[
            pltpu.VMEM((2, 2 * rows, D), F32),
            pltpu.SemaphoreType.DMA((2,)),
        ],
    )
    return pl.pallas_call(
        kern,
        grid_spec=grid_spec,
        out_shape=jax.ShapeDtypeStruct((B, L, D), F32),
        compiler_params=_cparams(("arbitrary",)),
        name="combine_ln2",
    )(dest_tiles, yb, x1, mod3, wts, ln_g.reshape(1, 1, D), ln_b.reshape(1, 1, D))


def _pad_keys(t, S_pad):
    S = t.shape[1]
    if S == S_pad:
        return t
    return jnp.pad(t, ((0, 0), (0, S_pad - S), (0, 0)))


def _mixers(x3, mod3, row0, conv_state, gdn_state, past_k, past_v, past_ik, wp, h2_all, t_all, t_off):
    B, L, D = x3.shape
    pos0 = past_k.shape[1]
    S = pos0 + L
    proj = _inproj(x3, mod3, row0, wp["w_in"])

    oa, new_gdn = _gdn(proj, B, L, conv_state, gdn_state, wp["conv_w"], wp["a_log"], wp["dt_bias"],
                       wp["gdn_norm_w"])
    new_conv = proj.reshape(B, L, NP)[:, L - (CONV_W - 1):, C_QKV:C_QKV + GDN_CONV_CH]

    qb = min(Q_BLOCK, L)
    q_hm, kf, k16, vf, v16, iq_hm, ikf, ik16 = _rope(proj, B, L, pos0, qb)
    S_pad = ((S + TK_ATT - 1) // TK_ATT) * TK_ATT
    k_all = _pad_keys(jnp.concatenate([past_k.reshape(B, pos0, ATT_KV).astype(BF16), k16.reshape(B, L, ATT_KV)], 1), S_pad)
    v_all = _pad_keys(jnp.concatenate([past_v.reshape(B, pos0, ATT_KV).astype(BF16), v16.reshape(B, L, ATT_KV)], 1), S_pad)
    ik_all = _pad_keys(jnp.concatenate([past_ik.astype(BF16), ik16.reshape(B, L, IDX_HD)], 1), S_pad)
    ob = _dsa(iq_hm, proj, q_hm, ik_all, k_all, v_all, B, L, S, pos0)

    m = _merge(oa, ob, wp["w_br_a"], wp["w_br_b"], proj)
    x1, h2_all = _oproj(m, wp["w_o"], x3, mod3, row0, wp["ln1_g"], wp["ln1_b"], h2_all, t_all, t_off)

    kb_out = kf.reshape(B, L, ATT_KV_HEADS, ATT_HD)
    vb_out = vf.reshape(B, L, ATT_KV_HEADS, ATT_HD)
    ik_out = ikf.reshape(B, L, IDX_HD)
    return x1, h2_all, (new_conv, new_gdn, kb_out, vb_out, ik_out)


def _permute_w_in(w_in):
    sizes = (GDN_CONV_CH, GDN_HEADS, GDN_HEADS, GDN_VAL, ATT_Q, ATT_KV, ATT_KV, IDX_HEADS * IDX_HD, IDX_HD,
             IDX_HEADS, 2 * D_MODEL)
    parts, off = [], 0
    for s in sizes:
        parts.append(w_in[:, off:off + s])
        off += s
    qkv, a, b, z, q, k, v, iq, ik, iw, gates = parts
    pad = jnp.zeros((w_in.shape[0], SM_W - 2 * GDN_HEADS - IDX_HEADS), w_in.dtype)
    return jnp.concatenate([qkv, z, iq, q, k, v, ik, a, b, iw, pad, gates], axis=1).astype(BF16)


def kernel(x_prompt, x_sample, c_prompt, c_sample, state_conv, state_gdn, cache_k, cache_v, cache_idx_k,
           w_ada, b_ada, w_in, conv_w, a_log, dt_bias, gdn_norm_w, w_br_a, w_br_b, w_o,
           ln1_g, ln1_b, ln2_g, ln2_b, w_grp, b_grp, w_rtr, b_rtr, w_up, w_down):
    Bp, Lp, D = x_prompt.shape
    Bs, Ls, _ = x_sample.shape
    l = 0
    row0_p = Bs
    nrows = ((Bs + Bp + 7) // 8) * 8
    c_all = jnp.concatenate([c_sample, c_prompt, jnp.zeros((nrows - Bs - Bp, D), F32)], 0)
    mod = _ada(c_all, w_ada[l], b_ada[l])
    mod3 = mod.reshape(nrows, 1, 6 * D)

    nr = LANE - N_GROUPS - N_EXPERTS
    wp = dict(
        w_in=_permute_w_in(w_in[l]),
        conv_w=conv_w[l], a_log=a_log[l], dt_bias=dt_bias[l], gdn_norm_w=gdn_norm_w[l],
        w_br_a=w_br_a[l].astype(BF16), w_br_b=w_br_b[l].astype(BF16), w_o=w_o[l].astype(BF16),
        ln1_g=ln1_g[l], ln1_b=ln1_b[l], ln2_g=ln2_g[l], ln2_b=ln2_b[l],
        w_r=jnp.concatenate([w_grp[l], w_rtr[l], jnp.zeros((D, nr), F32)], 1),
        b_r=jnp.concatenate([b_grp[l], b_rtr[l], jnp.zeros((nr,), F32)]).reshape(1, LANE),
        w_up=w_up[l].astype(BF16), w_down=w_down[l].astype(BF16),
    )

    zc = jnp.zeros((Bp, CONV_W - 1, GDN_CONV_CH), F32)
    zs = jnp.zeros((Bp, GDN_HEADS, GDN_DK, GDN_DV), F32)
    zk = jnp.zeros((Bp, 0, ATT_KV_HEADS, ATT_HD), F32)
    zik = jnp.zeros((Bp, 0, IDX_HD), F32)
    Tp, Ts = Bp * Lp, Bs * Ls
    t_all = Tp + Ts
    x1p, h2_all, (c1, g1, k1, v1, i1) = _mixers(x_prompt, mod3, row0_p, zc, zs, zk, zk, zik, wp, None, t_all, 0)
    x1s, h2_all, (c2, g2, k2, v2, i2) = _mixers(x_sample, mod3, 0, state_conv[l], state_gdn[l], cache_k[l],
                                                 cache_v[l], cache_idx_k[l], wp, h2_all, t_all, Tp)

    ids, wts = _router(h2_all, wp["w_r"], wp["b_r"])
    dest, slot_tok, blk_e, nused = _route_metadata(ids, t_all, MOE_BM)
    yb = _experts(h2_all, blk_e, slot_tok, nused, wp["w_up"], wp["w_down"], MOE_BM)
    yp = _final(dest, yb, x1p, mod3, row0_p, wts, 0, wp["ln2_g"], wp["ln2_b"])
    ys = _final(dest, yb, x1s, mod3, 0, wts, Tp, wp["ln2_g"], wp["ln2_b"])
    st = lambda t: t[None]
    return (yp, ys, st(c1), st(g1), st(k1), st(v1), st(i1), st(c2), st(g2), st(k2), st(v2), st(i2))
```

```python
import functools

import jax
import jax.numpy as jnp
from jax import lax
from jax.experimental import pallas as pl
from jax.experimental.pallas import tpu as pltpu

F32 = jnp.float32
BF16 = jnp.bfloat16

D_MODEL = 4096
CHUNK = 64
GDN_HEADS = D_MODEL // 256
GDN_DK = 128
GDN_DV = 128
GDN_KEY = GDN_HEADS * GDN_DK
GDN_VAL = GDN_HEADS * GDN_DV
GDN_CONV_CH = 2 * GDN_KEY + GDN_VAL
CONV_W = 4
ATT_HEADS = D_MODEL // 256
ATT_KV_HEADS = ATT_HEADS // 4
ATT_REP = ATT_HEADS // ATT_KV_HEADS
ATT_HD = 128
ATT_Q = ATT_HEADS * ATT_HD
ATT_KV = ATT_KV_HEADS * ATT_HD
IDX_HEADS = D_MODEL // 128
IDX_HD = 128
TOPK_KEYS = 256
Q_BLOCK = 128
ROPE_THETA = 500000.0
ROPE_ROT = ATT_HD // 4
ROPE_HALF = ROPE_ROT // 2
N_GROUPS = 4
EXP_PER_GROUP = 8
N_EXPERTS = N_GROUPS * EXP_PER_GROUP
D_EXPERT = D_MODEL // 4
DEPTH = 1
ALPHA = (2.0 * DEPTH) ** 0.25
LN_EPS = 1e-5
RMS_EPS = 1e-6

C_QKV = 0
C_Z = C_QKV + GDN_CONV_CH
C_IQ = C_Z + GDN_VAL
C_Q = C_IQ + IDX_HEADS * IDX_HD
C_K = C_Q + ATT_Q
C_V = C_K + ATT_KV
C_IK = C_V + ATT_KV
C_SM = C_IK + IDX_HD
SM_W = 384
C_GA = C_SM + SM_W
C_GB = C_GA + D_MODEL
NP = C_GB + D_MODEL
SM_A, SM_B, SM_IW = 0, GDN_HEADS, 2 * GDN_HEADS

LANE = 128
NEG_BIG = -1e30
VMEM_LIMIT = 56 * 1024 * 1024

TN_IN = 512
TM_IN = 1024
TM_MIX = 256
TN_MIX = 512
TK_ATT = 512
FC_MOE = 256
MOE_BM = 512


def _cparams(sem):
    return pltpu.CompilerParams(dimension_semantics=sem, vmem_limit_bytes=VMEM_LIMIT)


def _dot(a, b):
    return jnp.dot(a, b, preferred_element_type=F32)


def _dot_nt(a, b):
    return lax.dot_general(a, b, (((1,), (1,)), ((), ())), preferred_element_type=F32)


def _dot_tn(a, b):
    return lax.dot_general(a, b, (((0,), (0,)), ((), ())), preferred_element_type=F32)


def _split_bf16(a):
    hi = a.astype(BF16)
    lo = (a - hi.astype(F32)).astype(BF16)
    return hi, lo


def _mm3(a, b, dot=_dot):
    ah, al = _split_bf16(a)
    bh, bl = _split_bf16(b)
    return dot(ah, bh) + (dot(ah, bl) + dot(al, bh))


def _mm1(a, b):
    return _dot(a.astype(BF16), b.astype(BF16))


def _mm1_nt(a, b):
    return _dot_nt(a.astype(BF16), b.astype(BF16))


def _sigmoid(x):
    return 1.0 / (1.0 + jnp.exp(-x))


def _silu(x):
    return x * _sigmoid(x)


def _softplus(x):
    return jnp.maximum(x, 0.0) + jnp.log(1.0 + jnp.exp(-jnp.abs(x)))


def _ada_kernel(c_ref, w_ref, b_ref, o_ref):
    s = _silu(c_ref[...])
    o_ref[...] = _dot(s.astype(BF16), w_ref[...].astype(BF16)) + b_ref[...]


def _ada(c_all, w_ada, b_ada):
    R, D = c_all.shape
    N = w_ada.shape[1]
    tn = 512
    return pl.pallas_call(
        _ada_kernel,
        grid=(N // tn,),
        in_specs=[
            pl.BlockSpec((R, D), lambda j: (0, 0)),
            pl.BlockSpec((D, tn), lambda j: (0, j)),
            pl.BlockSpec((1, tn), lambda j: (0, j)),
        ],
        out_specs=pl.BlockSpec((R, tn), lambda j: (0, j)),
        out_shape=jax.ShapeDtypeStruct((R, N), F32),
        compiler_params=_cparams(("parallel",)),
        name="ada",
    )(c_all, w_ada, b_ada.reshape(1, N))


def _inproj_kernel(x_ref, sc_ref, sh_ref, w_ref, o_ref, h_ref):
    @pl.when(pl.program_id(1) == 0)
    def _():
        h = x_ref[...] * (1.0 + sc_ref[...]) + sh_ref[...]
        h_ref[...] = h.reshape(h_ref.shape).astype(BF16)

    o_ref[...] = _dot(h_ref[...], w_ref[...])


def _row_tiling(B, L, tm):
    if L >= tm:
        return 1, tm
    bb = max(1, min(B, tm // L))
    while B % bb:
        bb -= 1
    return bb, L


def _inproj(x3, mod3, row0, w_perm):
    B, L, D = x3.shape
    bb, tl = _row_tiling(B, L, TM_IN)
    nl = L // tl
    rows = bb * tl
    mrow = row0 // bb
    return pl.pallas_call(
        _inproj_kernel,
        grid=((B // bb) * nl, NP // TN_IN),
        in_specs=[
            pl.BlockSpec((bb, tl, D), lambda i, j: (i // nl, i % nl, 0), pipeline_mode=pl.Buffered(1)),
            pl.BlockSpec((bb, 1, D), lambda i, j: (mrow + i // nl, 0, 1)),
            pl.BlockSpec((bb, 1, D), lambda i, j: (mrow + i // nl, 0, 0)),
            pl.BlockSpec((D, TN_IN), lambda i, j: (0, j)),
        ],
        out_specs=pl.BlockSpec((rows, TN_IN), lambda i, j: (i, j)),
        out_shape=jax.ShapeDtypeStruct((B * L, NP), F32),
        scratch_shapes=[pltpu.VMEM((rows, D), BF16)],
        compiler_params=_cparams(("parallel", "arbitrary")),
        name="inproj",
    )(x3, mod3, mod3, w_perm)


GDN_GROUP = 4


def _split3_bf16(a):
    hi = a.astype(BF16)
    r = a - hi.astype(F32)
    mid = r.astype(BF16)
    lo = (r - mid.astype(F32)).astype(BF16)
    return hi, mid, lo


def _dot_exact01(a, b01, a_is_01=False):
    if a_is_01:
        h, m, l = _split3_bf16(b01)
        return _dot(a, h) + (_dot(a, m) + _dot(a, l))
    h, m, l = _split3_bf16(a)
    return _dot(h, b01) + (_dot(m, b01) + _dot(l, b01))


def _mm3p(a_hl, b_hl, dot=_dot):
    (ah, al), (bh, bl) = a_hl, b_hl
    return dot(ah, bh) + (dot(ah, bl) + dot(al, bh))


def _block_rows_hl(x_hl, nblk, mask01):
    return tuple(jnp.concatenate([p] * nblk, axis=0) * mask01 for p in x_hl)


def _gdn_kernel(qkv_ref, z_ref, sm_ref, cst_ref, cw_ref, alr_ref, dtr_ref, nw_ref, s0_ref,
                o_ref, sfin_ref, s_ref, ext_ref, mt_ref, mk_ref, mx_ref, ml_ref, *, C, nchunks):
    n = pl.program_id(1)
    H = GDN_HEADS
    G = GDN_GROUP
    NG = H // G
    GC = G * C
    PADR = 8
    logc = C.bit_length() - 1

    @pl.when(n == 0)
    def _():
        s_ref[...] = s0_ref[0]
        ext_ref[PADR - (CONV_W - 1):PADR, :] = cst_ref[0]

    ext_ref[PADR:PADR + C, :] = qkv_ref[...]
    cw = cw_ref[...]
    y = ext_ref[PADR - 3:PADR - 3 + C, :] * cw[0:1]
    for jw in range(1, CONV_W):
        y = y + ext_ref[PADR - 3 + jw:PADR - 3 + jw + C, :] * cw[jw:jw + 1]
    y = _silu(y)
    hist = ext_ref[PADR + C - (CONV_W - 1):PADR + C, :]
    ext_ref[PADR - (CONV_W - 1):PADR, :] = hist

    sm = sm_ref[...]
    g_col = -jnp.exp(alr_ref[...]) * _softplus(sm[:, SM_A:SM_A + H] + dtr_ref[...])
    beta_col = _sigmoid(sm[:, SM_B:SM_B + H])

    ii = lax.broadcasted_iota(jnp.int32, (C, C), 0)
    jj = lax.broadcasted_iota(jnp.int32, (C, C), 1)
    tril01 = jnp.where(jj <= ii, 1.0, 0.0).astype(BF16)
    gc_col = _dot_exact01(tril01, g_col, a_is_01=True)
    egc_col = jnp.exp(gc_col)
    gc_last = gc_col[C - 1:C, :]
    ekd_col = jnp.exp(gc_last - gc_col)
    egl = jnp.exp(gc_last)

    W = H * C
    hrow = lax.broadcasted_iota(jnp.int32, (H, W), 0)
    hlane = jnp.right_shift(lax.broadcasted_iota(jnp.int32, (H, W), 1), logc)
    e_seg = jnp.where(hrow == hlane, 1.0, 0.0).astype(BF16)
    gseg = _dot_exact01(gc_col, e_seg)
    ri = lax.broadcasted_iota(jnp.int32, (C, W), 0)
    cj = jnp.bitwise_and(lax.broadcasted_iota(jnp.int32, (C, W), 1), C - 1)
    grow = jnp.sum(jnp.where(ri == cj, gseg, 0.0), axis=0, keepdims=True)
    decay_all = jnp.where(cj <= ri, jnp.exp(gseg - grow), 0.0)

    gi = lax.broadcasted_iota(jnp.int32, (C, GC), 0)
    gj = jnp.bitwise_and(lax.broadcasted_iota(jnp.int32, (C, GC), 1), C - 1)
    strict = gj < gi
    eye = jnp.where(gi == gj, 1.0, 0.0)
    nlev = logc

    def lev_mask(lev):
        return ((jnp.right_shift(gi, lev + 1) == jnp.right_shift(gj, lev + 1))
                & (jnp.bitwise_and(jnp.right_shift(gi, lev), 1) == 1)
                & (jnp.bitwise_and(jnp.right_shift(gj, lev), 1) == 0))

    @pl.when(n == 0)
    def _():
        def own(ncols, col_head):
            r = jnp.right_shift(lax.broadcasted_iota(jnp.int32, (G * C, ncols), 0), logc)
            c = col_head(lax.broadcasted_iota(jnp.int32, (G * C, ncols), 1))
            return jnp.where(r == c, 1.0, 0.0).astype(BF16)

        mt_ref[...] = own(GC, lambda c: jnp.right_shift(c, logc))
        mk_ref[...] = own(G * GDN_DK, lambda c: jnp.right_shift(c, 7))
        mx_ref[...] = own(2 * G * GDN_DK, lambda c: jnp.bitwise_and(jnp.right_shift(c, 7), G - 1))
        for lev in range(1, nlev):
            ml_ref[lev] = jnp.where(lev_mask(lev), 1.0, 0.0).astype(BF16)

    nw = nw_ref[...]
    z = z_ref[...]

    lmats, lm_hls, intras, rhss, qds, kds = [], [], [], [], [], []
    for g in range(NG):
        qs, ks, kbs, vbs, kes = [], [], [], [], []
        for t in range(G):
            h = g * G + t
            qh = y[:, h * GDN_DK:(h + 1) * GDN_DK]
            kh = y[:, GDN_KEY + h * GDN_DK:GDN_KEY + (h + 1) * GDN_DK]
            vh = y[:, 2 * GDN_KEY + h * GDN_DV:2 * GDN_KEY + (h + 1) * GDN_DV]
            q = qh * lax.rsqrt(jnp.sum(qh * qh, -1, keepdims=True) + RMS_EPS) * (GDN_DK ** -0.5)
            k = kh * lax.rsqrt(jnp.sum(kh * kh, -1, keepdims=True) + RMS_EPS)
            beta = beta_col[:, h:h + 1]
            egc = egc_col[:, h:h + 1]
            kb = k * beta
            qs.append(q)
            ks.append(k)
            kbs.append(kb)
            vbs.append(vh * beta)
            kes.append(kb * egc)
            qds.append(q * egc)
            kds.append(k * ekd_col[:, h:h + 1])
        bdk_hl = _block_rows_hl(_split_bf16(jnp.concatenate(ks, axis=-1)), G, mk_ref[...])
        dec = decay_all[:, g * GC:(g + 1) * GC]
        kk = _mm3p(_split_bf16(jnp.concatenate(kbs, axis=-1)), bdk_hl, _dot_nt)
        qk = _dot_nt(jnp.concatenate(qs, axis=-1).astype(BF16), bdk_hl[0])
        lmat = jnp.where(strict, kk * dec, 0.0)
        lmats.append(lmat)
        lm_hls.append(_split_bf16(lmat))
        intras.append((qk * dec).astype(BF16))
        rhss.append(_split_bf16(jnp.concatenate(vbs + kes, axis=-1)))

    tinvs = [eye - jnp.where(lev_mask(0), lmats[g], 0.0) for g in range(NG)]
    for lev in range(1, nlev):
        ml = ml_ref[lev]
        for g in range(NG):
            t_hl = _split_bf16(tinvs[g])
            b_hl = (lm_hls[g][0] * ml, lm_hls[g][1] * ml)
            p = _mm3p(b_hl, _block_rows_hl(t_hl, G, mt_ref[...]))
            tinvs[g] = tinvs[g] - _mm3p(t_hl, _block_rows_hl(_split_bf16(p), G, mt_ref[...]))

    for g in range(NG):
        x = _mm3p(_split_bf16(tinvs[g]), _block_rows_hl(rhss[g], G, mx_ref[...]))
        u_cat = x[:, :G * GDN_DV]
        w_cat = x[:, G * GDN_DV:]
        for t in range(G):
            h = g * G + t
            s = s_ref[h]
            s_hl = _split_bf16(s)
            lanes = slice(t * GDN_DV, (t + 1) * GDN_DV)
            v_new = u_cat[:, lanes] - _mm3p(_split_bf16(w_cat[:, lanes]), s_hl)
            v_hl = _split_bf16(v_new)
            o = _dot(qds[h].astype(BF16), s_hl[0]) + _dot(intras[g][:, t * C:(t + 1) * C], v_hl[0])
            s_ref[h] = s * egl[:, h:h + 1] + _mm3p(_split_bf16(kds[h]), v_hl, _dot_tn)
            zh = z[:, h * GDN_DV:(h + 1) * GDN_DV]
            on = o * lax.rsqrt(jnp.mean(o * o, -1, keepdims=True) + RMS_EPS) * nw * _silu(zh)
            o_ref[:, h * GDN_DV:(h + 1) * GDN_DV] = on.astype(o_ref.dtype)

    @pl.when(n == nchunks - 1)
    def _():
        sfin_ref[0] = s_ref[...]


def _gdn(proj, B, L, conv_state, gdn_state, conv_w, a_log, dt_bias, norm_w):
    C = min(CHUNK, L)
    N = L // C
    H = GDN_HEADS
    kern = functools.partial(_gdn_kernel, C=C, nchunks=N)
    return pl.pallas_call(
        kern,
        grid=(B, N),
        in_specs=[
            pl.BlockSpec((C, GDN_CONV_CH), lambda b, n: (b * N + n, C_QKV // GDN_CONV_CH)),
            pl.BlockSpec((C, GDN_VAL), lambda b, n: (b * N + n, C_Z // GDN_VAL)),
            pl.BlockSpec((C, LANE), lambda b, n: (b * N + n, C_SM // LANE)),
            pl.BlockSpec((1, CONV_W - 1, GDN_CONV_CH), lambda b, n: (b, 0, 0)),
            pl.BlockSpec((CONV_W, GDN_CONV_CH), lambda b, n: (0, 0)),
            pl.BlockSpec((1, H), lambda b, n: (0, 0)),
            pl.BlockSpec((1, H), lambda b, n: (0, 0)),
            pl.BlockSpec((1, GDN_DV), lambda b, n: (0, 0)),
            pl.BlockSpec((1, H, GDN_DK, GDN_DV), lambda b, n: (b, 0, 0, 0)),
        ],
        out_specs=[
            pl.BlockSpec((C, GDN_VAL), lambda b, n: (b * N + n, 0)),
            pl.BlockSpec((1, H, GDN_DK, GDN_DV), lambda b, n: (b, 0, 0, 0)),
        ],
        out_shape=[
            jax.ShapeDtypeStruct((B * L, GDN_VAL), BF16),
            jax.ShapeDtypeStruct((B, H, GDN_DK, GDN_DV), F32),
        ],
        scratch_shapes=[
            pltpu.VMEM((H, GDN_DK, GDN_DV), F32),
            pltpu.VMEM((8 + C, GDN_CONV_CH), F32),
            pltpu.VMEM((GDN_GROUP * C, GDN_GROUP * C), BF16),
            pltpu.VMEM((GDN_GROUP * C, GDN_GROUP * GDN_DK), BF16),
            pltpu.VMEM((GDN_GROUP * C, 2 * GDN_GROUP * GDN_DK), BF16),
            pltpu.VMEM((C.bit_length() - 1, C, GDN_GROUP * C), BF16),
        ],
        compiler_params=_cparams(("parallel", "arbitrary")),
        name="gdn",
    )(proj, proj, proj, conv_state, conv_w, a_log.reshape(1, H), dt_bias.reshape(1, H),
      norm_w.reshape(1, GDN_DV), gdn_state)


def _rope_kernel(q_ref, k_ref, v_ref, iq_ref, ik_ref, cos_ref, sa_ref, sb_ref,
                 qo_ref, kf_ref, kb_ref, vf_ref, vb_ref, iqo_ref, ikf_ref, ikb_ref):
    cosf = cos_ref[...]
    sa = sa_ref[...]
    sb = sb_ref[...]

    def rope(x):
        return (x * cosf + pltpu.roll(x, LANE - ROPE_HALF, 1) * sa + pltpu.roll(x, ROPE_HALF, 1) * sb)

    for h in range(ATT_HEADS):
        qo_ref[0, h] = rope(q_ref[:, h * ATT_HD:(h + 1) * ATT_HD]).astype(BF16)
    for h in range(ATT_KV_HEADS):
        kr = rope(k_ref[:, h * ATT_HD:(h + 1) * ATT_HD])
        kf_ref[:, h * ATT_HD:(h + 1) * ATT_HD] = kr
        kb_ref[:, h * ATT_HD:(h + 1) * ATT_HD] = kr.astype(BF16)
    v = v_ref[...]
    vf_ref[...] = v
    vb_ref[...] = v.astype(BF16)
    for h in range(IDX_HEADS):
        iqo_ref[0, h] = rope(iq_ref[:, h * IDX_HD:(h + 1) * IDX_HD]).astype(BF16)
    ikr = rope(ik_ref[...])
    ikf_ref[...] = ikr
    ikb_ref[...] = ikr.astype(BF16)


def _rope_tables(L, pos0, reps):
    inv = jnp.power(ROPE_THETA, -jnp.arange(ROPE_HALF, dtype=F32) * (2.0 / ROPE_ROT))
    ang = (pos0 + jnp.arange(L)).astype(F32)[:, None] * inv[None, :]
    cos, sin = jnp.cos(ang), jnp.sin(ang)
    z16 = jnp.zeros((L, ROPE_HALF), F32)
    rest0 = jnp.zeros((L, ATT_HD - ROPE_ROT), F32)
    cosf = jnp.concatenate([cos, cos, jnp.ones((L, ATT_HD - ROPE_ROT), F32)], -1)
    sa = jnp.concatenate([-sin, z16, rest0], -1)
    sb = jnp.concatenate([z16, sin, rest0], -1)
    if reps > 1:
        cosf, sa, sb = (jnp.tile(t, (reps, 1)) for t in (cosf, sa, sb))
    return cosf, sa, sb


def _rope(proj, B, L, pos0, qb):
    T = B * L
    reps = 1
    tab_rows = L
    cosf, sa, sb = _rope_tables(L, pos0, reps)
    nq = L // qb
    tspec = pl.BlockSpec((qb, LANE), lambda i: (i % nq, 0))
    outs = pl.pallas_call(
        _rope_kernel,
        grid=(T // qb,),
        in_specs=[
            pl.BlockSpec((qb, ATT_Q), lambda i: (i, C_Q // ATT_Q)),
            pl.BlockSpec((qb, ATT_KV), lambda i: (i, C_K // ATT_KV)),
            pl.BlockSpec((qb, ATT_KV), lambda i: (i, C_V // ATT_KV)),
            pl.BlockSpec((qb, IDX_HEADS * IDX_HD), lambda i: (i, C_IQ // (IDX_HEADS * IDX_HD))),
            pl.BlockSpec((qb, IDX_HD), lambda i: (i, C_IK // IDX_HD)),
            tspec, tspec, tspec,
        ],
        out_specs=[
            pl.BlockSpec((1, ATT_HEADS, qb, ATT_HD), lambda i: (i, 0, 0, 0)),
            pl.BlockSpec((qb, ATT_KV), lambda i: (i, 0)),
            pl.BlockSpec((qb, ATT_KV), lambda i: (i, 0)),
            pl.BlockSpec((qb, ATT_KV), lambda i: (i, 0)),
            pl.BlockSpec((qb, ATT_KV), lambda i: (i, 0)),
            pl.BlockSpec((1, IDX_HEADS, qb, IDX_HD), lambda i: (i, 0, 0, 0)),
            pl.BlockSpec((qb, IDX_HD), lambda i: (i, 0)),
            pl.BlockSpec((qb, IDX_HD), lambda i: (i, 0)),
        ],
        out_shape=[
            jax.ShapeDtypeStruct((T // qb, ATT_HEADS, qb, ATT_HD), BF16),
            jax.ShapeDtypeStruct((T, ATT_KV), F32),
            jax.ShapeDtypeStruct((T, ATT_KV), BF16),
            jax.ShapeDtypeStruct((T, ATT_KV), F32),
            jax.ShapeDtypeStruct((T, ATT_KV), BF16),
            jax.ShapeDtypeStruct((T // qb, IDX_HEADS, qb, IDX_HD), BF16),
            jax.ShapeDtypeStruct((T, IDX_HD), F32),
            jax.ShapeDtypeStruct((T, IDX_HD), BF16),
        ],
        compiler_params=_cparams(("parallel",)),
        name="rope",
    )(proj, proj, proj, proj, proj, cosf, sa, sb)
    return outs


def _dsa_kernel(iq_ref, sm_ref, q_ref, ik_ref, k_ref, v_ref, o_ref, sc_ref, *, qb, S, pos0, n_sel):
    j = pl.program_id(1)
    TK = TK_ATT
    q_last = pos0 + (j + 1) * qb - 1
    lim = jnp.minimum(((q_last // CHUNK) + 1) * CHUNK, S)
    nkt = (lim + TK - 1) // TK
    qpos = pos0 + j * qb + lax.broadcasted_iota(jnp.int32, (qb, 1), 0)
    qlim = jnp.minimum((jnp.right_shift(qpos, CHUNK.bit_length() - 1) + 1) * CHUNK, S)
    iw = sm_ref[:, SM_IW:SM_IW + IDX_HEADS] * ((IDX_HEADS ** -0.5) * (IDX_HD ** -0.5))
    HG = 4

    def score_tile(kt, carry):
        ks = pl.multiple_of(kt * TK, TK)
        ik_t = ik_ref[0, pl.ds(ks, TK), :]
        acc = jnp.zeros((qb, TK), F32)
        for hg in range(IDX_HEADS // HG):
            iq_g = iq_ref[0, hg * HG:(hg + 1) * HG].reshape(HG * qb, IDX_HD)
            r = jnp.maximum(_dot_nt(iq_g, ik_t), 0.0)
            for t in range(HG):
                hh = hg * HG + t
                acc = acc + iw[:, hh:hh + 1] * r[t * qb:(t + 1) * qb]
        kpos = ks + lax.broadcasted_iota(jnp.int32, (qb, TK), 1)
        sc_ref[kt] = jnp.where(kpos < qlim, acc, -jnp.inf)
        return carry

    lax.fori_loop(0, nkt, score_tile, 0)

    def lane_fold(m):
        p = m[:, 0:LANE]
        for t in range(1, TK // LANE):
            p = p + m[:, t * LANE:(t + 1) * LANE]
        return p

    def count_ge(x):
        def body(kt, c):
            return c + lane_fold(jnp.where(sc_ref[kt] >= x, 1.0, 0.0))
        part = lax.fori_loop(0, nkt, body, jnp.zeros((qb, LANE), F32))
        return jnp.sum(part, axis=1, keepdims=True)

    def minmax(kt, c):
        lo, hi = c
        t = sc_ref[kt]
        lo = jnp.minimum(lo, jnp.min(jnp.where(t > -jnp.inf, t, jnp.inf), axis=1, keepdims=True))
        hi = jnp.maximum(hi, jnp.max(t, axis=1, keepdims=True))
        return lo, hi

    lo0, hi0 = lax.fori_loop(0, nkt, minmax,
                             (jnp.full((qb, 1), jnp.inf, F32), jnp.full((qb, 1), -jnp.inf, F32)))
    kf = float(n_sel)
    cnt_all = count_ge(lo0)
    cnt_hi = count_ge(hi0)
    top_tied = cnt_hi >= kf
    fixed = (cnt_all <= kf) | top_tied
    lo_init = jnp.where(top_tied, hi0, lo0)

    def probe(lo, hi, done):
        mid = lo + (hi - lo) * 0.5
        stop = (mid <= lo) | (mid >= hi) | (done > 0.0)
        c = count_ge(mid)
        ge = c >= kf
        lo2 = jnp.where(stop, lo, jnp.where(ge, mid, lo))
        hi2 = jnp.where(stop, hi, jnp.where(ge, hi, mid))
        done2 = jnp.where(stop | (c == kf), 1.0, 0.0)
        return lo2, hi2, done2

    def cond(st):
        return jnp.logical_and(st[3] > 0, st[4] < 128)

    def body(st):
        lo, hi, done, _, it = st
        lo, hi, done = probe(lo, hi, done)
        lo, hi, done = probe(lo, hi, done)
        nact = jnp.sum(jnp.where(done > 0.0, 0, 1))
        return lo, hi, done, nact, it + 1

    done0 = jnp.where(fixed, 1.0, 0.0)
    thr, _, _, _, _ = lax.while_loop(cond, body, (lo_init, hi0, done0, jnp.int32(1), jnp.int32(0)))

    def to_bias(kt, carry):
        sc_ref[kt] = jnp.where(sc_ref[kt] >= thr, 0.0, NEG_BIG)
        return carry

    lax.fori_loop(0, nkt, to_bias, 0)

    R = ATT_REP
    for g in range(ATT_KV_HEADS):
        qg = q_ref[0, g * R:(g + 1) * R].reshape(R * qb, ATT_HD)

        def att_tile(kt, carry, g=g, qg=qg):
            m, l, acc = carry
            ks = pl.multiple_of(kt * TK, TK)
            k_t = k_ref[0, pl.ds(ks, TK), g * ATT_HD:(g + 1) * ATT_HD]
            v_t = v_ref[0, pl.ds(ks, TK), g * ATT_HD:(g + 1) * ATT_HD]
            s = _dot_nt(qg, k_t) * (ATT_HD ** -0.5)
            s = (s.reshape(R, qb, TK) + sc_ref[kt][None]).reshape(R * qb, TK)
            m_new = jnp.maximum(m, jnp.max(s, axis=1, keepdims=True))
            p = jnp.exp(s - m_new)
            a = jnp.exp(m - m_new)
            l = a * l + jnp.sum(p, axis=1, keepdims=True)
            acc = a * acc + _dot(p.astype(BF16), v_t)
            return m_new, l, acc

        def att_pair(i, carry, att_tile=att_tile):
            return att_tile(2 * i + 1, att_tile(2 * i, carry))

        m0 = jnp.full((R * qb, 1), NEG_BIG, F32)
        l0 = jnp.zeros((R * qb, 1), F32)
        a0 = jnp.zeros((R * qb, ATT_HD), F32)
        carry = lax.fori_loop(0, nkt // 2, att_pair, (m0, l0, a0))
        _, l, acc = lax.cond(nkt % 2 == 1, lambda c, att_tile=att_tile: att_tile(nkt - 1, c), lambda c: c, carry)
        out = acc / l
        for r in range(R):
            hh = g * R + r
            o_ref[:, hh * ATT_HD:(hh + 1) * ATT_HD] = out[r * qb:(r + 1) * qb].astype(o_ref.dtype)


def _dsa(iq_hm, proj, q_hm, ik_all, k_all, v_all, B, L, S, pos0):
    qb = min(Q_BLOCK, L)
    nq = L // qb
    S_pad = ik_all.shape[1]
    n_sel = min(TOPK_KEYS, S // 4)
    kern = functools.partial(_dsa_kernel, qb=qb, S=S, pos0=pos0, n_sel=n_sel)
    return pl.pallas_call(
        kern,
        grid=(B, nq),
        in_specs=[
            pl.BlockSpec((1, IDX_HEADS, qb, IDX_HD), lambda b, j: (b * nq + j, 0, 0, 0)),
            pl.BlockSpec((qb, LANE), lambda b, j: (b * nq + j, C_SM // LANE)),
            pl.BlockSpec((1, ATT_HEADS, qb, ATT_HD), lambda b, j: (b * nq + j, 0, 0, 0)),
            pl.BlockSpec((1, S_pad, IDX_HD), lambda b, j: (b, 0, 0)),
            pl.BlockSpec((1, S_pad, ATT_KV), lambda b, j: (b, 0, 0)),
            pl.BlockSpec((1, S_pad, ATT_KV), lambda b, j: (b, 0, 0)),
        ],
        out_specs=pl.BlockSpec((qb, ATT_Q), lambda b, j: (b * nq + j, 0)),
        out_shape=jax.ShapeDtypeStruct((B * L, ATT_Q), BF16),
        scratch_shapes=[pltpu.VMEM((S_pad // TK_ATT, qb, TK_ATT), F32)],
        compiler_params=_cparams(("parallel", "arbitrary")),
        name="dsa",
    )(iq_hm, proj, q_hm, ik_all, k_all, v_all)


def _merge_kernel(oa_ref, ob_ref, wa_ref, wb_ref, ga_ref, gb_ref, o_ref):
    ya = _dot(oa_ref[...], wa_ref[...])
    yb = _dot(ob_ref[...], wb_ref[...])
    o_ref[...] = (_sigmoid(ga_ref[...]) * ya + _sigmoid(gb_ref[...]) * yb).astype(o_ref.dtype)


def _merge(oa, ob, wa, wb, proj):
    T = oa.shape[0]
    tm = min(512, T)
    tn = TN_MIX
    return pl.pallas_call(
        _merge_kernel,
        grid=(T // tm, D_MODEL // tn),
        in_specs=[
            pl.BlockSpec((tm, GDN_VAL), lambda i, j: (i, 0)),
            pl.BlockSpec((tm, ATT_Q), lambda i, j: (i, 0)),
            pl.BlockSpec((GDN_VAL, tn), lambda i, j: (0, j)),
            pl.BlockSpec((ATT_Q, tn), lambda i, j: (0, j)),
            pl.BlockSpec((tm, tn), lambda i, j: (i, C_GA // tn + j)),
            pl.BlockSpec((tm, tn), lambda i, j: (i, C_GB // tn + j)),
        ],
        out_specs=pl.BlockSpec((tm, tn), lambda i, j: (i, j)),
        out_shape=jax.ShapeDtypeStruct((T, D_MODEL), BF16),
        compiler_params=_cparams(("parallel", "arbitrary")),
        name="merge",
    )(oa, ob, wa, wb, proj, proj)


def _oproj_kernel(m_ref, w_ref, x_ref, g1_ref, sc2_ref, sh2_ref, lg_ref, lb_ref, x1_ref, h2_ref, acc_ref, *, nj):
    j = pl.program_id(1)
    acc_ref[j] = _dot(m_ref[...], w_ref[...])

    @pl.when(j == nj - 1)
    def _():
        bb, tl, D = x_ref.shape
        tn = acc_ref.shape[2]
        sl = lambda t: slice(t * tn, (t + 1) * tn)
        s1 = jnp.zeros((bb, tl, 1), F32)
        for t in range(nj):
            v = ALPHA * x_ref[:, :, sl(t)] + g1_ref[:, :, sl(t)] * acc_ref[t].reshape(bb, tl, tn)
            acc_ref[t] = v.reshape(bb * tl, tn)
            s1 = s1 + jnp.sum(v, -1, keepdims=True)
        mu = s1 * (1.0 / D)
        s2 = jnp.zeros((bb, tl, 1), F32)
        for t in range(nj):
            d = acc_ref[t].reshape(bb, tl, tn) - mu
            s2 = s2 + jnp.sum(d * d, -1, keepdims=True)
        rstd = lax.rsqrt(s2 * (1.0 / D) + LN_EPS)
        for t in range(nj):
            x1 = (acc_ref[t].reshape(bb, tl, tn) - mu) * rstd * lg_ref[:, :, sl(t)] + lb_ref[:, :, sl(t)]
            x1_ref[:, :, sl(t)] = x1
            h2_ref[:, sl(t)] = (x1 * (1.0 + sc2_ref[:, :, sl(t)]) + sh2_ref[:, :, sl(t)]).reshape(bb * tl, tn)


def _oproj_kernel_aliased(m_ref, w_ref, x_ref, g1_ref, sc2_ref, sh2_ref, lg_ref, lb_ref, h2_all_ref,
                          x1_ref, h2_ref, acc_ref, *, nj):
    del h2_all_ref
    _oproj_kernel(m_ref, w_ref, x_ref, g1_ref, sc2_ref, sh2_ref, lg_ref, lb_ref, x1_ref, h2_ref, acc_ref, nj=nj)


def _oproj(m, w_o, x3, mod3, row0, ln_g, ln_b, h2_all, t_all, t_off):
    B, L, D = x3.shape
    bb, tl = _row_tiling(B, L, TM_MIX)
    nl = L // tl
    rows = bb * tl
    mrow = row0 // bb
    tn = TN_MIX
    nj = D // tn
    assert t_off % rows == 0
    boff = t_off // rows
    modspec = lambda c: pl.BlockSpec((bb, 1, D), lambda i, j: (mrow + i // nl, 0, c))
    in_specs = [
        pl.BlockSpec((rows, D), lambda i, j: (i, 0)),
        pl.BlockSpec((D, tn), lambda i, j: (0, j)),
        pl.BlockSpec((bb, tl, D), lambda i, j: (i // nl, i % nl, 0)),
        modspec(2), modspec(4), modspec(3),
        pl.BlockSpec((1, 1, D), lambda i, j: (0, 0, 0)),
        pl.BlockSpec((1, 1, D), lambda i, j: (0, 0, 0)),
    ]
    args = [m, w_o, x3, mod3, mod3, mod3, ln_g.reshape(1, 1, D), ln_b.reshape(1, 1, D), h2_all]
    in_specs.append(pl.BlockSpec(memory_space=pl.ANY))
    aliases = {len(args) - 1: 1}
    kern = functools.partial(_oproj_kernel_aliased, nj=nj)
    return pl.pallas_call(
        kern,
        grid=((B // bb) * nl, nj),
        in_specs=in_specs,
        out_specs=[
            pl.BlockSpec((bb, tl, D), lambda i, j: (i // nl, i % nl, 0)),
            pl.BlockSpec((rows, D), lambda i, j: (boff + i, 0)),
        ],
        out_shape=[
            jax.ShapeDtypeStruct((B, L, D), F32),
            jax.ShapeDtypeStruct((t_all, D), F32),
        ],
        scratch_shapes=[pltpu.VMEM((nj, rows, tn), F32)],
        input_output_aliases=aliases,
        compiler_params=_cparams(("parallel", "arbitrary")),
        name="oproj_ln1",
    )(*args)


def _router_kernel(h_ref, w_ref, b_ref, id_ref, wt_ref):
    logits = jnp.dot(h_ref[...], w_ref[...], precision=lax.Precision.HIGHEST,
                     preferred_element_type=F32) + b_ref[...]
    rows = logits.shape[0]
    lane = lax.broadcasted_iota(jnp.int32, (rows, LANE), 1)
    is_g = lane < N_GROUPS
    gl = jnp.where(is_g, logits, -jnp.inf)
    gmax = jnp.max(gl, axis=1, keepdims=True)
    g_sel = jnp.min(jnp.where(gl == gmax, lane, LANE), axis=1, keepdims=True)
    p_grp = 1.0 / jnp.sum(jnp.where(is_g, jnp.exp(gl - gmax), 0.0), axis=1, keepdims=True)
    e_lo = N_GROUPS + g_sel * EXP_PER_GROUP
    in_g = (lane >= e_lo) & (lane < e_lo + EXP_PER_GROUP)
    el = jnp.where(in_g, logits, -jnp.inf)
    m1 = jnp.max(el, axis=1, keepdims=True)
    i1 = jnp.min(jnp.where(el == m1, lane, LANE), axis=1, keepdims=True)
    el2 = jnp.where(lane == i1, -jnp.inf, el)
    m2 = jnp.max(el2, axis=1, keepdims=True)
    i2 = jnp.min(jnp.where(el2 == m2, lane, LANE), axis=1, keepdims=True)
    e21 = jnp.exp(m2 - m1)
    w1 = p_grp / (1.0 + e21)
    w2 = p_grp * e21 / (1.0 + e21)
    id_ref[...] = jnp.where(lane == 0, i1 - N_GROUPS, jnp.where(lane == 1, i2 - N_GROUPS, 0))
    wt_ref[...] = jnp.where(lane == 0, w1, jnp.where(lane == 1, w2, 0.0))


def _router(h2, w_r, b_r):
    T, D = h2.shape
    tm = 512
    while T % tm:
        tm //= 2
    return pl.pallas_call(
        _router_kernel,
        grid=(T // tm,),
        in_specs=[
            pl.BlockSpec((tm, D), lambda i: (i, 0)),
            pl.BlockSpec((D, LANE), lambda i: (0, 0)),
            pl.BlockSpec((1, LANE), lambda i: (0, 0)),
        ],
        out_specs=[pl.BlockSpec((tm, LANE), lambda i: (i, 0)), pl.BlockSpec((tm, LANE), lambda i: (i, 0))],
        out_shape=[jax.ShapeDtypeStruct((T, LANE), jnp.int32), jax.ShapeDtypeStruct((T, LANE), F32)],
        compiler_params=_cparams(("parallel",)),
        name="router",
    )(h2, w_r, b_r)


def _expert_kernel(blk_e_ref, tok_ref, nused_ref, h_hbm, wg_ref, wu_ref, wd_ref, o_ref,
                   xbuf_ref, xb16_ref, act_ref, sem_ref, *, BM, nfc):
    i = pl.program_id(0)
    c = pl.program_id(1)
    nused = nused_ref[0]
    slot = lax.rem(i, 2)

    def start_gather(blk, sl):
        def body(r, carry):
            tok = tok_ref[blk * BM + r]
            pltpu.make_async_copy(h_hbm.at[pl.ds(tok, 1)], xbuf_ref.at[sl, pl.ds(r, 1)], sem_ref.at[sl]).start()
            return carry
        lax.fori_loop(0, BM, body, 0, unroll=8)

    def wait_gather(sl):
        pltpu.make_async_copy(xbuf_ref.at[sl], xbuf_ref.at[sl], sem_ref.at[sl]).wait()

    @pl.when(jnp.logical_and(c == 0, i < nused))
    def _():
        @pl.when(i == 0)
        def _():
            start_gather(0, 0)

        @pl.when(i + 1 < nused)
        def _():
            start_gather(i + 1, 1 - slot)

        wait_gather(slot)
        xb16_ref[...] = xbuf_ref[slot].astype(BF16)

    @pl.when(i < nused)
    def _():
        x = xb16_ref[...]
        gate = _dot(x, wg_ref[0])
        up = _dot(x, wu_ref[0])
        act_ref[c] = (_silu(gate) * up).astype(BF16)

        @pl.when(c == nfc - 1)
        def _():
            act = jnp.concatenate([act_ref[t] for t in range(nfc)], axis=-1)
            o_ref[...] = _dot(act, wd_ref[0])

    @pl.when(jnp.logical_and(i >= nused, c == nfc - 1))
    def _():
        o_ref[...] = jnp.zeros_like(o_ref)


def _experts(h2, blk_e, slot_tok, nused, w_up16, w_down16, BM):
    T, D = h2.shape
    P = slot_tok.shape[0]
    nblk = P // BM
    fc = FC_MOE
    nfc = D_EXPERT // fc
    kern = functools.partial(_expert_kernel, BM=BM, nfc=nfc)

    def eidx(i, nu):
        return jnp.minimum(i, nu[0] - 1)

    def cidx(i, c, nu):
        return jnp.where(i < nu[0], c, nfc - 1)

    grid_spec = pltpu.PrefetchScalarGridSpec(
        num_scalar_prefetch=3,
        grid=(nblk, nfc),
        in_specs=[
            pl.BlockSpec(memory_space=pl.ANY),
            pl.BlockSpec((1, D, fc), lambda i, c, be, tk, nu: (be[eidx(i, nu)], 0, cidx(i, c, nu))),
            pl.BlockSpec((1, D, fc), lambda i, c, be, tk, nu: (be[eidx(i, nu)], 0, nfc + cidx(i, c, nu))),
            pl.BlockSpec((1, D_EXPERT, D), lambda i, c, be, tk, nu: (be[eidx(i, nu)], 0, 0),
                         pipeline_mode=pl.Buffered(1)),
        ],
        out_specs=pl.BlockSpec((BM, D), lambda i, c, be, tk, nu: (i, 0), pipeline_mode=pl.Buffered(1)),
        scratch_shapes=[
            pltpu.VMEM((2, BM, D), F32),
            pltpu.VMEM((BM, D), BF16),
            pltpu.VMEM((nfc, BM, fc), BF16),
            pltpu.SemaphoreType.DMA((2,)),
        ],
    )
    return pl.pallas_call(
        kern,
        grid_spec=grid_spec,
        out_shape=jax.ShapeDtypeStruct((P, D), F32),
        compiler_params=_cparams(("arbitrary", "arbitrary")),
        name="experts",
    )(blk_e, slot_tok, nused, h2, w_up16, w_up16, w_down16)


def _route_metadata(ids, T, BM):
    expert = ids[:, 0:2].reshape(-1)
    A = 2 * T
    onehot = (expert[:, None] == jnp.arange(N_EXPERTS, dtype=jnp.int32)[None, :]).astype(jnp.int32)
    csum = jnp.cumsum(onehot, axis=0)
    counts = csum[-1]
    rank = jnp.sum((csum - onehot) * onehot, axis=1)
    padded = ((counts + BM - 1) // BM) * BM
    pend = jnp.cumsum(padded)
    pstart = pend - padded
    dest = (pstart[expert] + rank).astype(jnp.int32)
    P = ((A + BM - 1) // BM) * BM + N_EXPERTS * BM
    nblk = P // BM
    tok = jnp.arange(A, dtype=jnp.int32) // 2
    slot_tok = jnp.zeros((P,), jnp.int32).at[dest].set(tok)
    blk_e = jnp.clip(jnp.searchsorted(pend, jnp.arange(nblk, dtype=jnp.int32) * BM, side="right"),
                     0, N_EXPERTS - 1).astype(jnp.int32)
    nused = (pend[-1] // BM).astype(jnp.int32).reshape(1)
    return dest, slot_tok, blk_e, nused


def _final_kernel(dest_ref, yb_hbm, x1_ref, g2_ref, wt_ref, lg_ref, lb_ref, o_ref, buf_ref, sem_ref, *, rows, ntiles):
    i = pl.program_id(0)
    slot = lax.rem(i, 2)

    def start_gather(tile, sl):
        def body(r, carry):
            d = dest_ref[tile * 2 * rows + r]
            pltpu.make_async_copy(yb_hbm.at[pl.ds(d, 1)], buf_ref.at[sl, pl.ds(r, 1)], sem_ref.at[sl]).start()
            return carry
        lax.fori_loop(0, 2 * rows, body, 0, unroll=8)

    def wait_gather(sl):
        pltpu.make_async_copy(buf_ref.at[sl], buf_ref.at[sl], sem_ref.at[sl]).wait()

    @pl.when(i == 0)
    def _():
        start_gather(0, 0)

    @pl.when(i + 1 < ntiles)
    def _():
        start_gather(i + 1, 1 - slot)

    wait_gather(slot)
    bb, tl, D = x1_ref.shape
    tn = TN_MIX
    sl = lambda t: slice(t * tn, (t + 1) * tn)
    w0 = wt_ref[:, 0:1]
    w1 = wt_ref[:, 1:2]
    s1 = jnp.zeros((bb, tl, 1), F32)
    for t in range(D // tn):
        f = w0 * buf_ref[slot, 0:rows, sl(t)] + w1 * buf_ref[slot, rows:2 * rows, sl(t)]
        v = ALPHA * x1_ref[:, :, sl(t)] + g2_ref[:, :, sl(t)] * f.reshape(bb, tl, tn)
        o_ref[:, :, sl(t)] = v
        s1 = s1 + jnp.sum(v, -1, keepdims=True)
    mu = s1 * (1.0 / D)
    s2 = jnp.zeros((bb, tl, 1), F32)
    for t in range(D // tn):
        d = o_ref[:, :, sl(t)] - mu
        s2 = s2 + jnp.sum(d * d, -1, keepdims=True)
    rstd = lax.rsqrt(s2 * (1.0 / D) + LN_EPS)
    for t in range(D // tn):
        o_ref[:, :, sl(t)] = (o_ref[:, :, sl(t)] - mu) * rstd * lg_ref[:, :, sl(t)] + lb_ref[:, :, sl(t)]


def _final(dest, yb, x1, mod3, row0, wts, t_off, ln_g, ln_b):
    B, L, D = x1.shape
    T = B * L
    bb, tl = _row_tiling(B, L, TM_MIX)
    nl = L // tl
    rows = bb * tl
    mrow = row0 // bb
    ntiles = (B // bb) * nl
    woff = t_off // rows
    dest_tiles = dest[2 * t_off:2 * (t_off + T)].reshape(T // rows, rows, 2).transpose(0, 2, 1).reshape(-1)
    kern = functools.partial(_final_kernel, rows=rows, ntiles=ntiles)
    grid_spec = pltpu.PrefetchScalarGridSpec(
        num_scalar_prefetch=1,
        grid=(ntiles,),
        in_specs=[
            pl.BlockSpec(memory_space=pl.ANY),
            pl.BlockSpec((bb, tl, D), lambda i, d: (i // nl, i % nl, 0)),
            pl.BlockSpec((bb, 1, D), lambda i, d: (mrow + i // nl, 0, 5)),
            pl.BlockSpec((rows, LANE), lambda i, d: (woff + i, 0)),
            pl.BlockSpec((1, 1, D), lambda i, d: (0, 0, 0)),
            pl.BlockSpec((1, 1, D), lambda i, d: (0, 0, 0)),
        ],
        out_specs=pl.BlockSpec((bb, tl, D), lambda i, d: (i // nl, i % nl, 0)),
        scratch_shapes=[
            pltpu.VMEM((2, 2 * rows, D), F32),
            pltpu.SemaphoreType.DMA((2,)),
        ],
    )
    return pl.pallas_call(
        kern,
        grid_spec=grid_spec,
        out_shape=jax.ShapeDtypeStruct((B, L, D), F32),
        compiler_params=_cparams(("arbitrary",)),
        name="combine_ln2",
    )(dest_tiles, yb, x1, mod3, wts, ln_g.reshape(1, 1, D), ln_b.reshape(1, 1, D))


def _pad_keys(t, S_pad):
    S = t.shape[1]
    if S == S_pad:
        return t
    return jnp.pad(t, ((0, 0), (0, S_pad - S), (0, 0)))


def _mixers(x3, mod3, row0, conv_state, gdn_state, past_k, past_v, past_ik, wp, h2_all, t_all, t_off):
    B, L, D = x3.shape
    pos0 = past_k.shape[1]
    S = pos0 + L
    proj = _inproj(x3, mod3, row0, wp["w_in"])

    oa, new_gdn = _gdn(proj, B, L, conv_state, gdn_state, wp["conv_w"], wp["a_log"], wp["dt_bias"],
                       wp["gdn_norm_w"])
    new_conv = proj.reshape(B, L, NP)[:, L - (CONV_W - 1):, C_QKV:C_QKV + GDN_CONV_CH]

    qb = min(Q_BLOCK, L)
    q_hm, kf, k16, vf, v16, iq_hm, ikf, ik16 = _rope(proj, B, L, pos0, qb)
    S_pad = ((S + TK_ATT - 1) // TK_ATT) * TK_ATT
    k_all = _pad_keys(jnp.concatenate([past_k.reshape(B, pos0, ATT_KV).astype(BF16), k16.reshape(B, L, ATT_KV)], 1), S_pad)
    v_all = _pad_keys(jnp.concatenate([past_v.reshape(B, pos0, ATT_KV).astype(BF16), v16.reshape(B, L, ATT_KV)], 1), S_pad)
    ik_all = _pad_keys(jnp.concatenate([past_ik.astype(BF16), ik16.reshape(B, L, IDX_HD)], 1), S_pad)
    ob = _dsa(iq_hm, proj, q_hm, ik_all, k_all, v_all, B, L, S, pos0)

    m = _merge(oa, ob, wp["w_br_a"], wp["w_br_b"], proj)
    x1, h2_all = _oproj(m, wp["w_o"], x3, mod3, row0, wp["ln1_g"], wp["ln1_b"], h2_all, t_all, t_off)

    kb_out = kf.reshape(B, L, ATT_KV_HEADS, ATT_HD)
    vb_out = vf.reshape(B, L, ATT_KV_HEADS, ATT_HD)
    ik_out = ikf.reshape(B, L, IDX_HD)
    return x1, h2_all, (new_conv, new_gdn, kb_out, vb_out, ik_out)


def _permute_w_in(w_in):
    sizes = (GDN_CONV_CH, GDN_HEADS, GDN_HEADS, GDN_VAL, ATT_Q, ATT_KV, ATT_KV, IDX_HEADS * IDX_HD, IDX_HD,
             IDX_HEADS, 2 * D_MODEL)
    parts, off = [], 0
    for s in sizes:
        parts.append(w_in[:, off:off + s])
        off += s
    qkv, a, b, z, q, k, v, iq, ik, iw, gates = parts
    pad = jnp.zeros((w_in.shape[0], SM_W - 2 * GDN_HEADS - IDX_HEADS), w_in.dtype)
    return jnp.concatenate([qkv, z, iq, q, k, v, ik, a, b, iw, pad, gates], axis=1).astype(BF16)


def kernel(x_prompt, x_sample, c_prompt, c_sample, state_conv, state_gdn, cache_k, cache_v, cache_idx_k,
           w_ada, b_ada, w_in, conv_w, a_log, dt_bias, gdn_norm_w, w_br_a, w_br_b, w_o,
           ln1_g, ln1_b, ln2_g, ln2_b, w_grp, b_grp, w_rtr, b_rtr, w_up, w_down):
    Bp, Lp, D = x_prompt.shape
    Bs, Ls, _ = x_sample.shape
    l = 0
    row0_p = Bs
    nrows = ((Bs + Bp + 7) // 8) * 8
    c_all = jnp.concatenate([c_sample, c_prompt, jnp.zeros((nrows - Bs - Bp, D), F32)], 0)
    mod = _ada(c_all, w_ada[l], b_ada[l])
    mod3 = mod.reshape(nrows, 1, 6 * D)

    nr = LANE - N_GROUPS - N_EXPERTS
    wp = dict(
        w_in=_permute_w_in(w_in[l]),
        conv_w=conv_w[l], a_log=a_log[l], dt_bias=dt_bias[l], gdn_norm_w=gdn_norm_w[l],
        w_br_a=w_br_a[l].astype(BF16), w_br_b=w_br_b[l].astype(BF16), w_o=w_o[l].astype(BF16),
        ln1_g=ln1_g[l], ln1_b=ln1_b[l], ln2_g=ln2_g[l], ln2_b=ln2_b[l],
        w_r=jnp.concatenate([w_grp[l], w_rtr[l], jnp.zeros((D, nr), F32)], 1),
        b_r=jnp.concatenate([b_grp[l], b_rtr[l], jnp.zeros((nr,), F32)]).reshape(1, LANE),
        w_up=w_up[l].astype(BF16), w_down=w_down[l].astype(BF16),
    )

    zc = jnp.zeros((Bp, CONV_W - 1, GDN_CONV_CH), F32)
    zs = jnp.zeros((Bp, GDN_HEADS, GDN_DK, GDN_DV), F32)
    zk = jnp.zeros((Bp, 0, ATT_KV_HEADS, ATT_HD), F32)
    zik = jnp.zeros((Bp, 0, IDX_HD), F32)
    Tp, Ts = Bp * Lp, Bs * Ls
    t_all = Tp + Ts
    h2_all = jnp.zeros((t_all, D), F32)
    x1p, h2_all, (c1, g1, k1, v1, i1) = _mixers(x_prompt, mod3, row0_p, zc, zs, zk, zk, zik, wp, h2_all, t_all, 0)
    x1s, h2_all, (c2, g2, k2, v2, i2) = _mixers(x_sample, mod3, 0, state_conv[l], state_gdn[l], cache_k[l],
                                                 cache_v[l], cache_idx_k[l], wp, h2_all, t_all, Tp)

    ids, wts = _router(h2_all, wp["w_r"], wp["b_r"])
    dest, slot_tok, blk_e, nused = _route_metadata(ids, t_all, MOE_BM)
    yb = _experts(h2_all, blk_e, slot_tok, nused, wp["w_up"], wp["w_down"], MOE_BM)
    yp = _final(dest, yb, x1p, mod3, row0_p, wts, 0, wp["ln2_g"], wp["ln2_b"])
    ys = _final(dest, yb, x1s, mod3, 0, wts, Tp, wp["ln2_g"], wp["ln2_b"])
    st = lambda t: t[None]
    return (yp, ys, st(c1), st(g1), st(k1), st(v1), st(i1), st(c2), st(g2), st(k2), st(v2), st(i2))
```

```python
import functools

import jax
import jax.numpy as jnp
from jax import lax
from jax.experimental import pallas as pl
from jax.experimental.pallas import tpu as pltpu

F32 = jnp.float32
BF16 = jnp.bfloat16

D_MODEL = 4096
CHUNK = 64
GDN_HEADS = D_MODEL // 256
GDN_DK = 128
GDN_DV = 128
GDN_KEY = GDN_HEADS * GDN_DK
GDN_VAL = GDN_HEADS * GDN_DV
GDN_CONV_CH = 2 * GDN_KEY + GDN_VAL
CONV_W = 4
ATT_HEADS = D_MODEL // 256
ATT_KV_HEADS = ATT_HEADS // 4
ATT_REP = ATT_HEADS // ATT_KV_HEADS
ATT_HD = 128
ATT_Q = ATT_HEADS * ATT_HD
ATT_KV = ATT_KV_HEADS * ATT_HD
IDX_HEADS = D_MODEL // 128
IDX_HD = 128
TOPK_KEYS = 256
Q_BLOCK = 128
ROPE_THETA = 500000.0
ROPE_ROT = ATT_HD // 4
ROPE_HALF = ROPE_ROT // 2
N_GROUPS = 4
EXP_PER_GROUP = 8
N_EXPERTS = N_GROUPS * EXP_PER_GROUP
D_EXPERT = D_MODEL // 4
DEPTH = 1
ALPHA = (2.0 * DEPTH) ** 0.25
LN_EPS = 1e-5
RMS_EPS = 1e-6

C_QKV = 0
C_Z = C_QKV + GDN_CONV_CH
C_IQ = C_Z + GDN_VAL
C_Q = C_IQ + IDX_HEADS * IDX_HD
C_K = C_Q + ATT_Q
C_V = C_K + ATT_KV
C_IK = C_V + ATT_KV
C_SM = C_IK + IDX_HD
SM_W = 384
C_GA = C_SM + SM_W
C_GB = C_GA + D_MODEL
NP = C_GB + D_MODEL
SM_A, SM_B, SM_IW = 0, GDN_HEADS, 2 * GDN_HEADS

LANE = 128
NEG_BIG = -1e30
VMEM_LIMIT = 56 * 1024 * 1024

TN_IN = 512
TM_IN = 1024
TM_MIX = 256
TN_MIX = 512
TK_ATT = 512
FC_MOE = 256
MOE_BM = 256


def _cparams(sem):
    return pltpu.CompilerParams(dimension_semantics=sem, vmem_limit_bytes=VMEM_LIMIT)


def _dot(a, b):
    return jnp.dot(a, b, preferred_element_type=F32)


def _dot_nt(a, b):
    return lax.dot_general(a, b, (((1,), (1,)), ((), ())), preferred_element_type=F32)


def _dot_tn(a, b):
    return lax.dot_general(a, b, (((0,), (0,)), ((), ())), preferred_element_type=F32)


def _split_bf16(a):
    hi = a.astype(BF16)
    lo = (a - hi.astype(F32)).astype(BF16)
    return hi, lo


def _mm3(a, b, dot=_dot):
    ah, al = _split_bf16(a)
    bh, bl = _split_bf16(b)
    return dot(ah, bh) + (dot(ah, bl) + dot(al, bh))


def _mm1(a, b):
    return _dot(a.astype(BF16), b.astype(BF16))


def _mm1_nt(a, b):
    return _dot_nt(a.astype(BF16), b.astype(BF16))


def _sigmoid(x):
    return 1.0 / (1.0 + jnp.exp(-x))


def _silu(x):
    return x * _sigmoid(x)


def _softplus(x):
    return jnp.maximum(x, 0.0) + jnp.log(1.0 + jnp.exp(-jnp.abs(x)))


def _ada_kernel(c_ref, w_ref, b_ref, o_ref):
    s = _silu(c_ref[...])
    o_ref[...] = _dot(s.astype(BF16), w_ref[...].astype(BF16)) + b_ref[...]


def _ada(c_all, w_ada, b_ada):
    R, D = c_all.shape
    N = w_ada.shape[1]
    tn = 512
    return pl.pallas_call(
        _ada_kernel,
        grid=(N // tn,),
        in_specs=[
            pl.BlockSpec((R, D), lambda j: (0, 0)),
            pl.BlockSpec((D, tn), lambda j: (0, j)),
            pl.BlockSpec((1, tn), lambda j: (0, j)),
        ],
        out_specs=pl.BlockSpec((R, tn), lambda j: (0, j)),
        out_shape=jax.ShapeDtypeStruct((R, N), F32),
        compiler_params=_cparams(("parallel",)),
        name="ada",
    )(c_all, w_ada, b_ada.reshape(1, N))


def _inproj_kernel(x_ref, sc_ref, sh_ref, w_ref, o_ref, h_ref):
    @pl.when(pl.program_id(1) == 0)
    def _():
        h = x_ref[...] * (1.0 + sc_ref[...]) + sh_ref[...]
        h_ref[...] = h.reshape(h_ref.shape).astype(BF16)

    o_ref[...] = _dot(h_ref[...], w_ref[...])


def _row_tiling(B, L, tm):
    if L >= tm:
        return 1, tm
    bb = max(1, min(B, tm // L))
    while B % bb:
        bb -= 1
    return bb, L


def _inproj(x3, mod3, row0, w_perm):
    B, L, D = x3.shape
    bb, tl = _row_tiling(B, L, TM_IN)
    nl = L // tl
    rows = bb * tl
    mrow = row0 // bb
    return pl.pallas_call(
        _inproj_kernel,
        grid=((B // bb) * nl, NP // TN_IN),
        in_specs=[
            pl.BlockSpec((bb, tl, D), lambda i, j: (i // nl, i % nl, 0), pipeline_mode=pl.Buffered(1)),
            pl.BlockSpec((bb, 1, D), lambda i, j: (mrow + i // nl, 0, 1)),
            pl.BlockSpec((bb, 1, D), lambda i, j: (mrow + i // nl, 0, 0)),
            pl.BlockSpec((D, TN_IN), lambda i, j: (0, j)),
        ],
        out_specs=pl.BlockSpec((rows, TN_IN), lambda i, j: (i, j)),
        out_shape=jax.ShapeDtypeStruct((B * L, NP), F32),
        scratch_shapes=[pltpu.VMEM((rows, D), BF16)],
        compiler_params=_cparams(("parallel", "arbitrary")),
        name="inproj",
    )(x3, mod3, mod3, w_perm)


GDN_GROUP = 4


def _split3_bf16(a):
    hi = a.astype(BF16)
    r = a - hi.astype(F32)
    mid = r.astype(BF16)
    lo = (r - mid.astype(F32)).astype(BF16)
    return hi, mid, lo


def _dot_exact01(a, b01, a_is_01=False):
    if a_is_01:
        h, m, l = _split3_bf16(b01)
        return _dot(a, h) + (_dot(a, m) + _dot(a, l))
    h, m, l = _split3_bf16(a)
    return _dot(h, b01) + (_dot(m, b01) + _dot(l, b01))


def _mm3p(a_hl, b_hl, dot=_dot):
    (ah, al), (bh, bl) = a_hl, b_hl
    return dot(ah, bh) + (dot(ah, bl) + dot(al, bh))


def _block_rows_hl(x_hl, nblk, mask01):
    return tuple(jnp.concatenate([p] * nblk, axis=0) * mask01 for p in x_hl)


def _gdn_kernel(qkv_ref, z_ref, sm_ref, cst_ref, cw_ref, alr_ref, dtr_ref, nw_ref, s0_ref,
                o_ref, sfin_ref, s_ref, ext_ref, mt_ref, mk_ref, mx_ref, ml_ref, *, C, nchunks):
    n = pl.program_id(1)
    H = GDN_HEADS
    G = GDN_GROUP
    NG = H // G
    GC = G * C
    PADR = 8
    logc = C.bit_length() - 1

    @pl.when(n == 0)
    def _():
        s_ref[...] = s0_ref[0]
        ext_ref[PADR - (CONV_W - 1):PADR, :] = cst_ref[0]

    ext_ref[PADR:PADR + C, :] = qkv_ref[...]
    cw = cw_ref[...]
    y = ext_ref[PADR - 3:PADR - 3 + C, :] * cw[0:1]
    for jw in range(1, CONV_W):
        y = y + ext_ref[PADR - 3 + jw:PADR - 3 + jw + C, :] * cw[jw:jw + 1]
    y = _silu(y)
    hist = ext_ref[PADR + C - (CONV_W - 1):PADR + C, :]
    ext_ref[PADR - (CONV_W - 1):PADR, :] = hist

    sm = sm_ref[...]
    g_col = -jnp.exp(alr_ref[...]) * _softplus(sm[:, SM_A:SM_A + H] + dtr_ref[...])
    beta_col = _sigmoid(sm[:, SM_B:SM_B + H])

    ii = lax.broadcasted_iota(jnp.int32, (C, C), 0)
    jj = lax.broadcasted_iota(jnp.int32, (C, C), 1)
    tril01 = jnp.where(jj <= ii, 1.0, 0.0).astype(BF16)
    gc_col = _dot_exact01(tril01, g_col, a_is_01=True)
    egc_col = jnp.exp(gc_col)
    gc_last = gc_col[C - 1:C, :]
    ekd_col = jnp.exp(gc_last - gc_col)
    egl = jnp.exp(gc_last)

    W = H * C
    hrow = lax.broadcasted_iota(jnp.int32, (H, W), 0)
    hlane = jnp.right_shift(lax.broadcasted_iota(jnp.int32, (H, W), 1), logc)
    e_seg = jnp.where(hrow == hlane, 1.0, 0.0).astype(BF16)
    gseg = _dot_exact01(gc_col, e_seg)
    ri = lax.broadcasted_iota(jnp.int32, (C, W), 0)
    cj = jnp.bitwise_and(lax.broadcasted_iota(jnp.int32, (C, W), 1), C - 1)
    grow = jnp.sum(jnp.where(ri == cj, gseg, 0.0), axis=0, keepdims=True)
    decay_all = jnp.where(cj <= ri, jnp.exp(gseg - grow), 0.0)

    gi = lax.broadcasted_iota(jnp.int32, (C, GC), 0)
    gj = jnp.bitwise_and(lax.broadcasted_iota(jnp.int32, (C, GC), 1), C - 1)
    strict = gj < gi
    eye = jnp.where(gi == gj, 1.0, 0.0)
    nlev = logc

    def lev_mask(lev):
        return ((jnp.right_shift(gi, lev + 1) == jnp.right_shift(gj, lev + 1))
                & (jnp.bitwise_and(jnp.right_shift(gi, lev), 1) == 1)
                & (jnp.bitwise_and(jnp.right_shift(gj, lev), 1) == 0))

    @pl.when(n == 0)
    def _():
        def own(ncols, col_head):
            r = jnp.right_shift(lax.broadcasted_iota(jnp.int32, (G * C, ncols), 0), logc)
            c = col_head(lax.broadcasted_iota(jnp.int32, (G * C, ncols), 1))
            return jnp.where(r == c, 1.0, 0.0).astype(BF16)

        mt_ref[...] = own(GC, lambda c: jnp.right_shift(c, logc))
        mk_ref[...] = own(G * GDN_DK, lambda c: jnp.right_shift(c, 7))
        mx_ref[...] = own(2 * G * GDN_DK, lambda c: jnp.bitwise_and(jnp.right_shift(c, 7), G - 1))
        for lev in range(1, nlev):
            ml_ref[lev] = jnp.where(lev_mask(lev), 1.0, 0.0).astype(BF16)

    nw = nw_ref[...]
    z = z_ref[...]

    lmats, lm_hls, intras, rhss, qds, kds = [], [], [], [], [], []
    for g in range(NG):
        qs, ks, kbs, vbs, kes = [], [], [], [], []
        for t in range(G):
            h = g * G + t
            qh = y[:, h * GDN_DK:(h + 1) * GDN_DK]
            kh = y[:, GDN_KEY + h * GDN_DK:GDN_KEY + (h + 1) * GDN_DK]
            vh = y[:, 2 * GDN_KEY + h * GDN_DV:2 * GDN_KEY + (h + 1) * GDN_DV]
            q = qh * lax.rsqrt(jnp.sum(qh * qh, -1, keepdims=True) + RMS_EPS) * (GDN_DK ** -0.5)
            k = kh * lax.rsqrt(jnp.sum(kh * kh, -1, keepdims=True) + RMS_EPS)
            beta = beta_col[:, h:h + 1]
            egc = egc_col[:, h:h + 1]
            kb = k * beta
            qs.append(q)
            ks.append(k)
            kbs.append(kb)
            vbs.append(vh * beta)
            kes.append(kb * egc)
            qds.append(q * egc)
            kds.append(k * ekd_col[:, h:h + 1])
        bdk_hl = _block_rows_hl(_split_bf16(jnp.concatenate(ks, axis=-1)), G, mk_ref[...])
        dec = decay_all[:, g * GC:(g + 1) * GC]
        kk = _mm3p(_split_bf16(jnp.concatenate(kbs, axis=-1)), bdk_hl, _dot_nt)
        qk = _dot_nt(jnp.concatenate(qs, axis=-1).astype(BF16), bdk_hl[0])
        lmat = jnp.where(strict, kk * dec, 0.0)
        lmats.append(lmat)
        lm_hls.append(_split_bf16(lmat))
        intras.append((qk * dec).astype(BF16))
        rhss.append(_split_bf16(jnp.concatenate(vbs + kes, axis=-1)))

    tinvs = [eye - jnp.where(lev_mask(0), lmats[g], 0.0) for g in range(NG)]
    for lev in range(1, nlev):
        ml = ml_ref[lev]
        for g in range(NG):
            t_hl = _split_bf16(tinvs[g])
            b_hl = (lm_hls[g][0] * ml, lm_hls[g][1] * ml)
            p = _mm3p(b_hl, _block_rows_hl(t_hl, G, mt_ref[...]))
            tinvs[g] = tinvs[g] - _mm3p(t_hl, _block_rows_hl(_split_bf16(p), G, mt_ref[...]))

    for g in range(NG):
        x = _mm3p(_split_bf16(tinvs[g]), _block_rows_hl(rhss[g], G, mx_ref[...]))
        u_cat = x[:, :G * GDN_DV]
        w_cat = x[:, G * GDN_DV:]
        for t in range(G):
            h = g * G + t
            s = s_ref[h]
            s_hl = _split_bf16(s)
            lanes = slice(t * GDN_DV, (t + 1) * GDN_DV)
            v_new = u_cat[:, lanes] - _mm3p(_split_bf16(w_cat[:, lanes]), s_hl)
            v_hl = _split_bf16(v_new)
            o = _dot(qds[h].astype(BF16), s_hl[0]) + _dot(intras[g][:, t * C:(t + 1) * C], v_hl[0])
            s_ref[h] = s * egl[:, h:h + 1] + _mm3p(_split_bf16(kds[h]), v_hl, _dot_tn)
            zh = z[:, h * GDN_DV:(h + 1) * GDN_DV]
            on = o * lax.rsqrt(jnp.mean(o * o, -1, keepdims=True) + RMS_EPS) * nw * _silu(zh)
            o_ref[:, h * GDN_DV:(h + 1) * GDN_DV] = on.astype(o_ref.dtype)

    @pl.when(n == nchunks - 1)
    def _():
        sfin_ref[0] = s_ref[...]


def _gdn(proj, B, L, conv_state, gdn_state, conv_w, a_log, dt_bias, norm_w):
    C = min(CHUNK, L)
    N = L // C
    H = GDN_HEADS
    kern = functools.partial(_gdn_kernel, C=C, nchunks=N)
    return pl.pallas_call(
        kern,
        grid=(B, N),
        in_specs=[
            pl.BlockSpec((C, GDN_CONV_CH), lambda b, n: (b * N + n, C_QKV // GDN_CONV_CH)),
            pl.BlockSpec((C, GDN_VAL), lambda b, n: (b * N + n, C_Z // GDN_VAL)),
            pl.BlockSpec((C, LANE), lambda b, n: (b * N + n, C_SM // LANE)),
            pl.BlockSpec((1, CONV_W - 1, GDN_CONV_CH), lambda b, n: (b, 0, 0)),
            pl.BlockSpec((CONV_W, GDN_CONV_CH), lambda b, n: (0, 0)),
            pl.BlockSpec((1, H), lambda b, n: (0, 0)),
            pl.BlockSpec((1, H), lambda b, n: (0, 0)),
            pl.BlockSpec((1, GDN_DV), lambda b, n: (0, 0)),
            pl.BlockSpec((1, H, GDN_DK, GDN_DV), lambda b, n: (b, 0, 0, 0)),
        ],
        out_specs=[
            pl.BlockSpec((C, GDN_VAL), lambda b, n: (b * N + n, 0)),
            pl.BlockSpec((1, H, GDN_DK, GDN_DV), lambda b, n: (b, 0, 0, 0)),
        ],
        out_shape=[
            jax.ShapeDtypeStruct((B * L, GDN_VAL), BF16),
            jax.ShapeDtypeStruct((B, H, GDN_DK, GDN_DV), F32),
        ],
        scratch_shapes=[
            pltpu.VMEM((H, GDN_DK, GDN_DV), F32),
            pltpu.VMEM((8 + C, GDN_CONV_CH), F32),
            pltpu.VMEM((GDN_GROUP * C, GDN_GROUP * C), BF16),
            pltpu.VMEM((GDN_GROUP * C, GDN_GROUP * GDN_DK), BF16),
            pltpu.VMEM((GDN_GROUP * C, 2 * GDN_GROUP * GDN_DK), BF16),
            pltpu.VMEM((C.bit_length() - 1, C, GDN_GROUP * C), BF16),
        ],
        compiler_params=_cparams(("parallel", "arbitrary")),
        name="gdn",
    )(proj, proj, proj, conv_state, conv_w, a_log.reshape(1, H), dt_bias.reshape(1, H),
      norm_w.reshape(1, GDN_DV), gdn_state)


def _rope_kernel(q_ref, k_ref, v_ref, iq_ref, ik_ref, cos_ref, sa_ref, sb_ref,
                 qo_ref, kf_ref, kb_ref, vf_ref, vb_ref, iqo_ref, ikf_ref, ikb_ref):
    cosf = cos_ref[...]
    sa = sa_ref[...]
    sb = sb_ref[...]

    def rope(x):
        return (x * cosf + pltpu.roll(x, LANE - ROPE_HALF, 1) * sa + pltpu.roll(x, ROPE_HALF, 1) * sb)

    for h in range(ATT_HEADS):
        qo_ref[0, h] = rope(q_ref[:, h * ATT_HD:(h + 1) * ATT_HD]).astype(BF16)
    for h in range(ATT_KV_HEADS):
        kr = rope(k_ref[:, h * ATT_HD:(h + 1) * ATT_HD])
        kf_ref[:, h * ATT_HD:(h + 1) * ATT_HD] = kr
        kb_ref[:, h * ATT_HD:(h + 1) * ATT_HD] = kr.astype(BF16)
    v = v_ref[...]
    vf_ref[...] = v
    vb_ref[...] = v.astype(BF16)
    for h in range(IDX_HEADS):
        iqo_ref[0, h] = rope(iq_ref[:, h * IDX_HD:(h + 1) * IDX_HD]).astype(BF16)
    ikr = rope(ik_ref[...])
    ikf_ref[...] = ikr
    ikb_ref[...] = ikr.astype(BF16)


def _rope_tables(L, pos0, reps):
    inv = jnp.power(ROPE_THETA, -jnp.arange(ROPE_HALF, dtype=F32) * (2.0 / ROPE_ROT))
    ang = (pos0 + jnp.arange(L)).astype(F32)[:, None] * inv[None, :]
    cos, sin = jnp.cos(ang), jnp.sin(ang)
    z16 = jnp.zeros((L, ROPE_HALF), F32)
    rest0 = jnp.zeros((L, ATT_HD - ROPE_ROT), F32)
    cosf = jnp.concatenate([cos, cos, jnp.ones((L, ATT_HD - ROPE_ROT), F32)], -1)
    sa = jnp.concatenate([-sin, z16, rest0], -1)
    sb = jnp.concatenate([z16, sin, rest0], -1)
    if reps > 1:
        cosf, sa, sb = (jnp.tile(t, (reps, 1)) for t in (cosf, sa, sb))
    return cosf, sa, sb


def _rope(proj, B, L, pos0, qb):
    T = B * L
    reps = 1
    tab_rows = L
    cosf, sa, sb = _rope_tables(L, pos0, reps)
    nq = L // qb
    tspec = pl.BlockSpec((qb, LANE), lambda i: (i % nq, 0))
    outs = pl.pallas_call(
        _rope_kernel,
        grid=(T // qb,),
        in_specs=[
            pl.BlockSpec((qb, ATT_Q), lambda i: (i, C_Q // ATT_Q)),
            pl.BlockSpec((qb, ATT_KV), lambda i: (i, C_K // ATT_KV)),
            pl.BlockSpec((qb, ATT_KV), lambda i: (i, C_V // ATT_KV)),
            pl.BlockSpec((qb, IDX_HEADS * IDX_HD), lambda i: (i, C_IQ // (IDX_HEADS * IDX_HD))),
            pl.BlockSpec((qb, IDX_HD), lambda i: (i, C_IK // IDX_HD)),
            tspec, tspec, tspec,
        ],
        out_specs=[
            pl.BlockSpec((1, ATT_HEADS, qb, ATT_HD), lambda i: (i, 0, 0, 0)),
            pl.BlockSpec((qb, ATT_KV), lambda i: (i, 0)),
            pl.BlockSpec((qb, ATT_KV), lambda i: (i, 0)),
            pl.BlockSpec((qb, ATT_KV), lambda i: (i, 0)),
            pl.BlockSpec((qb, ATT_KV), lambda i: (i, 0)),
            pl.BlockSpec((1, IDX_HEADS, qb, IDX_HD), lambda i: (i, 0, 0, 0)),
            pl.BlockSpec((qb, IDX_HD), lambda i: (i, 0)),
            pl.BlockSpec((qb, IDX_HD), lambda i: (i, 0)),
        ],
        out_shape=[
            jax.ShapeDtypeStruct((T // qb, ATT_HEADS, qb, ATT_HD), BF16),
            jax.ShapeDtypeStruct((T, ATT_KV), F32),
            jax.ShapeDtypeStruct((T, ATT_KV), BF16),
            jax.ShapeDtypeStruct((T, ATT_KV), F32),
            jax.ShapeDtypeStruct((T, ATT_KV), BF16),
            jax.ShapeDtypeStruct((T // qb, IDX_HEADS, qb, IDX_HD), BF16),
            jax.ShapeDtypeStruct((T, IDX_HD), F32),
            jax.ShapeDtypeStruct((T, IDX_HD), BF16),
        ],
        compiler_params=_cparams(("parallel",)),
        name="rope",
    )(proj, proj, proj, proj, proj, cosf, sa, sb)
    return outs


def _dsa_kernel(iq_ref, sm_ref, q_ref, ik_ref, k_ref, v_ref, o_ref, sc_ref, *, qb, S, pos0, n_sel):
    j = pl.program_id(1)
    TK = TK_ATT
    q_last = pos0 + (j + 1) * qb - 1
    lim = jnp.minimum(((q_last // CHUNK) + 1) * CHUNK, S)
    nkt = (lim + TK - 1) // TK
    qpos = pos0 + j * qb + lax.broadcasted_iota(jnp.int32, (qb, 1), 0)
    qlim = jnp.minimum((jnp.right_shift(qpos, CHUNK.bit_length() - 1) + 1) * CHUNK, S)
    iw = sm_ref[:, SM_IW:SM_IW + IDX_HEADS] * ((IDX_HEADS ** -0.5) * (IDX_HD ** -0.5))
    HG = 4

    def score_tile(kt, carry):
        ks = pl.multiple_of(kt * TK, TK)
        ik_t = ik_ref[0, pl.ds(ks, TK), :]
        acc = jnp.zeros((qb, TK), F32)
        for hg in range(IDX_HEADS // HG):
            iq_g = iq_ref[0, hg * HG:(hg + 1) * HG].reshape(HG * qb, IDX_HD)
            r = jnp.maximum(_dot_nt(iq_g, ik_t), 0.0)
            for t in range(HG):
                hh = hg * HG + t
                acc = acc + iw[:, hh:hh + 1] * r[t * qb:(t + 1) * qb]
        kpos = ks + lax.broadcasted_iota(jnp.int32, (qb, TK), 1)
        sc_ref[kt] = jnp.where(kpos < qlim, acc, -jnp.inf)
        return carry

    lax.fori_loop(0, nkt, score_tile, 0)

    def lane_fold(m):
        p = m[:, 0:LANE]
        for t in range(1, TK // LANE):
            p = p + m[:, t * LANE:(t + 1) * LANE]
        return p

    def count_ge(x):
        def body(kt, c):
            return c + lane_fold(jnp.where(sc_ref[kt] >= x, 1.0, 0.0))
        part = lax.fori_loop(0, nkt, body, jnp.zeros((qb, LANE), F32))
        return jnp.sum(part, axis=1, keepdims=True)

    def minmax(kt, c):
        lo, hi = c
        t = sc_ref[kt]
        lo = jnp.minimum(lo, jnp.min(jnp.where(t > -jnp.inf, t, jnp.inf), axis=1, keepdims=True))
        hi = jnp.maximum(hi, jnp.max(t, axis=1, keepdims=True))
        return lo, hi

    lo0, hi0 = lax.fori_loop(0, nkt, minmax,
                             (jnp.full((qb, 1), jnp.inf, F32), jnp.full((qb, 1), -jnp.inf, F32)))
    kf = float(n_sel)
    cnt_all = count_ge(lo0)
    cnt_hi = count_ge(hi0)
    top_tied = cnt_hi >= kf
    fixed = (cnt_all <= kf) | top_tied
    lo_init = jnp.where(top_tied, hi0, lo0)

    def probe(lo, hi, clo, done):
        mid = lo + (hi - lo) * 0.5
        stop = (mid <= lo) | (mid >= hi) | (done > 0.0)
        c = count_ge(mid)
        up = jnp.logical_and(jnp.logical_not(stop), c >= kf)
        dn = jnp.logical_and(jnp.logical_not(stop), c < kf)
        done2 = jnp.where(stop | (c == kf), 1.0, 0.0)
        return jnp.where(up, mid, lo), jnp.where(dn, mid, hi), jnp.where(up, c, clo), done2

    def cond(st):
        return jnp.logical_and(st[4] > 0, st[5] < 128)

    def body(st):
        lo, hi, clo, done, _, it = st
        lo, hi, clo, done = probe(lo, hi, clo, done)
        lo, hi, clo, done = probe(lo, hi, clo, done)
        nact = jnp.sum(jnp.where(done > 0.0, 0, 1))
        return lo, hi, clo, done, nact, it + 1

    done0 = jnp.where(fixed, 1.0, 0.0)
    clo0 = jnp.where(top_tied, cnt_hi, cnt_all)
    thr, _, cthr, _, _, _ = lax.while_loop(cond, body, (lo_init, hi0, clo0, done0, jnp.int32(1), jnp.int32(0)))

    s_end = sc_ref.shape[0] * TK

    def count_tied(op, x):
        def body(kt, c):
            t = sc_ref[kt]
            if op == "gt":
                hit = t > thr
            else:
                kpos = kt * TK + lax.broadcasted_iota(jnp.int32, (qb, TK), 1)
                hit = jnp.logical_and(t == thr, kpos < x)
            return c + lane_fold(jnp.where(hit, 1.0, 0.0))
        part = lax.fori_loop(0, nkt, body, jnp.zeros((qb, LANE), F32))
        return jnp.sum(part, axis=1, keepdims=True)

    def tie_bound(_):
        need = kf - count_tied("gt", None)

        def step(_, st):
            lo_i, hi_i = st
            mid = jnp.right_shift(lo_i + hi_i, 1)
            ge = count_tied("eq", mid) >= need
            return jnp.where(ge, lo_i, mid + 1), jnp.where(ge, mid, hi_i)

        _, hi_i = lax.fori_loop(0, s_end.bit_length(), step,
                                (jnp.zeros((qb, 1), jnp.int32), jnp.full((qb, 1), s_end, jnp.int32)))
        return jnp.where(cthr > kf, hi_i, s_end)

    n_over = jnp.sum(jnp.where(cthr > kf, 1, 0))
    ibound = lax.cond(n_over > 0, tie_bound, lambda _: jnp.full((qb, 1), s_end, jnp.int32), 0)

    def to_bias(kt, carry):
        t = sc_ref[kt]
        kpos = kt * TK + lax.broadcasted_iota(jnp.int32, (qb, TK), 1)
        keep = (t > thr) | ((t == thr) & (kpos < ibound))
        sc_ref[kt] = jnp.where(keep, 0.0, NEG_BIG)
        return carry

    lax.fori_loop(0, nkt, to_bias, 0)

    R = ATT_REP
    for g in range(ATT_KV_HEADS):
        qg = q_ref[0, g * R:(g + 1) * R].reshape(R * qb, ATT_HD)

        def att_tile(kt, carry, g=g, qg=qg):
            m, l, acc = carry
            ks = pl.multiple_of(kt * TK, TK)
            k_t = k_ref[0, pl.ds(ks, TK), g * ATT_HD:(g + 1) * ATT_HD]
            v_t = v_ref[0, pl.ds(ks, TK), g * ATT_HD:(g + 1) * ATT_HD]
            s = _dot_nt(qg, k_t) * (ATT_HD ** -0.5)
            s = (s.reshape(R, qb, TK) + sc_ref[kt][None]).reshape(R * qb, TK)
            m_new = jnp.maximum(m, jnp.max(s, axis=1, keepdims=True))
            p = jnp.exp(s - m_new)
            a = jnp.exp(m - m_new)
            l = a * l + jnp.sum(p, axis=1, keepdims=True)
            acc = a * acc + _dot(p.astype(BF16), v_t)
            return m_new, l, acc

        def att_pair(i, carry, att_tile=att_tile):
            return att_tile(2 * i + 1, att_tile(2 * i, carry))

        m0 = jnp.full((R * qb, 1), NEG_BIG, F32)
        l0 = jnp.zeros((R * qb, 1), F32)
        a0 = jnp.zeros((R * qb, ATT_HD), F32)
        carry = lax.fori_loop(0, nkt // 2, att_pair, (m0, l0, a0))
        _, l, acc = lax.cond(nkt % 2 == 1, lambda c, att_tile=att_tile: att_tile(nkt - 1, c), lambda c: c, carry)
        out = acc / l
        for r in range(R):
            hh = g * R + r
            o_ref[:, hh * ATT_HD:(hh + 1) * ATT_HD] = out[r * qb:(r + 1) * qb].astype(o_ref.dtype)


def _dsa(iq_hm, proj, q_hm, ik_all, k_all, v_all, B, L, S, pos0):
    qb = min(Q_BLOCK, L)
    nq = L // qb
    S_pad = ik_all.shape[1]
    n_sel = min(TOPK_KEYS, S // 4)
    kern = functools.partial(_dsa_kernel, qb=qb, S=S, pos0=pos0, n_sel=n_sel)
    return pl.pallas_call(
        kern,
        grid=(B, nq),
        in_specs=[
            pl.BlockSpec((1, IDX_HEADS, qb, IDX_HD), lambda b, j: (b * nq + j, 0, 0, 0)),
            pl.BlockSpec((qb, LANE), lambda b, j: (b * nq + j, C_SM // LANE)),
            pl.BlockSpec((1, ATT_HEADS, qb, ATT_HD), lambda b, j: (b * nq + j, 0, 0, 0)),
            pl.BlockSpec((1, S_pad, IDX_HD), lambda b, j: (b, 0, 0)),
            pl.BlockSpec((1, S_pad, ATT_KV), lambda b, j: (b, 0, 0)),
            pl.BlockSpec((1, S_pad, ATT_KV), lambda b, j: (b, 0, 0)),
        ],
        out_specs=pl.BlockSpec((qb, ATT_Q), lambda b, j: (b * nq + j, 0)),
        out_shape=jax.ShapeDtypeStruct((B * L, ATT_Q), BF16),
        scratch_shapes=[pltpu.VMEM((S_pad // TK_ATT, qb, TK_ATT), F32)],
        compiler_params=_cparams(("parallel", "arbitrary")),
        name="dsa",
    )(iq_hm, proj, q_hm, ik_all, k_all, v_all)


def _merge_kernel(oa_ref, ob_ref, wa_ref, wb_ref, ga_ref, gb_ref, o_ref):
    ya = _dot(oa_ref[...], wa_ref[...])
    yb = _dot(ob_ref[...], wb_ref[...])
    o_ref[...] = (_sigmoid(ga_ref[...]) * ya + _sigmoid(gb_ref[...]) * yb).astype(o_ref.dtype)


def _merge(oa, ob, wa, wb, proj):
    T = oa.shape[0]
    tm = min(512, T)
    tn = TN_MIX
    return pl.pallas_call(
        _merge_kernel,
        grid=(T // tm, D_MODEL // tn),
        in_specs=[
            pl.BlockSpec((tm, GDN_VAL), lambda i, j: (i, 0)),
            pl.BlockSpec((tm, ATT_Q), lambda i, j: (i, 0)),
            pl.BlockSpec((GDN_VAL, tn), lambda i, j: (0, j)),
            pl.BlockSpec((ATT_Q, tn), lambda i, j: (0, j)),
            pl.BlockSpec((tm, tn), lambda i, j: (i, C_GA // tn + j)),
            pl.BlockSpec((tm, tn), lambda i, j: (i, C_GB // tn + j)),
        ],
        out_specs=pl.BlockSpec((tm, tn), lambda i, j: (i, j)),
        out_shape=jax.ShapeDtypeStruct((T, D_MODEL), BF16),
        compiler_params=_cparams(("parallel", "arbitrary")),
        name="merge",
    )(oa, ob, wa, wb, proj, proj)


def _oproj_kernel(m_ref, w_ref, x_ref, g1_ref, sc2_ref, sh2_ref, lg_ref, lb_ref, x1_ref, h2_ref, acc_ref, *, nj):
    j = pl.program_id(1)
    acc_ref[j] = _dot(m_ref[...], w_ref[...])

    @pl.when(j == nj - 1)
    def _():
        bb, tl, D = x_ref.shape
        tn = acc_ref.shape[2]
        sl = lambda t: slice(t * tn, (t + 1) * tn)
        s1 = jnp.zeros((bb, tl, 1), F32)
        for t in range(nj):
            v = ALPHA * x_ref[:, :, sl(t)] + g1_ref[:, :, sl(t)] * acc_ref[t].reshape(bb, tl, tn)
            acc_ref[t] = v.reshape(bb * tl, tn)
            s1 = s1 + jnp.sum(v, -1, keepdims=True)
        mu = s1 * (1.0 / D)
        s2 = jnp.zeros((bb, tl, 1), F32)
        for t in range(nj):
            d = acc_ref[t].reshape(bb, tl, tn) - mu
            s2 = s2 + jnp.sum(d * d, -1, keepdims=True)
        rstd = lax.rsqrt(s2 * (1.0 / D) + LN_EPS)
        for t in range(nj):
            x1 = (acc_ref[t].reshape(bb, tl, tn) - mu) * rstd * lg_ref[:, :, sl(t)] + lb_ref[:, :, sl(t)]
            x1_ref[:, :, sl(t)] = x1
            h2_ref[:, sl(t)] = (x1 * (1.0 + sc2_ref[:, :, sl(t)]) + sh2_ref[:, :, sl(t)]).reshape(bb * tl, tn)


def _oproj_kernel_aliased(m_ref, w_ref, x_ref, g1_ref, sc2_ref, sh2_ref, lg_ref, lb_ref, h2_all_ref,
                          x1_ref, h2_ref, acc_ref, *, nj):
    del h2_all_ref
    _oproj_kernel(m_ref, w_ref, x_ref, g1_ref, sc2_ref, sh2_ref, lg_ref, lb_ref, x1_ref, h2_ref, acc_ref, nj=nj)


def _oproj(m, w_o, x3, mod3, row0, ln_g, ln_b, h2_all, t_all, t_off):
    B, L, D = x3.shape
    bb, tl = _row_tiling(B, L, TM_MIX)
    nl = L // tl
    rows = bb * tl
    mrow = row0 // bb
    tn = TN_MIX
    nj = D // tn
    assert t_off % rows == 0
    boff = t_off // rows
    modspec = lambda c: pl.BlockSpec((bb, 1, D), lambda i, j: (mrow + i // nl, 0, c))
    in_specs = [
        pl.BlockSpec((rows, D), lambda i, j: (i, 0)),
        pl.BlockSpec((D, tn), lambda i, j: (0, j)),
        pl.BlockSpec((bb, tl, D), lambda i, j: (i // nl, i % nl, 0)),
        modspec(2), modspec(4), modspec(3),
        pl.BlockSpec((1, 1, D), lambda i, j: (0, 0, 0)),
        pl.BlockSpec((1, 1, D), lambda i, j: (0, 0, 0)),
    ]
    args = [m, w_o, x3, mod3, mod3, mod3, ln_g.reshape(1, 1, D), ln_b.reshape(1, 1, D), h2_all]
    in_specs.append(pl.BlockSpec(memory_space=pl.ANY))
    aliases = {len(args) - 1: 1}
    kern = functools.partial(_oproj_kernel_aliased, nj=nj)
    return pl.pallas_call(
        kern,
        grid=((B // bb) * nl, nj),
        in_specs=in_specs,
        out_specs=[
            pl.BlockSpec((bb, tl, D), lambda i, j: (i // nl, i % nl, 0)),
            pl.BlockSpec((rows, D), lambda i, j: (boff + i, 0)),
        ],
        out_shape=[
            jax.ShapeDtypeStruct((B, L, D), F32),
            jax.ShapeDtypeStruct((t_all, D), F32),
        ],
        scratch_shapes=[pltpu.VMEM((nj, rows, tn), F32)],
        input_output_aliases=aliases,
        compiler_params=_cparams(("parallel", "arbitrary")),
        name="oproj_ln1",
    )(*args)


def _router_kernel(h_ref, w_ref, b_ref, id_ref, wt_ref):
    logits = jnp.dot(h_ref[...], w_ref[...], precision=lax.Precision.HIGHEST,
                     preferred_element_type=F32) + b_ref[...]
    rows = logits.shape[0]
    lane = lax.broadcasted_iota(jnp.int32, (rows, LANE), 1)
    is_g = lane < N_GROUPS
    gl = jnp.where(is_g, logits, -jnp.inf)
    gmax = jnp.max(gl, axis=1, keepdims=True)
    g_sel = jnp.min(jnp.where(gl == gmax, lane, LANE), axis=1, keepdims=True)
    p_grp = 1.0 / jnp.sum(jnp.where(is_g, jnp.exp(gl - gmax), 0.0), axis=1, keepdims=True)
    e_lo = N_GROUPS + g_sel * EXP_PER_GROUP
    in_g = (lane >= e_lo) & (lane < e_lo + EXP_PER_GROUP)
    el = jnp.where(in_g, logits, -jnp.inf)
    m1 = jnp.max(el, axis=1, keepdims=True)
    i1 = jnp.min(jnp.where(el == m1, lane, LANE), axis=1, keepdims=True)
    el2 = jnp.where(lane == i1, -jnp.inf, el)
    m2 = jnp.max(el2, axis=1, keepdims=True)
    i2 = jnp.min(jnp.where(el2 == m2, lane, LANE), axis=1, keepdims=True)
    e21 = jnp.exp(m2 - m1)
    w1 = p_grp / (1.0 + e21)
    w2 = p_grp * e21 / (1.0 + e21)
    id_ref[...] = jnp.where(lane == 0, i1 - N_GROUPS, jnp.where(lane == 1, i2 - N_GROUPS, 0))
    wt_ref[...] = jnp.where(lane == 0, w1, jnp.where(lane == 1, w2, 0.0))


def _router(h2, w_r, b_r):
    T, D = h2.shape
    tm = 512
    while T % tm:
        tm //= 2
    return pl.pallas_call(
        _router_kernel,
        grid=(T // tm,),
        in_specs=[
            pl.BlockSpec((tm, D), lambda i: (i, 0)),
            pl.BlockSpec((D, LANE), lambda i: (0, 0)),
            pl.BlockSpec((1, LANE), lambda i: (0, 0)),
        ],
        out_specs=[pl.BlockSpec((tm, LANE), lambda i: (i, 0)), pl.BlockSpec((tm, LANE), lambda i: (i, 0))],
        out_shape=[jax.ShapeDtypeStruct((T, LANE), jnp.int32), jax.ShapeDtypeStruct((T, LANE), F32)],
        compiler_params=_cparams(("parallel",)),
        name="router",
    )(h2, w_r, b_r)


def _expert_kernel(blk_e_ref, tok_ref, nused_ref, h_hbm, wg_ref, wu_ref, wd_ref, o_ref,
                   xbuf_ref, xb16_ref, act_ref, sem_ref, *, BM, nfc):
    i = pl.program_id(0)
    c = pl.program_id(1)
    nused = nused_ref[0]
    slot = lax.rem(i, 2)

    def start_gather(blk, sl):
        def body(r, carry):
            tok = tok_ref[blk * BM + r]
            pltpu.make_async_copy(h_hbm.at[pl.ds(tok, 1)], xbuf_ref.at[sl, pl.ds(r, 1)], sem_ref.at[sl]).start()
            return carry
        lax.fori_loop(0, BM, body, 0, unroll=8)

    def wait_gather(sl):
        pltpu.make_async_copy(xbuf_ref.at[sl], xbuf_ref.at[sl], sem_ref.at[sl]).wait()

    @pl.when(jnp.logical_and(c == 0, i < nused))
    def _():
        @pl.when(i == 0)
        def _():
            start_gather(0, 0)

        @pl.when(i + 1 < nused)
        def _():
            start_gather(i + 1, 1 - slot)

        wait_gather(slot)
        xb16_ref[...] = xbuf_ref[slot].astype(BF16)

    @pl.when(i < nused)
    def _():
        x = xb16_ref[...]
        gate = _dot(x, wg_ref[0])
        up = _dot(x, wu_ref[0])
        act_ref[c] = (_silu(gate) * up).astype(BF16)

        @pl.when(c == nfc - 1)
        def _():
            act = jnp.concatenate([act_ref[t] for t in range(nfc)], axis=-1)
            o_ref[...] = _dot(act, wd_ref[0])

    @pl.when(jnp.logical_and(i >= nused, c == nfc - 1))
    def _():
        o_ref[...] = jnp.zeros_like(o_ref)


def _experts(h2, blk_e, slot_tok, nused, w_up16, w_down16, BM):
    T, D = h2.shape
    P = slot_tok.shape[0]
    nblk = P // BM
    fc = FC_MOE
    nfc = D_EXPERT // fc
    kern = functools.partial(_expert_kernel, BM=BM, nfc=nfc)

    def eidx(i, nu):
        return jnp.minimum(i, nu[0] - 1)

    def cidx(i, c, nu):
        return jnp.where(i < nu[0], c, nfc - 1)

    grid_spec = pltpu.PrefetchScalarGridSpec(
        num_scalar_prefetch=3,
        grid=(nblk, nfc),
        in_specs=[
            pl.BlockSpec(memory_space=pl.ANY),
            pl.BlockSpec((1, D, fc), lambda i, c, be, tk, nu: (be[eidx(i, nu)], 0, cidx(i, c, nu))),
            pl.BlockSpec((1, D, fc), lambda i, c, be, tk, nu: (be[eidx(i, nu)], 0, nfc + cidx(i, c, nu))),
            pl.BlockSpec((1, D_EXPERT, D), lambda i, c, be, tk, nu: (be[eidx(i, nu)], 0, 0)),
        ],
        out_specs=pl.BlockSpec((BM, D), lambda i, c, be, tk, nu: (i, 0)),
        scratch_shapes=[
            pltpu.VMEM((2, BM, D), F32),
            pltpu.VMEM((BM, D), BF16),
            pltpu.VMEM((nfc, BM, fc), BF16),
            pltpu.SemaphoreType.DMA((2,)),
        ],
    )
    return pl.pallas_call(
        kern,
        grid_spec=grid_spec,
        out_shape=jax.ShapeDtypeStruct((P, D), F32),
        compiler_params=_cparams(("arbitrary", "arbitrary")),
        name="experts",
    )(blk_e, slot_tok, nused, h2, w_up16, w_up16, w_down16)


def _route_metadata(ids, T, BM):
    expert = ids[:, 0:2].reshape(-1)
    A = 2 * T
    onehot = (expert[:, None] == jnp.arange(N_EXPERTS, dtype=jnp.int32)[None, :]).astype(jnp.int32)
    csum = jnp.cumsum(onehot, axis=0)
    counts = csum[-1]
    rank = jnp.sum((csum - onehot) * onehot, axis=1)
    padded = ((counts + BM - 1) // BM) * BM
    pend = jnp.cumsum(padded)
    pstart = pend - padded
    dest = (pstart[expert] + rank).astype(jnp.int32)
    P = ((A + BM - 1) // BM) * BM + N_EXPERTS * BM
    nblk = P // BM
    tok = jnp.arange(A, dtype=jnp.int32) // 2
    slot_tok = jnp.zeros((P,), jnp.int32).at[dest].set(tok)
    blk_e = jnp.clip(jnp.searchsorted(pend, jnp.arange(nblk, dtype=jnp.int32) * BM, side="right"),
                     0, N_EXPERTS - 1).astype(jnp.int32)
    nused = (pend[-1] // BM).astype(jnp.int32).reshape(1)
    return dest, slot_tok, blk_e, nused


def _final_kernel(dest_ref, yb_hbm, x1_ref, g2_ref, wt_ref, lg_ref, lb_ref, o_ref, buf_ref, sem_ref, *, rows, ntiles):
    i = pl.program_id(0)
    slot = lax.rem(i, 2)

    def start_gather(tile, sl):
        def body(r, carry):
            d = dest_ref[tile * 2 * rows + r]
            pltpu.make_async_copy(yb_hbm.at[pl.ds(d, 1)], buf_ref.at[sl, pl.ds(r, 1)], sem_ref.at[sl]).start()
            return carry
        lax.fori_loop(0, 2 * rows, body, 0, unroll=8)

    def wait_gather(sl):
        pltpu.make_async_copy(buf_ref.at[sl], buf_ref.at[sl], sem_ref.at[sl]).wait()

    @pl.when(i == 0)
    def _():
        start_gather(0, 0)

    @pl.when(i + 1 < ntiles)
    def _():
        start_gather(i + 1, 1 - slot)

    wait_gather(slot)
    bb, tl, D = x1_ref.shape
    tn = TN_MIX
    sl = lambda t: slice(t * tn, (t + 1) * tn)
    w0 = wt_ref[:, 0:1]
    w1 = wt_ref[:, 1:2]
    s1 = jnp.zeros((bb, tl, 1), F32)
    for t in range(D // tn):
        f = w0 * buf_ref[slot, 0:rows, sl(t)] + w1 * buf_ref[slot, rows:2 * rows, sl(t)]
        v = ALPHA * x1_ref[:, :, sl(t)] + g2_ref[:, :, sl(t)] * f.reshape(bb, tl, tn)
        o_ref[:, :, sl(t)] = v
        s1 = s1 + jnp.sum(v, -1, keepdims=True)
    mu = s1 * (1.0 / D)
    s2 = jnp.zeros((bb, tl, 1), F32)
    for t in range(D // tn):
        d = o_ref[:, :, sl(t)] - mu
        s2 = s2 + jnp.sum(d * d, -1, keepdims=True)
    rstd = lax.rsqrt(s2 * (1.0 / D) + LN_EPS)
    for t in range(D // tn):
        o_ref[:, :, sl(t)] = (o_ref[:, :, sl(t)] - mu) * rstd * lg_ref[:, :, sl(t)] + lb_ref[:, :, sl(t)]


def _final(dest, yb, x1, mod3, row0, wts, t_off, ln_g, ln_b):
    B, L, D = x1.shape
    T = B * L
    bb, tl = _row_tiling(B, L, TM_MIX)
    nl = L // tl
    rows = bb * tl
    mrow = row0 // bb
    ntiles = (B // bb) * nl
    woff = t_off // rows
    dest_tiles = dest[2 * t_off:2 * (t_off + T)].reshape(T // rows, rows, 2).transpose(0, 2, 1).reshape(-1)
    kern = functools.partial(_final_kernel, rows=rows, ntiles=ntiles)
    grid_spec = pltpu.PrefetchScalarGridSpec(
        num_scalar_prefetch=1,
        grid=(ntiles,),
        in_specs=[
            pl.BlockSpec(memory_space=pl.ANY),
            pl.BlockSpec((bb, tl, D), lambda i, d: (i // nl, i % nl, 0)),
            pl.BlockSpec((bb, 1, D), lambda i, d: (mrow + i // nl, 0, 5)),
            pl.BlockSpec((rows, LANE), lambda i, d: (woff + i, 0)),
            pl.BlockSpec((1, 1, D), lambda i, d: (0, 0, 0)),
            pl.BlockSpec((1, 1, D), lambda i, d: (0, 0, 0)),
        ],
        out_specs=pl.BlockSpec((bb, tl, D), lambda i, d: (i // nl, i % nl, 0)),
        scratch_shapes=[
            pltpu.VMEM((2, 2 * rows, D), F32),
            pltpu.SemaphoreType.DMA((2,)),
        ],
    )
    return pl.pallas_call(
        kern,
        grid_spec=grid_spec,
        out_shape=jax.ShapeDtypeStruct((B, L, D), F32),
        compiler_params=_cparams(("arbitrary",)),
        name="combine_ln2",
    )(dest_tiles, yb, x1, mod3, wts, ln_g.reshape(1, 1, D), ln_b.reshape(1, 1, D))


def _pad_keys(t, S_pad):
    S = t.shape[1]
    if S == S_pad:
        return t
    return jnp.pad(t, ((0, 0), (0, S_pad - S), (0, 0)))


def _mixers(x3, mod3, row0, conv_state, gdn_state, past_k, past_v, past_ik, wp, h2_all, t_all, t_off):
    B, L, D = x3.shape
    pos0 = past_k.shape[1]
    S = pos0 + L
    proj = _inproj(x3, mod3, row0, wp["w_in"])

    oa, new_gdn = _gdn(proj, B, L, conv_state, gdn_state, wp["conv_w"], wp["a_log"], wp["dt_bias"],
                       wp["gdn_norm_w"])
    new_conv = proj.reshape(B, L, NP)[:, L - (CONV_W - 1):, C_QKV:C_QKV + GDN_CONV_CH]

    qb = min(Q_BLOCK, L)
    q_hm, kf, k16, vf, v16, iq_hm, ikf, ik16 = _rope(proj, B, L, pos0, qb)
    S_pad = ((S + TK_ATT - 1) // TK_ATT) * TK_ATT
    k_all = _pad_keys(jnp.concatenate([past_k.reshape(B, pos0, ATT_KV).astype(BF16), k16.reshape(B, L, ATT_KV)], 1), S_pad)
    v_all = _pad_keys(jnp.concatenate([past_v.reshape(B, pos0, ATT_KV).astype(BF16), v16.reshape(B, L, ATT_KV)], 1), S_pad)
    ik_all = _pad_keys(jnp.concatenate([past_ik.astype(BF16), ik16.reshape(B, L, IDX_HD)], 1), S_pad)
    ob = _dsa(iq_hm, proj, q_hm, ik_all, k_all, v_all, B, L, S, pos0)

    m = _merge(oa, ob, wp["w_br_a"], wp["w_br_b"], proj)
    x1, h2_all = _oproj(m, wp["w_o"], x3, mod3, row0, wp["ln1_g"], wp["ln1_b"], h2_all, t_all, t_off)

    kb_out = kf.reshape(B, L, ATT_KV_HEADS, ATT_HD)
    vb_out = vf.reshape(B, L, ATT_KV_HEADS, ATT_HD)
    ik_out = ikf.reshape(B, L, IDX_HD)
    return x1, h2_all, (new_conv, new_gdn, kb_out, vb_out, ik_out)


WP_TN = 512
WP_TR = 1024
WP_NSRC = WP_TN // LANE + 1


def _w_in_source_columns():
    sizes = (GDN_CONV_CH, GDN_HEADS, GDN_HEADS, GDN_VAL, ATT_Q, ATT_KV, ATT_KV, IDX_HEADS * IDX_HD, IDX_HD,
             IDX_HEADS, 2 * D_MODEL)
    offs, off = [], 0
    for s in sizes:
        offs.append(off)
        off += s
    o_qkv, o_a, o_b, o_z, o_q, o_k, o_v, o_iq, o_ik, o_iw, o_g = offs
    srcs = []
    for dst, src, width in ((C_QKV, o_qkv, GDN_CONV_CH), (C_Z, o_z, GDN_VAL), (C_IQ, o_iq, IDX_HEADS * IDX_HD),
                            (C_Q, o_q, ATT_Q), (C_K, o_k, ATT_KV), (C_V, o_v, ATT_KV), (C_IK, -1, WP_TN),
                            (C_GA, o_g, 2 * D_MODEL)):
        assert dst == len(srcs) * WP_TN and width % WP_TN == 0
        srcs += [src + t * WP_TN if src >= 0 else -1 for t in range(width // WP_TN)]
    assert len(srcs) * WP_TN == NP
    return srcs, (o_a, o_b, o_ik, o_iw)


def _wperm_kernel(base_ref, shift_ref, *refs):
    src_refs, sp_ref, o_ref = refs[:WP_NSRC], refs[WP_NSRC], refs[WP_NSRC + 1]
    shift = shift_ref[pl.program_id(0)]

    @pl.when(shift < 0)
    def _():
        o_ref[...] = sp_ref[...]

    for sh in sorted(set(s for s in _W_IN_SHIFTS if s >= 0)):
        @pl.when(shift == sh)
        def _(sh=sh):
            a = jnp.concatenate([r[...] for r in src_refs], axis=1)
            o_ref[...] = a[:, sh:sh + WP_TN].astype(BF16)


_W_IN_SRCS, _W_IN_NARROW = _w_in_source_columns()
_W_IN_SHIFTS = [s % LANE if s >= 0 else -1 for s in _W_IN_SRCS]


def _permute_w_in(w_in):
    D, ncols = w_in.shape
    o_a, o_b, o_ik, o_iw = _W_IN_NARROW
    pad = jnp.zeros((D, SM_W - 2 * GDN_HEADS - IDX_HEADS), w_in.dtype)
    special = jnp.concatenate([w_in[:, o_ik:o_ik + IDX_HD], w_in[:, o_a:o_a + GDN_HEADS],
                               w_in[:, o_b:o_b + GDN_HEADS], w_in[:, o_iw:o_iw + IDX_HEADS], pad], axis=1).astype(BF16)
    base = jnp.asarray([max(s, 0) // LANE for s in _W_IN_SRCS], jnp.int32)
    shift = jnp.asarray(_W_IN_SHIFTS, jnp.int32)
    last = (ncols - 1) // LANE

    def src_spec(t):
        return pl.BlockSpec((WP_TR, LANE), lambda j, r, base, shift: (r, jnp.minimum(base[j] + t, last)))

    grid_spec = pltpu.PrefetchScalarGridSpec(
        num_scalar_prefetch=2,
        grid=(NP // WP_TN, D // WP_TR),
        in_specs=[src_spec(t) for t in range(WP_NSRC)]
        + [pl.BlockSpec((WP_TR, WP_TN), lambda j, r, base, shift: (r, 0))],
        out_specs=pl.BlockSpec((WP_TR, WP_TN), lambda j, r, base, shift: (r, j)),
    )
    return pl.pallas_call(
        _wperm_kernel,
        grid_spec=grid_spec,
        out_shape=jax.ShapeDtypeStruct((D, NP), BF16),
        compiler_params=_cparams(("parallel", "parallel")),
        name="w_in_permute",
    )(base, shift, *([w_in] * WP_NSRC), special)


def kernel(x_prompt, x_sample, c_prompt, c_sample, state_conv, state_gdn, cache_k, cache_v, cache_idx_k,
           w_ada, b_ada, w_in, conv_w, a_log, dt_bias, gdn_norm_w, w_br_a, w_br_b, w_o,
           ln1_g, ln1_b, ln2_g, ln2_b, w_grp, b_grp, w_rtr, b_rtr, w_up, w_down):
    Bp, Lp, D = x_prompt.shape
    Bs, Ls, _ = x_sample.shape
    l = 0
    row0_p = Bs
    nrows = ((Bs + Bp + 7) // 8) * 8
    c_all = jnp.concatenate([c_sample, c_prompt, jnp.zeros((nrows - Bs - Bp, D), F32)], 0)
    mod = _ada(c_all, w_ada[l], b_ada[l])
    mod3 = mod.reshape(nrows, 1, 6 * D)

    nr = LANE - N_GROUPS - N_EXPERTS
    wp = dict(
        w_in=_permute_w_in(w_in[l]),
        conv_w=conv_w[l], a_log=a_log[l], dt_bias=dt_bias[l], gdn_norm_w=gdn_norm_w[l],
        w_br_a=w_br_a[l].astype(BF16), w_br_b=w_br_b[l].astype(BF16), w_o=w_o[l].astype(BF16),
        ln1_g=ln1_g[l], ln1_b=ln1_b[l], ln2_g=ln2_g[l], ln2_b=ln2_b[l],
        w_r=jnp.concatenate([w_grp[l], w_rtr[l], jnp.zeros((D, nr), F32)], 1),
        b_r=jnp.concatenate([b_grp[l], b_rtr[l], jnp.zeros((nr,), F32)]).reshape(1, LANE),
        w_up=w_up[l].astype(BF16), w_down=w_down[l].astype(BF16),
    )

    zc = jnp.zeros((Bp, CONV_W - 1, GDN_CONV_CH), F32)
    zs = jnp.zeros((Bp, GDN_HEADS, GDN_DK, GDN_DV), F32)
    zk = jnp.zeros((Bp, 0, ATT_KV_HEADS, ATT_HD), F32)
    zik = jnp.zeros((Bp, 0, IDX_HD), F32)
    Tp, Ts = Bp * Lp, Bs * Ls
    t_all = Tp + Ts
    h2_all = jnp.zeros((t_all, D), F32)
    x1p, h2_all, (c1, g1, k1, v1, i1) = _mixers(x_prompt, mod3, row0_p, zc, zs, zk, zk, zik, wp, h2_all, t_all, 0)
    x1s, h2_all, (c2, g2, k2, v2, i2) = _mixers(x_sample, mod3, 0, state_conv[l], state_gdn[l], cache_k[l],
                                                 cache_v[l], cache_idx_k[l], wp, h2_all, t_all, Tp)

    ids, wts = _router(h2_all, wp["w_r"], wp["b_r"])
    dest, slot_tok, blk_e, nused = _route_metadata(ids, t_all, MOE_BM)
    yb = _experts(h2_all, blk_e, slot_tok, nused, wp["w_up"], wp["w_down"], MOE_BM)
    yp = _final(dest, yb, x1p, mod3, row0_p, wts, 0, wp["ln2_g"], wp["ln2_b"])
    ys = _final(dest, yb, x1s, mod3, 0, wts, Tp, wp["ln2_g"], wp["ln2_b"])
    st = lambda t: t[None]
    return (yp, ys, st(c1), st(g1), st(k1), st(v1), st(i1), st(c2), st(g2), st(k2), st(v2), st(i2))
```

```python
import functools

import jax
import jax.numpy as jnp
from jax import lax
from jax.experimental import pallas as pl
from jax.experimental.pallas import tpu as pltpu

F32 = jnp.float32
BF16 = jnp.bfloat16

D_MODEL = 4096
CHUNK = 64
GDN_HEADS = D_MODEL // 256
GDN_DK = 128
GDN_DV = 128
GDN_KEY = GDN_HEADS * GDN_DK
GDN_VAL = GDN_HEADS * GDN_DV
GDN_CONV_CH = 2 * GDN_KEY + GDN_VAL
CONV_W = 4
ATT_HEADS = D_MODEL // 256
ATT_KV_HEADS = ATT_HEADS // 4
ATT_REP = ATT_HEADS // ATT_KV_HEADS
ATT_HD = 128
ATT_Q = ATT_HEADS * ATT_HD
ATT_KV = ATT_KV_HEADS * ATT_HD
IDX_HEADS = D_MODEL // 128
IDX_HD = 128
TOPK_KEYS = 256
Q_BLOCK = 128
ROPE_THETA = 500000.0
ROPE_ROT = ATT_HD // 4
ROPE_HALF = ROPE_ROT // 2
N_GROUPS = 4
EXP_PER_GROUP = 8
N_EXPERTS = N_GROUPS * EXP_PER_GROUP
D_EXPERT = D_MODEL // 4
DEPTH = 1
ALPHA = (2.0 * DEPTH) ** 0.25
LN_EPS = 1e-5
RMS_EPS = 1e-6

C_QKV = 0
C_Z = C_QKV + GDN_CONV_CH
C_IQ = C_Z + GDN_VAL
C_Q = C_IQ + IDX_HEADS * IDX_HD
C_K = C_Q + ATT_Q
C_V = C_K + ATT_KV
C_IK = C_V + ATT_KV
C_SM = C_IK + IDX_HD
SM_W = 384
C_GA = C_SM + SM_W
C_GB = C_GA + D_MODEL
NP = C_GB + D_MODEL
SM_A, SM_B, SM_IW = 0, GDN_HEADS, 2 * GDN_HEADS

LANE = 128
NEG_BIG = -1e30
VMEM_LIMIT = 56 * 1024 * 1024

TN_IN = 512
TM_IN = 1024
TM_MIX = 256
TN_MIX = 512
TK_ATT = 512
FC_MOE = 256
MOE_BM = 256


def _cparams(sem):
    return pltpu.CompilerParams(dimension_semantics=sem, vmem_limit_bytes=VMEM_LIMIT)


def _dot(a, b):
    return jnp.dot(a, b, preferred_element_type=F32)


def _dot_nt(a, b):
    return lax.dot_general(a, b, (((1,), (1,)), ((), ())), preferred_element_type=F32)


def _dot_tn(a, b):
    return lax.dot_general(a, b, (((0,), (0,)), ((), ())), preferred_element_type=F32)


def _split_bf16(a):
    hi = a.astype(BF16)
    lo = (a - hi.astype(F32)).astype(BF16)
    return hi, lo


def _mm3(a, b, dot=_dot):
    ah, al = _split_bf16(a)
    bh, bl = _split_bf16(b)
    return dot(ah, bh) + (dot(ah, bl) + dot(al, bh))


def _mm1(a, b):
    return _dot(a.astype(BF16), b.astype(BF16))


def _mm1_nt(a, b):
    return _dot_nt(a.astype(BF16), b.astype(BF16))


def _sigmoid(x):
    return 1.0 / (1.0 + jnp.exp(-x))


def _silu(x):
    return x * _sigmoid(x)


def _softplus(x):
    return jnp.maximum(x, 0.0) + jnp.log(1.0 + jnp.exp(-jnp.abs(x)))


def _ada_kernel(c_ref, w_ref, b_ref, o_ref):
    s = _silu(c_ref[...])
    o_ref[...] = _dot(s.astype(BF16), w_ref[...].astype(BF16)) + b_ref[...]


def _ada(c_all, w_ada, b_ada):
    R, D = c_all.shape
    N = w_ada.shape[1]
    tn = 512
    return pl.pallas_call(
        _ada_kernel,
        grid=(N // tn,),
        in_specs=[
            pl.BlockSpec((R, D), lambda j: (0, 0)),
            pl.BlockSpec((D, tn), lambda j: (0, j)),
            pl.BlockSpec((1, tn), lambda j: (0, j)),
        ],
        out_specs=pl.BlockSpec((R, tn), lambda j: (0, j)),
        out_shape=jax.ShapeDtypeStruct((R, N), F32),
        compiler_params=_cparams(("parallel",)),
        name="ada",
    )(c_all, w_ada, b_ada.reshape(1, N))


def _inproj_kernel(x_ref, sc_ref, sh_ref, w_ref, o_ref, h_ref):
    @pl.when(pl.program_id(1) == 0)
    def _():
        h = x_ref[...] * (1.0 + sc_ref[...]) + sh_ref[...]
        h_ref[...] = h.reshape(h_ref.shape).astype(BF16)

    o_ref[...] = _dot(h_ref[...], w_ref[...])


def _row_tiling(B, L, tm):
    if L >= tm:
        return 1, tm
    bb = max(1, min(B, tm // L))
    while B % bb:
        bb -= 1
    return bb, L


def _inproj(x3, mod3, row0, w_perm):
    B, L, D = x3.shape
    bb, tl = _row_tiling(B, L, TM_IN)
    nl = L // tl
    rows = bb * tl
    mrow = row0 // bb
    return pl.pallas_call(
        _inproj_kernel,
        grid=((B // bb) * nl, NP // TN_IN),
        in_specs=[
            pl.BlockSpec((bb, tl, D), lambda i, j: (i // nl, i % nl, 0), pipeline_mode=pl.Buffered(1)),
            pl.BlockSpec((bb, 1, D), lambda i, j: (mrow + i // nl, 0, 1)),
            pl.BlockSpec((bb, 1, D), lambda i, j: (mrow + i // nl, 0, 0)),
            pl.BlockSpec((D, TN_IN), lambda i, j: (0, j)),
        ],
        out_specs=pl.BlockSpec((rows, TN_IN), lambda i, j: (i, j)),
        out_shape=jax.ShapeDtypeStruct((B * L, NP), F32),
        scratch_shapes=[pltpu.VMEM((rows, D), BF16)],
        compiler_params=_cparams(("parallel", "arbitrary")),
        name="inproj",
    )(x3, mod3, mod3, w_perm)


GDN_GROUP = 4


def _split3_bf16(a):
    hi = a.astype(BF16)
    r = a - hi.astype(F32)
    mid = r.astype(BF16)
    lo = (r - mid.astype(F32)).astype(BF16)
    return hi, mid, lo


def _dot_exact01(a, b01, a_is_01=False):
    if a_is_01:
        h, m, l = _split3_bf16(b01)
        return _dot(a, h) + (_dot(a, m) + _dot(a, l))
    h, m, l = _split3_bf16(a)
    return _dot(h, b01) + (_dot(m, b01) + _dot(l, b01))


def _mm3p(a_hl, b_hl, dot=_dot):
    (ah, al), (bh, bl) = a_hl, b_hl
    return dot(ah, bh) + (dot(ah, bl) + dot(al, bh))


def _block_rows_hl(x_hl, nblk, mask01):
    return tuple(jnp.concatenate([p] * nblk, axis=0) * mask01 for p in x_hl)


def _gdn_kernel(qkv_ref, z_ref, sm_ref, cst_ref, cw_ref, alr_ref, dtr_ref, nw_ref, s0_ref,
                o_ref, sfin_ref, s_ref, ext_ref, mt_ref, mk_ref, mx_ref, ml_ref, *, C, nchunks):
    n = pl.program_id(1)
    H = GDN_HEADS
    G = GDN_GROUP
    NG = H // G
    GC = G * C
    PADR = 8
    logc = C.bit_length() - 1

    @pl.when(n == 0)
    def _():
        s_ref[...] = s0_ref[0]
        ext_ref[PADR - (CONV_W - 1):PADR, :] = cst_ref[0]

    ext_ref[PADR:PADR + C, :] = qkv_ref[...]
    cw = cw_ref[...]
    y = ext_ref[PADR - 3:PADR - 3 + C, :] * cw[0:1]
    for jw in range(1, CONV_W):
        y = y + ext_ref[PADR - 3 + jw:PADR - 3 + jw + C, :] * cw[jw:jw + 1]
    y = _silu(y)
    hist = ext_ref[PADR + C - (CONV_W - 1):PADR + C, :]
    ext_ref[PADR - (CONV_W - 1):PADR, :] = hist

    sm = sm_ref[...]
    g_col = -jnp.exp(alr_ref[...]) * _softplus(sm[:, SM_A:SM_A + H] + dtr_ref[...])
    beta_col = _sigmoid(sm[:, SM_B:SM_B + H])

    ii = lax.broadcasted_iota(jnp.int32, (C, C), 0)
    jj = lax.broadcasted_iota(jnp.int32, (C, C), 1)
    tril01 = jnp.where(jj <= ii, 1.0, 0.0).astype(BF16)
    gc_col = _dot_exact01(tril01, g_col, a_is_01=True)
    egc_col = jnp.exp(gc_col)
    gc_last = gc_col[C - 1:C, :]
    ekd_col = jnp.exp(gc_last - gc_col)
    egl = jnp.exp(gc_last)

    W = H * C
    hrow = lax.broadcasted_iota(jnp.int32, (H, W), 0)
    hlane = jnp.right_shift(lax.broadcasted_iota(jnp.int32, (H, W), 1), logc)
    e_seg = jnp.where(hrow == hlane, 1.0, 0.0).astype(BF16)
    gseg = _dot_exact01(gc_col, e_seg)
    ri = lax.broadcasted_iota(jnp.int32, (C, W), 0)
    cj = jnp.bitwise_and(lax.broadcasted_iota(jnp.int32, (C, W), 1), C - 1)
    grow = jnp.sum(jnp.where(ri == cj, gseg, 0.0), axis=0, keepdims=True)
    decay_all = jnp.where(cj <= ri, jnp.exp(gseg - grow), 0.0)

    gi = lax.broadcasted_iota(jnp.int32, (C, GC), 0)
    gj = jnp.bitwise_and(lax.broadcasted_iota(jnp.int32, (C, GC), 1), C - 1)
    strict = gj < gi
    eye = jnp.where(gi == gj, 1.0, 0.0)
    nlev = logc

    def lev_mask(lev):
        return ((jnp.right_shift(gi, lev + 1) == jnp.right_shift(gj, lev + 1))
                & (jnp.bitwise_and(jnp.right_shift(gi, lev), 1) == 1)
                & (jnp.bitwise_and(jnp.right_shift(gj, lev), 1) == 0))

    @pl.when(n == 0)
    def _():
        def own(ncols, col_head):
            r = jnp.right_shift(lax.broadcasted_iota(jnp.int32, (G * C, ncols), 0), logc)
            c = col_head(lax.broadcasted_iota(jnp.int32, (G * C, ncols), 1))
            return jnp.where(r == c, 1.0, 0.0).astype(BF16)

        mt_ref[...] = own(GC, lambda c: jnp.right_shift(c, logc))
        mk_ref[...] = own(G * GDN_DK, lambda c: jnp.right_shift(c, 7))
        mx_ref[...] = own(2 * G * GDN_DK, lambda c: jnp.bitwise_and(jnp.right_shift(c, 7), G - 1))
        for lev in range(1, nlev):
            ml_ref[lev] = jnp.where(lev_mask(lev), 1.0, 0.0).astype(BF16)

    nw = nw_ref[...]
    z = z_ref[...]

    lmats, lm_hls, intras, rhss, qds, kds = [], [], [], [], [], []
    for g in range(NG):
        qs, ks, kbs, vbs, kes = [], [], [], [], []
        for t in range(G):
            h = g * G + t
            qh = y[:, h * GDN_DK:(h + 1) * GDN_DK]
            kh = y[:, GDN_KEY + h * GDN_DK:GDN_KEY + (h + 1) * GDN_DK]
            vh = y[:, 2 * GDN_KEY + h * GDN_DV:2 * GDN_KEY + (h + 1) * GDN_DV]
            q = qh * lax.rsqrt(jnp.sum(qh * qh, -1, keepdims=True) + RMS_EPS) * (GDN_DK ** -0.5)
            k = kh * lax.rsqrt(jnp.sum(kh * kh, -1, keepdims=True) + RMS_EPS)
            beta = beta_col[:, h:h + 1]
            egc = egc_col[:, h:h + 1]
            kb = k * beta
            qs.append(q)
            ks.append(k)
            kbs.append(kb)
            vbs.append(vh * beta)
            kes.append(kb * egc)
            qds.append(q * egc)
            kds.append(k * ekd_col[:, h:h + 1])
        bdk_hl = _block_rows_hl(_split_bf16(jnp.concatenate(ks, axis=-1)), G, mk_ref[...])
        dec = decay_all[:, g * GC:(g + 1) * GC]
        kk = _mm3p(_split_bf16(jnp.concatenate(kbs, axis=-1)), bdk_hl, _dot_nt)
        qk = _dot_nt(jnp.concatenate(qs, axis=-1).astype(BF16), bdk_hl[0])
        lmat = jnp.where(strict, kk * dec, 0.0)
        lmats.append(lmat)
        lm_hls.append(_split_bf16(lmat))
        intras.append((qk * dec).astype(BF16))
        rhss.append(_split_bf16(jnp.concatenate(vbs + kes, axis=-1)))

    tinvs = [eye - jnp.where(lev_mask(0), lmats[g], 0.0) for g in range(NG)]
    for lev in range(1, nlev):
        ml = ml_ref[lev]
        for g in range(NG):
            t_hl = _split_bf16(tinvs[g])
            b_hl = (lm_hls[g][0] * ml, lm_hls[g][1] * ml)
            p = _mm3p(b_hl, _block_rows_hl(t_hl, G, mt_ref[...]))
            tinvs[g] = tinvs[g] - _mm3p(t_hl, _block_rows_hl(_split_bf16(p), G, mt_ref[...]))

    for g in range(NG):
        x = _mm3p(_split_bf16(tinvs[g]), _block_rows_hl(rhss[g], G, mx_ref[...]))
        u_cat = x[:, :G * GDN_DV]
        w_cat = x[:, G * GDN_DV:]
        for t in range(G):
            h = g * G + t
            s = s_ref[h]
            s_hl = _split_bf16(s)
            lanes = slice(t * GDN_DV, (t + 1) * GDN_DV)
            v_new = u_cat[:, lanes] - _mm3p(_split_bf16(w_cat[:, lanes]), s_hl)
            v_hl = _split_bf16(v_new)
            o = _dot(qds[h].astype(BF16), s_hl[0]) + _dot(intras[g][:, t * C:(t + 1) * C], v_hl[0])
            s_ref[h] = s * egl[:, h:h + 1] + _mm3p(_split_bf16(kds[h]), v_hl, _dot_tn)
            zh = z[:, h * GDN_DV:(h + 1) * GDN_DV]
            on = o * lax.rsqrt(jnp.mean(o * o, -1, keepdims=True) + RMS_EPS) * nw * _silu(zh)
            o_ref[:, h * GDN_DV:(h + 1) * GDN_DV] = on.astype(o_ref.dtype)

    @pl.when(n == nchunks - 1)
    def _():
        sfin_ref[0] = s_ref[...]


def _gdn(proj, B, L, conv_state, gdn_state, conv_w, a_log, dt_bias, norm_w):
    C = min(CHUNK, L)
    N = L // C
    H = GDN_HEADS
    kern = functools.partial(_gdn_kernel, C=C, nchunks=N)
    return pl.pallas_call(
        kern,
        grid=(B, N),
        in_specs=[
            pl.BlockSpec((C, GDN_CONV_CH), lambda b, n: (b * N + n, C_QKV // GDN_CONV_CH)),
            pl.BlockSpec((C, GDN_VAL), lambda b, n: (b * N + n, C_Z // GDN_VAL)),
            pl.BlockSpec((C, LANE), lambda b, n: (b * N + n, C_SM // LANE)),
            pl.BlockSpec((1, CONV_W - 1, GDN_CONV_CH), lambda b, n: (b, 0, 0)),
            pl.BlockSpec((CONV_W, GDN_CONV_CH), lambda b, n: (0, 0)),
            pl.BlockSpec((1, H), lambda b, n: (0, 0)),
            pl.BlockSpec((1, H), lambda b, n: (0, 0)),
            pl.BlockSpec((1, GDN_DV), lambda b, n: (0, 0)),
            pl.BlockSpec((1, H, GDN_DK, GDN_DV), lambda b, n: (b, 0, 0, 0)),
        ],
        out_specs=[
            pl.BlockSpec((C, GDN_VAL), lambda b, n: (b * N + n, 0)),
            pl.BlockSpec((1, H, GDN_DK, GDN_DV), lambda b, n: (b, 0, 0, 0)),
        ],
        out_shape=[
            jax.ShapeDtypeStruct((B * L, GDN_VAL), BF16),
            jax.ShapeDtypeStruct((B, H, GDN_DK, GDN_DV), F32),
        ],
        scratch_shapes=[
            pltpu.VMEM((H, GDN_DK, GDN_DV), F32),
            pltpu.VMEM((8 + C, GDN_CONV_CH), F32),
            pltpu.VMEM((GDN_GROUP * C, GDN_GROUP * C), BF16),
            pltpu.VMEM((GDN_GROUP * C, GDN_GROUP * GDN_DK), BF16),
            pltpu.VMEM((GDN_GROUP * C, 2 * GDN_GROUP * GDN_DK), BF16),
            pltpu.VMEM((C.bit_length() - 1, C, GDN_GROUP * C), BF16),
        ],
        compiler_params=_cparams(("parallel", "arbitrary")),
        name="gdn",
    )(proj, proj, proj, conv_state, conv_w, a_log.reshape(1, H), dt_bias.reshape(1, H),
      norm_w.reshape(1, GDN_DV), gdn_state)


def _rope_kernel(q_ref, k_ref, v_ref, iq_ref, ik_ref, cos_ref, sa_ref, sb_ref,
                 qo_ref, kf_ref, kb_ref, vf_ref, vb_ref, iqo_ref, ikf_ref, ikb_ref):
    cosf = cos_ref[...]
    sa = sa_ref[...]
    sb = sb_ref[...]

    def rope(x):
        return (x * cosf + pltpu.roll(x, LANE - ROPE_HALF, 1) * sa + pltpu.roll(x, ROPE_HALF, 1) * sb)

    for h in range(ATT_HEADS):
        qo_ref[0, h] = rope(q_ref[:, h * ATT_HD:(h + 1) * ATT_HD]).astype(BF16)
    for h in range(ATT_KV_HEADS):
        kr = rope(k_ref[:, h * ATT_HD:(h + 1) * ATT_HD])
        kf_ref[:, h * ATT_HD:(h + 1) * ATT_HD] = kr
        kb_ref[:, h * ATT_HD:(h + 1) * ATT_HD] = kr.astype(BF16)
    v = v_ref[...]
    vf_ref[...] = v
    vb_ref[...] = v.astype(BF16)
    for h in range(IDX_HEADS):
        iqo_ref[0, h] = rope(iq_ref[:, h * IDX_HD:(h + 1) * IDX_HD]).astype(BF16)
    ikr = rope(ik_ref[...])
    ikf_ref[...] = ikr
    ikb_ref[...] = ikr.astype(BF16)


def _rope_tables(L, pos0, reps):
    inv = jnp.power(ROPE_THETA, -jnp.arange(ROPE_HALF, dtype=F32) * (2.0 / ROPE_ROT))
    ang = (pos0 + jnp.arange(L)).astype(F32)[:, None] * inv[None, :]
    cos, sin = jnp.cos(ang), jnp.sin(ang)
    z16 = jnp.zeros((L, ROPE_HALF), F32)
    rest0 = jnp.zeros((L, ATT_HD - ROPE_ROT), F32)
    cosf = jnp.concatenate([cos, cos, jnp.ones((L, ATT_HD - ROPE_ROT), F32)], -1)
    sa = jnp.concatenate([-sin, z16, rest0], -1)
    sb = jnp.concatenate([z16, sin, rest0], -1)
    if reps > 1:
        cosf, sa, sb = (jnp.tile(t, (reps, 1)) for t in (cosf, sa, sb))
    return cosf, sa, sb


def _rope(proj, B, L, pos0, qb):
    T = B * L
    reps = 1
    tab_rows = L
    cosf, sa, sb = _rope_tables(L, pos0, reps)
    nq = L // qb
    tspec = pl.BlockSpec((qb, LANE), lambda i: (i % nq, 0))
    outs = pl.pallas_call(
        _rope_kernel,
        grid=(T // qb,),
        in_specs=[
            pl.BlockSpec((qb, ATT_Q), lambda i: (i, C_Q // ATT_Q)),
            pl.BlockSpec((qb, ATT_KV), lambda i: (i, C_K // ATT_KV)),
            pl.BlockSpec((qb, ATT_KV), lambda i: (i, C_V // ATT_KV)),
            pl.BlockSpec((qb, IDX_HEADS * IDX_HD), lambda i: (i, C_IQ // (IDX_HEADS * IDX_HD))),
            pl.BlockSpec((qb, IDX_HD), lambda i: (i, C_IK // IDX_HD)),
            tspec, tspec, tspec,
        ],
        out_specs=[
            pl.BlockSpec((1, ATT_HEADS, qb, ATT_HD), lambda i: (i, 0, 0, 0)),
            pl.BlockSpec((qb, ATT_KV), lambda i: (i, 0)),
            pl.BlockSpec((qb, ATT_KV), lambda i: (i, 0)),
            pl.BlockSpec((qb, ATT_KV), lambda i: (i, 0)),
            pl.BlockSpec((qb, ATT_KV), lambda i: (i, 0)),
            pl.BlockSpec((1, IDX_HEADS, qb, IDX_HD), lambda i: (i, 0, 0, 0)),
            pl.BlockSpec((qb, IDX_HD), lambda i: (i, 0)),
            pl.BlockSpec((qb, IDX_HD), lambda i: (i, 0)),
        ],
        out_shape=[
            jax.ShapeDtypeStruct((T // qb, ATT_HEADS, qb, ATT_HD), BF16),
            jax.ShapeDtypeStruct((T, ATT_KV), F32),
            jax.ShapeDtypeStruct((T, ATT_KV), BF16),
            jax.ShapeDtypeStruct((T, ATT_KV), F32),
            jax.ShapeDtypeStruct((T, ATT_KV), BF16),
            jax.ShapeDtypeStruct((T // qb, IDX_HEADS, qb, IDX_HD), BF16),
            jax.ShapeDtypeStruct((T, IDX_HD), F32),
            jax.ShapeDtypeStruct((T, IDX_HD), BF16),
        ],
        compiler_params=_cparams(("parallel",)),
        name="rope",
    )(proj, proj, proj, proj, proj, cosf, sa, sb)
    return outs


def _dsa_kernel(iq_ref, sm_ref, q_ref, ik_ref, k_ref, v_ref, o_ref, sc_ref, *, qb, S, pos0, n_sel):
    j = pl.program_id(1)
    TK = TK_ATT
    q_last = pos0 + (j + 1) * qb - 1
    lim = jnp.minimum(((q_last // CHUNK) + 1) * CHUNK, S)
    nkt = (lim + TK - 1) // TK
    qpos = pos0 + j * qb + lax.broadcasted_iota(jnp.int32, (qb, 1), 0)
    qlim = jnp.minimum((jnp.right_shift(qpos, CHUNK.bit_length() - 1) + 1) * CHUNK, S)
    iw = sm_ref[:, SM_IW:SM_IW + IDX_HEADS] * ((IDX_HEADS ** -0.5) * (IDX_HD ** -0.5))
    HG = 4

    def score_tile(kt, carry):
        ks = pl.multiple_of(kt * TK, TK)
        ik_t = ik_ref[0, pl.ds(ks, TK), :]
        acc = jnp.zeros((qb, TK), F32)
        for hg in range(IDX_HEADS // HG):
            iq_g = iq_ref[0, hg * HG:(hg + 1) * HG].reshape(HG * qb, IDX_HD)
            r = jnp.maximum(_dot_nt(iq_g, ik_t), 0.0)
            for t in range(HG):
                hh = hg * HG + t
                acc = acc + iw[:, hh:hh + 1] * r[t * qb:(t + 1) * qb]
        kpos = ks + lax.broadcasted_iota(jnp.int32, (qb, TK), 1)
        sc_ref[kt] = jnp.where(kpos < qlim, acc, -jnp.inf)
        return carry

    lax.fori_loop(0, nkt, score_tile, 0)

    def lane_fold(m):
        p = m[:, 0:LANE]
        for t in range(1, TK // LANE):
            p = p + m[:, t * LANE:(t + 1) * LANE]
        return p

    def count_ge(x):
        def body(kt, c):
            return c + lane_fold(jnp.where(sc_ref[kt] >= x, 1.0, 0.0))
        part = lax.fori_loop(0, nkt, body, jnp.zeros((qb, LANE), F32))
        return jnp.sum(part, axis=1, keepdims=True)

    def minmax(kt, c):
        lo, hi = c
        t = sc_ref[kt]
        lo = jnp.minimum(lo, jnp.min(jnp.where(t > -jnp.inf, t, jnp.inf), axis=1, keepdims=True))
        hi = jnp.maximum(hi, jnp.max(t, axis=1, keepdims=True))
        return lo, hi

    lo0, hi0 = lax.fori_loop(0, nkt, minmax,
                             (jnp.full((qb, 1), jnp.inf, F32), jnp.full((qb, 1), -jnp.inf, F32)))
    kf = float(n_sel)
    cnt_all = count_ge(lo0)
    cnt_hi = count_ge(hi0)
    top_tied = cnt_hi >= kf
    fixed = (cnt_all <= kf) | top_tied
    lo_init = jnp.where(top_tied, hi0, lo0)

    def probe(lo, hi, clo, done):
        mid = lo + (hi - lo) * 0.5
        stop = (mid <= lo) | (mid >= hi) | (done > 0.0)
        c = count_ge(mid)
        up = jnp.logical_and(jnp.logical_not(stop), c >= kf)
        dn = jnp.logical_and(jnp.logical_not(stop), c < kf)
        done2 = jnp.where(stop | (c == kf), 1.0, 0.0)
        return jnp.where(up, mid, lo), jnp.where(dn, mid, hi), jnp.where(up, c, clo), done2

    def cond(st):
        return jnp.logical_and(st[4] > 0, st[5] < 128)

    def body(st):
        lo, hi, clo, done, _, it = st
        lo, hi, clo, done = probe(lo, hi, clo, done)
        lo, hi, clo, done = probe(lo, hi, clo, done)
        nact = jnp.sum(jnp.where(done > 0.0, 0, 1))
        return lo, hi, clo, done, nact, it + 1

    done0 = jnp.where(fixed, 1.0, 0.0)
    clo0 = jnp.where(top_tied, cnt_hi, cnt_all)
    thr, _, cthr, _, _, _ = lax.while_loop(cond, body, (lo_init, hi0, clo0, done0, jnp.int32(1), jnp.int32(0)))

    s_end = sc_ref.shape[0] * TK

    def count_tied(op, x):
        def body(kt, c):
            t = sc_ref[kt]
            if op == "gt":
                hit = t > thr
            else:
                kpos = kt * TK + lax.broadcasted_iota(jnp.int32, (qb, TK), 1)
                hit = jnp.logical_and(t == thr, kpos < x)
            return c + lane_fold(jnp.where(hit, 1.0, 0.0))
        part = lax.fori_loop(0, nkt, body, jnp.zeros((qb, LANE), F32))
        return jnp.sum(part, axis=1, keepdims=True)

    def tie_bound(_):
        need = kf - count_tied("gt", None)

        def step(_, st):
            lo_i, hi_i = st
            mid = jnp.right_shift(lo_i + hi_i, 1)
            ge = count_tied("eq", mid) >= need
            return jnp.where(ge, lo_i, mid + 1), jnp.where(ge, mid, hi_i)

        _, hi_i = lax.fori_loop(0, s_end.bit_length(), step,
                                (jnp.zeros((qb, 1), jnp.int32), jnp.full((qb, 1), s_end, jnp.int32)))
        return jnp.where(cthr > kf, hi_i, s_end)

    n_over = jnp.sum(jnp.where(cthr > kf, 1, 0))
    ibound = lax.cond(n_over > 0, tie_bound, lambda _: jnp.full((qb, 1), s_end, jnp.int32), 0)

    def to_bias(kt, carry):
        t = sc_ref[kt]
        kpos = kt * TK + lax.broadcasted_iota(jnp.int32, (qb, TK), 1)
        keep = (t > thr) | ((t == thr) & (kpos < ibound))
        sc_ref[kt] = jnp.where(keep, 0.0, NEG_BIG)
        return carry

    lax.fori_loop(0, nkt, to_bias, 0)

    R = ATT_REP
    for g in range(ATT_KV_HEADS):
        qg = q_ref[0, g * R:(g + 1) * R].reshape(R * qb, ATT_HD)

        def att_tile(kt, carry, g=g, qg=qg):
            m, l, acc = carry
            ks = pl.multiple_of(kt * TK, TK)
            k_t = k_ref[0, pl.ds(ks, TK), g * ATT_HD:(g + 1) * ATT_HD]
            v_t = v_ref[0, pl.ds(ks, TK), g * ATT_HD:(g + 1) * ATT_HD]
            s = _dot_nt(qg, k_t) * (ATT_HD ** -0.5)
            s = (s.reshape(R, qb, TK) + sc_ref[kt][None]).reshape(R * qb, TK)
            m_new = jnp.maximum(m, jnp.max(s, axis=1, keepdims=True))
            p = jnp.exp(s - m_new)
            a = jnp.exp(m - m_new)
            l = a * l + jnp.sum(p, axis=1, keepdims=True)
            acc = a * acc + _dot(p.astype(BF16), v_t)
            return m_new, l, acc

        def att_pair(i, carry, att_tile=att_tile):
            return att_tile(2 * i + 1, att_tile(2 * i, carry))

        m0 = jnp.full((R * qb, 1), NEG_BIG, F32)
        l0 = jnp.zeros((R * qb, 1), F32)
        a0 = jnp.zeros((R * qb, ATT_HD), F32)
        carry = lax.fori_loop(0, nkt // 2, att_pair, (m0, l0, a0))
        _, l, acc = lax.cond(nkt % 2 == 1, lambda c, att_tile=att_tile: att_tile(nkt - 1, c), lambda c: c, carry)
        out = acc / l
        for r in range(R):
            hh = g * R + r
            o_ref[:, hh * ATT_HD:(hh + 1) * ATT_HD] = out[r * qb:(r + 1) * qb].astype(o_ref.dtype)


def _dsa(iq_hm, proj, q_hm, ik_all, k_all, v_all, B, L, S, pos0):
    qb = min(Q_BLOCK, L)
    nq = L // qb
    S_pad = ik_all.shape[1]
    n_sel = min(TOPK_KEYS, S // 4)
    kern = functools.partial(_dsa_kernel, qb=qb, S=S, pos0=pos0, n_sel=n_sel)
    return pl.pallas_call(
        kern,
        grid=(B, nq),
        in_specs=[
            pl.BlockSpec((1, IDX_HEADS, qb, IDX_HD), lambda b, j: (b * nq + j, 0, 0, 0)),
            pl.BlockSpec((qb, LANE), lambda b, j: (b * nq + j, C_SM // LANE)),
            pl.BlockSpec((1, ATT_HEADS, qb, ATT_HD), lambda b, j: (b * nq + j, 0, 0, 0)),
            pl.BlockSpec((1, S_pad, IDX_HD), lambda b, j: (b, 0, 0)),
            pl.BlockSpec((1, S_pad, ATT_KV), lambda b, j: (b, 0, 0)),
            pl.BlockSpec((1, S_pad, ATT_KV), lambda b, j: (b, 0, 0)),
        ],
        out_specs=pl.BlockSpec((qb, ATT_Q), lambda b, j: (b * nq + j, 0)),
        out_shape=jax.ShapeDtypeStruct((B * L, ATT_Q), BF16),
        scratch_shapes=[pltpu.VMEM((S_pad // TK_ATT, qb, TK_ATT), F32)],
        compiler_params=_cparams(("parallel", "arbitrary")),
        name="dsa",
    )(iq_hm, proj, q_hm, ik_all, k_all, v_all)


def _merge_kernel(oa_ref, ob_ref, wa_ref, wb_ref, ga_ref, gb_ref, o_ref):
    ya = _dot(oa_ref[...], wa_ref[...])
    yb = _dot(ob_ref[...], wb_ref[...])
    o_ref[...] = (_sigmoid(ga_ref[...]) * ya + _sigmoid(gb_ref[...]) * yb).astype(o_ref.dtype)


def _merge(oa, ob, wa, wb, proj):
    T = oa.shape[0]
    tm = min(512, T)
    tn = TN_MIX
    return pl.pallas_call(
        _merge_kernel,
        grid=(T // tm, D_MODEL // tn),
        in_specs=[
            pl.BlockSpec((tm, GDN_VAL), lambda i, j: (i, 0)),
            pl.BlockSpec((tm, ATT_Q), lambda i, j: (i, 0)),
            pl.BlockSpec((GDN_VAL, tn), lambda i, j: (0, j)),
            pl.BlockSpec((ATT_Q, tn), lambda i, j: (0, j)),
            pl.BlockSpec((tm, tn), lambda i, j: (i, C_GA // tn + j)),
            pl.BlockSpec((tm, tn), lambda i, j: (i, C_GB // tn + j)),
        ],
        out_specs=pl.BlockSpec((tm, tn), lambda i, j: (i, j)),
        out_shape=jax.ShapeDtypeStruct((T, D_MODEL), BF16),
        compiler_params=_cparams(("parallel", "arbitrary")),
        name="merge",
    )(oa, ob, wa, wb, proj, proj)


def _oproj_kernel(m_ref, w_ref, x_ref, g1_ref, sc2_ref, sh2_ref, lg_ref, lb_ref, x1_ref, h2_ref, acc_ref, *, nj):
    j = pl.program_id(1)
    acc_ref[j] = _dot(m_ref[...], w_ref[...])

    @pl.when(j == nj - 1)
    def _():
        bb, tl, D = x_ref.shape
        tn = acc_ref.shape[2]
        sl = lambda t: slice(t * tn, (t + 1) * tn)
        s1 = jnp.zeros((bb, tl, 1), F32)
        for t in range(nj):
            v = ALPHA * x_ref[:, :, sl(t)] + g1_ref[:, :, sl(t)] * acc_ref[t].reshape(bb, tl, tn)
            acc_ref[t] = v.reshape(bb * tl, tn)
            s1 = s1 + jnp.sum(v, -1, keepdims=True)
        mu = s1 * (1.0 / D)
        s2 = jnp.zeros((bb, tl, 1), F32)
        for t in range(nj):
            d = acc_ref[t].reshape(bb, tl, tn) - mu
            s2 = s2 + jnp.sum(d * d, -1, keepdims=True)
        rstd = lax.rsqrt(s2 * (1.0 / D) + LN_EPS)
        for t in range(nj):
            x1 = (acc_ref[t].reshape(bb, tl, tn) - mu) * rstd * lg_ref[:, :, sl(t)] + lb_ref[:, :, sl(t)]
            x1_ref[:, :, sl(t)] = x1
            h2_ref[:, sl(t)] = (x1 * (1.0 + sc2_ref[:, :, sl(t)]) + sh2_ref[:, :, sl(t)]).reshape(bb * tl, tn)


def _oproj_kernel_aliased(m_ref, w_ref, x_ref, g1_ref, sc2_ref, sh2_ref, lg_ref, lb_ref, h2_all_ref,
                          x1_ref, h2_ref, acc_ref, *, nj):
    del h2_all_ref
    _oproj_kernel(m_ref, w_ref, x_ref, g1_ref, sc2_ref, sh2_ref, lg_ref, lb_ref, x1_ref, h2_ref, acc_ref, nj=nj)


def _oproj(m, w_o, x3, mod3, row0, ln_g, ln_b, h2_all, t_all, t_off):
    B, L, D = x3.shape
    bb, tl = _row_tiling(B, L, TM_MIX)
    nl = L // tl
    rows = bb * tl
    mrow = row0 // bb
    tn = TN_MIX
    nj = D // tn
    assert t_off % rows == 0
    boff = t_off // rows
    modspec = lambda c: pl.BlockSpec((bb, 1, D), lambda i, j: (mrow + i // nl, 0, c))
    in_specs = [
        pl.BlockSpec((rows, D), lambda i, j: (i, 0)),
        pl.BlockSpec((D, tn), lambda i, j: (0, j)),
        pl.BlockSpec((bb, tl, D), lambda i, j: (i // nl, i % nl, 0)),
        modspec(2), modspec(4), modspec(3),
        pl.BlockSpec((1, 1, D), lambda i, j: (0, 0, 0)),
        pl.BlockSpec((1, 1, D), lambda i, j: (0, 0, 0)),
    ]
    args = [m, w_o, x3, mod3, mod3, mod3, ln_g.reshape(1, 1, D), ln_b.reshape(1, 1, D), h2_all]
    in_specs.append(pl.BlockSpec(memory_space=pl.ANY))
    aliases = {len(args) - 1: 1}
    kern = functools.partial(_oproj_kernel_aliased, nj=nj)
    return pl.pallas_call(
        kern,
        grid=((B // bb) * nl, nj),
        in_specs=in_specs,
        out_specs=[
            pl.BlockSpec((bb, tl, D), lambda i, j: (i // nl, i % nl, 0)),
            pl.BlockSpec((rows, D), lambda i, j: (boff + i, 0)),
        ],
        out_shape=[
            jax.ShapeDtypeStruct((B, L, D), F32),
            jax.ShapeDtypeStruct((t_all, D), F32),
        ],
        scratch_shapes=[pltpu.VMEM((nj, rows, tn), F32)],
        input_output_aliases=aliases,
        compiler_params=_cparams(("parallel", "arbitrary")),
        name="oproj_ln1",
    )(*args)


def _router_kernel(h_ref, w_ref, b_ref, id_ref, wt_ref):
    logits = jnp.dot(h_ref[...], w_ref[...], precision=lax.Precision.HIGHEST,
                     preferred_element_type=F32) + b_ref[...]
    rows = logits.shape[0]
    lane = lax.broadcasted_iota(jnp.int32, (rows, LANE), 1)
    is_g = lane < N_GROUPS
    gl = jnp.where(is_g, logits, -jnp.inf)
    gmax = jnp.max(gl, axis=1, keepdims=True)
    g_sel = jnp.min(jnp.where(gl == gmax, lane, LANE), axis=1, keepdims=True)
    p_grp = 1.0 / jnp.sum(jnp.where(is_g, jnp.exp(gl - gmax), 0.0), axis=1, keepdims=True)
    e_lo = N_GROUPS + g_sel * EXP_PER_GROUP
    in_g = (lane >= e_lo) & (lane < e_lo + EXP_PER_GROUP)
    el = jnp.where(in_g, logits, -jnp.inf)
    m1 = jnp.max(el, axis=1, keepdims=True)
    i1 = jnp.min(jnp.where(el == m1, lane, LANE), axis=1, keepdims=True)
    el2 = jnp.where(lane == i1, -jnp.inf, el)
    m2 = jnp.max(el2, axis=1, keepdims=True)
    i2 = jnp.min(jnp.where(el2 == m2, lane, LANE), axis=1, keepdims=True)
    e21 = jnp.exp(m2 - m1)
    w1 = p_grp / (1.0 + e21)
    w2 = p_grp * e21 / (1.0 + e21)
    id_ref[...] = jnp.where(lane == 0, i1 - N_GROUPS, jnp.where(lane == 1, i2 - N_GROUPS, 0))
    wt_ref[...] = jnp.where(lane == 0, w1, jnp.where(lane == 1, w2, 0.0))


def _router(h2, w_r, b_r):
    T, D = h2.shape
    tm = 512
    while T % tm:
        tm //= 2
    return pl.pallas_call(
        _router_kernel,
        grid=(T // tm,),
        in_specs=[
            pl.BlockSpec((tm, D), lambda i: (i, 0)),
            pl.BlockSpec((D, LANE), lambda i: (0, 0)),
            pl.BlockSpec((1, LANE), lambda i: (0, 0)),
        ],
        out_specs=[pl.BlockSpec((tm, LANE), lambda i: (i, 0)), pl.BlockSpec((tm, LANE), lambda i: (i, 0))],
        out_shape=[jax.ShapeDtypeStruct((T, LANE), jnp.int32), jax.ShapeDtypeStruct((T, LANE), F32)],
        compiler_params=_cparams(("parallel",)),
        name="router",
    )(h2, w_r, b_r)


def _expert_kernel(blk_e_ref, tok_ref, nused_ref, h_hbm, wg_ref, wu_ref, wd_ref, o_ref,
                   xbuf_ref, xb16_ref, act_ref, sem_ref, *, BM, nfc):
    i = pl.program_id(0)
    c = pl.program_id(1)
    nused = nused_ref[0]
    slot = lax.rem(i, 2)

    def start_gather(blk, sl):
        def body(r, carry):
            tok = tok_ref[blk * BM + r]
            pltpu.make_async_copy(h_hbm.at[pl.ds(tok, 1)], xbuf_ref.at[sl, pl.ds(r, 1)], sem_ref.at[sl]).start()
            return carry
        lax.fori_loop(0, BM, body, 0, unroll=8)

    def wait_gather(sl):
        pltpu.make_async_copy(xbuf_ref.at[sl], xbuf_ref.at[sl], sem_ref.at[sl]).wait()

    @pl.when(jnp.logical_and(c == 0, i < nused))
    def _():
        @pl.when(i == 0)
        def _():
            start_gather(0, 0)

        @pl.when(i + 1 < nused)
        def _():
            start_gather(i + 1, 1 - slot)

        wait_gather(slot)
        xb16_ref[...] = xbuf_ref[slot].astype(BF16)

    @pl.when(i < nused)
    def _():
        x = xb16_ref[...]
        gate = _dot(x, wg_ref[0])
        up = _dot(x, wu_ref[0])
        act_ref[c] = (_silu(gate) * up).astype(BF16)

        @pl.when(c == nfc - 1)
        def _():
            act = jnp.concatenate([act_ref[t] for t in range(nfc)], axis=-1)
            o_ref[...] = _dot(act, wd_ref[0])

    @pl.when(jnp.logical_and(i >= nused, c == nfc - 1))
    def _():
        o_ref[...] = jnp.zeros_like(o_ref)


def _experts(h2, blk_e, slot_tok, nused, w_up16, w_down16, BM):
    T, D = h2.shape
    P = slot_tok.shape[0]
    nblk = P // BM
    fc = FC_MOE
    nfc = D_EXPERT // fc
    kern = functools.partial(_expert_kernel, BM=BM, nfc=nfc)

    def eidx(i, nu):
        return jnp.minimum(i, nu[0] - 1)

    def cidx(i, c, nu):
        return jnp.where(i < nu[0], c, nfc - 1)

    grid_spec = pltpu.PrefetchScalarGridSpec(
        num_scalar_prefetch=3,
        grid=(nblk, nfc),
        in_specs=[
            pl.BlockSpec(memory_space=pl.ANY),
            pl.BlockSpec((1, D, fc), lambda i, c, be, tk, nu: (be[eidx(i, nu)], 0, cidx(i, c, nu))),
            pl.BlockSpec((1, D, fc), lambda i, c, be, tk, nu: (be[eidx(i, nu)], 0, nfc + cidx(i, c, nu))),
            pl.BlockSpec((1, D_EXPERT, D), lambda i, c, be, tk, nu: (be[eidx(i, nu)], 0, 0)),
        ],
        out_specs=pl.BlockSpec((BM, D), lambda i, c, be, tk, nu: (i, 0)),
        scratch_shapes=[
            pltpu.VMEM((2, BM, D), F32),
            pltpu.VMEM((BM, D), BF16),
            pltpu.VMEM((nfc, BM, fc), BF16),
            pltpu.SemaphoreType.DMA((2,)),
        ],
    )
    return pl.pallas_call(
        kern,
        grid_spec=grid_spec,
        out_shape=jax.ShapeDtypeStruct((P, D), F32),
        compiler_params=_cparams(("arbitrary", "arbitrary")),
        name="experts",
    )(blk_e, slot_tok, nused, h2, w_up16, w_up16, w_down16)


def _route_metadata(ids, T, BM):
    expert = ids[:, 0:2].reshape(-1)
    A = 2 * T
    onehot = (expert[:, None] == jnp.arange(N_EXPERTS, dtype=jnp.int32)[None, :]).astype(jnp.int32)
    csum = jnp.cumsum(onehot, axis=0)
    counts = csum[-1]
    rank = jnp.sum((csum - onehot) * onehot, axis=1)
    padded = ((counts + BM - 1) // BM) * BM
    pend = jnp.cumsum(padded)
    pstart = pend - padded
    dest = (pstart[expert] + rank).astype(jnp.int32)
    P = ((A + BM - 1) // BM) * BM + N_EXPERTS * BM
    nblk = P // BM
    tok = jnp.arange(A, dtype=jnp.int32) // 2
    slot_tok = jnp.zeros((P,), jnp.int32).at[dest].set(tok)
    starts = jnp.arange(nblk, dtype=jnp.int32) * BM
    blk_e = jnp.minimum(jnp.sum((pend[None, :] <= starts[:, None]).astype(jnp.int32), axis=1), N_EXPERTS - 1)
    nused = (pend[-1] // BM).astype(jnp.int32).reshape(1)
    return dest, slot_tok, blk_e, nused


def _final_kernel(dest_ref, yb_hbm, x1_ref, g2_ref, wt_ref, lg_ref, lb_ref, o_ref, buf_ref, sem_ref, *, rows, ntiles):
    i = pl.program_id(0)
    slot = lax.rem(i, 2)

    def start_gather(tile, sl):
        def body(r, carry):
            d = dest_ref[tile * 2 * rows + r]
            pltpu.make_async_copy(yb_hbm.at[pl.ds(d, 1)], buf_ref.at[sl, pl.ds(r, 1)], sem_ref.at[sl]).start()
            return carry
        lax.fori_loop(0, 2 * rows, body, 0, unroll=8)

    def wait_gather(sl):
        pltpu.make_async_copy(buf_ref.at[sl], buf_ref.at[sl], sem_ref.at[sl]).wait()

    @pl.when(i == 0)
    def _():
        start_gather(0, 0)

    @pl.when(i + 1 < ntiles)
    def _():
        start_gather(i + 1, 1 - slot)

    wait_gather(slot)
    bb, tl, D = x1_ref.shape
    tn = TN_MIX
    sl = lambda t: slice(t * tn, (t + 1) * tn)
    w0 = wt_ref[:, 0:1]
    w1 = wt_ref[:, 1:2]
    s1 = jnp.zeros((bb, tl, 1), F32)
    for t in range(D // tn):
        f = w0 * buf_ref[slot, 0:rows, sl(t)] + w1 * buf_ref[slot, rows:2 * rows, sl(t)]
        v = ALPHA * x1_ref[:, :, sl(t)] + g2_ref[:, :, sl(t)] * f.reshape(bb, tl, tn)
        o_ref[:, :, sl(t)] = v
        s1 = s1 + jnp.sum(v, -1, keepdims=True)
    mu = s1 * (1.0 / D)
    s2 = jnp.zeros((bb, tl, 1), F32)
    for t in range(D // tn):
        d = o_ref[:, :, sl(t)] - mu
        s2 = s2 + jnp.sum(d * d, -1, keepdims=True)
    rstd = lax.rsqrt(s2 * (1.0 / D) + LN_EPS)
    for t in range(D // tn):
        o_ref[:, :, sl(t)] = (o_ref[:, :, sl(t)] - mu) * rstd * lg_ref[:, :, sl(t)] + lb_ref[:, :, sl(t)]


def _final(dest, yb, x1, mod3, row0, wts, t_off, ln_g, ln_b):
    B, L, D = x1.shape
    T = B * L
    bb, tl = _row_tiling(B, L, TM_MIX)
    nl = L // tl
    rows = bb * tl
    mrow = row0 // bb
    ntiles = (B // bb) * nl
    woff = t_off // rows
    dest_tiles = dest[2 * t_off:2 * (t_off + T)].reshape(T // rows, rows, 2).transpose(0, 2, 1).reshape(-1)
    kern = functools.partial(_final_kernel, rows=rows, ntiles=ntiles)
    grid_spec = pltpu.PrefetchScalarGridSpec(
        num_scalar_prefetch=1,
        grid=(ntiles,),
        in_specs=[
            pl.BlockSpec(memory_space=pl.ANY),
            pl.BlockSpec((bb, tl, D), lambda i, d: (i // nl, i % nl, 0)),
            pl.BlockSpec((bb, 1, D), lambda i, d: (mrow + i // nl, 0, 5)),
            pl.BlockSpec((rows, LANE), lambda i, d: (woff + i, 0)),
            pl.BlockSpec((1, 1, D), lambda i, d: (0, 0, 0)),
            pl.BlockSpec((1, 1, D), lambda i, d: (0, 0, 0)),
        ],
        out_specs=pl.BlockSpec((bb, tl, D), lambda i, d: (i // nl, i % nl, 0)),
        scratch_shapes=[
            pltpu.VMEM((2, 2 * rows, D), F32),
            pltpu.SemaphoreType.DMA((2,)),
        ],
    )
    return pl.pallas_call(
        kern,
        grid_spec=grid_spec,
        out_shape=jax.ShapeDtypeStruct((B, L, D), F32),
        compiler_params=_cparams(("arbitrary",)),
        name="combine_ln2",
    )(dest_tiles, yb, x1, mod3, wts, ln_g.reshape(1, 1, D), ln_b.reshape(1, 1, D))


def _pad_keys(t, S_pad):
    S = t.shape[1]
    if S == S_pad:
        return t
    return jnp.pad(t, ((0, 0), (0, S_pad - S), (0, 0)))


def _mixers(x3, mod3, row0, conv_state, gdn_state, past_k, past_v, past_ik, wp, h2_all, t_all, t_off):
    B, L, D = x3.shape
    pos0 = past_k.shape[1]
    S = pos0 + L
    proj = _inproj(x3, mod3, row0, wp["w_in"])

    oa, new_gdn = _gdn(proj, B, L, conv_state, gdn_state, wp["conv_w"], wp["a_log"], wp["dt_bias"],
                       wp["gdn_norm_w"])
    new_conv = proj.reshape(B, L, NP)[:, L - (CONV_W - 1):, C_QKV:C_QKV + GDN_CONV_CH]

    qb = min(Q_BLOCK, L)
    q_hm, kf, k16, vf, v16, iq_hm, ikf, ik16 = _rope(proj, B, L, pos0, qb)
    S_pad = ((S + TK_ATT - 1) // TK_ATT) * TK_ATT
    k_all = _pad_keys(jnp.concatenate([past_k.reshape(B, pos0, ATT_KV).astype(BF16), k16.reshape(B, L, ATT_KV)], 1), S_pad)
    v_all = _pad_keys(jnp.concatenate([past_v.reshape(B, pos0, ATT_KV).astype(BF16), v16.reshape(B, L, ATT_KV)], 1), S_pad)
    ik_all = _pad_keys(jnp.concatenate([past_ik.astype(BF16), ik16.reshape(B, L, IDX_HD)], 1), S_pad)
    ob = _dsa(iq_hm, proj, q_hm, ik_all, k_all, v_all, B, L, S, pos0)

    m = _merge(oa, ob, wp["w_br_a"], wp["w_br_b"], proj)
    x1, h2_all = _oproj(m, wp["w_o"], x3, mod3, row0, wp["ln1_g"], wp["ln1_b"], h2_all, t_all, t_off)

    kb_out = kf.reshape(B, L, ATT_KV_HEADS, ATT_HD)
    vb_out = vf.reshape(B, L, ATT_KV_HEADS, ATT_HD)
    ik_out = ikf.reshape(B, L, IDX_HD)
    return x1, h2_all, (new_conv, new_gdn, kb_out, vb_out, ik_out)


WP_TN = 512
WP_TR = 1024
WP_NSRC = WP_TN // LANE + 1


def _w_in_source_columns():
    sizes = (GDN_CONV_CH, GDN_HEADS, GDN_HEADS, GDN_VAL, ATT_Q, ATT_KV, ATT_KV, IDX_HEADS * IDX_HD, IDX_HD,
             IDX_HEADS, 2 * D_MODEL)
    offs, off = [], 0
    for s in sizes:
        offs.append(off)
        off += s
    o_qkv, o_a, o_b, o_z, o_q, o_k, o_v, o_iq, o_ik, o_iw, o_g = offs
    srcs = []
    for dst, src, width in ((C_QKV, o_qkv, GDN_CONV_CH), (C_Z, o_z, GDN_VAL), (C_IQ, o_iq, IDX_HEADS * IDX_HD),
                            (C_Q, o_q, ATT_Q), (C_K, o_k, ATT_KV), (C_V, o_v, ATT_KV), (C_IK, -1, WP_TN),
                            (C_GA, o_g, 2 * D_MODEL)):
        assert dst == len(srcs) * WP_TN and width % WP_TN == 0
        srcs += [src + t * WP_TN if src >= 0 else -1 for t in range(width // WP_TN)]
    assert len(srcs) * WP_TN == NP
    return srcs, (o_a, o_b, o_ik, o_iw)


def _narrow_pieces():
    o_a, o_b, o_ik, o_iw = _W_IN_NARROW
    return ((o_ik, IDX_HD), (o_a, GDN_HEADS), (o_b, GDN_HEADS), (o_iw, IDX_HEADS))


def _narrow_blocks():
    blks = set()
    for o, w in _narrow_pieces():
        blks.update(range(o // LANE, (o + w - 1) // LANE + 1))
    return sorted(blks)


def _wperm_kernel(base_ref, shift_ref, *refs):
    src_refs = refs[:WP_NSRC]
    nb = dict(zip(_narrow_blocks(), refs[WP_NSRC:-1]))
    o_ref = refs[-1]
    shift = shift_ref[pl.program_id(0)]

    @pl.when(shift < 0)
    def _():
        def piece(o, w):
            blk, ln = o // LANE, o % LANE
            if ln + w <= LANE:
                return nb[blk][:, ln:ln + w]
            return jnp.concatenate([nb[blk][:, ln:], nb[blk + 1][:, :ln + w - LANE]], axis=1)

        parts = [piece(o, w) for o, w in _narrow_pieces()]
        used = sum(w for _, w in _narrow_pieces())
        parts.append(jnp.zeros((o_ref.shape[0], WP_TN - used), F32))
        o_ref[...] = jnp.concatenate(parts, axis=1).astype(BF16)

    for sh in sorted(set(s for s in _W_IN_SHIFTS if s >= 0)):
        @pl.when(shift == sh)
        def _(sh=sh):
            a = jnp.concatenate([r[...] for r in src_refs], axis=1)
            o_ref[...] = a[:, sh:sh + WP_TN].astype(BF16)


_W_IN_SRCS, _W_IN_NARROW = _w_in_source_columns()
_W_IN_SHIFTS = [s % LANE if s >= 0 else -1 for s in _W_IN_SRCS]


def _permute_w_in(w_in):
    D, ncols = w_in.shape
    base = jnp.asarray([max(s, 0) // LANE for s in _W_IN_SRCS], jnp.int32)
    shift = jnp.asarray(_W_IN_SHIFTS, jnp.int32)
    last = (ncols - 1) // LANE

    def src_spec(t):
        return pl.BlockSpec((WP_TR, LANE), lambda j, r, base, shift: (r, jnp.minimum(base[j] + t, last)))

    def narrow_spec(blk):
        return pl.BlockSpec((WP_TR, LANE), lambda j, r, base, shift: (r, blk))

    nblks = _narrow_blocks()
    grid_spec = pltpu.PrefetchScalarGridSpec(
        num_scalar_prefetch=2,
        grid=(NP // WP_TN, D // WP_TR),
        in_specs=[src_spec(t) for t in range(WP_NSRC)] + [narrow_spec(blk) for blk in nblks],
        out_specs=pl.BlockSpec((WP_TR, WP_TN), lambda j, r, base, shift: (r, j)),
    )
    return pl.pallas_call(
        _wperm_kernel,
        grid_spec=grid_spec,
        out_shape=jax.ShapeDtypeStruct((D, NP), BF16),
        compiler_params=_cparams(("parallel", "parallel")),
        name="w_in_permute",
    )(base, shift, *([w_in] * (WP_NSRC + len(nblks))))


def kernel(x_prompt, x_sample, c_prompt, c_sample, state_conv, state_gdn, cache_k, cache_v, cache_idx_k,
           w_ada, b_ada, w_in, conv_w, a_log, dt_bias, gdn_norm_w, w_br_a, w_br_b, w_o,
           ln1_g, ln1_b, ln2_g, ln2_b, w_grp, b_grp, w_rtr, b_rtr, w_up, w_down):
    Bp, Lp, D = x_prompt.shape
    Bs, Ls, _ = x_sample.shape
    l = 0
    row0_p = Bs
    nrows = ((Bs + Bp + 7) // 8) * 8
    c_all = jnp.concatenate([c_sample, c_prompt, jnp.zeros((nrows - Bs - Bp, D), F32)], 0)
    mod = _ada(c_all, w_ada[l], b_ada[l])
    mod3 = mod.reshape(nrows, 1, 6 * D)

    nr = LANE - N_GROUPS - N_EXPERTS
    wp = dict(
        w_in=_permute_w_in(w_in[l]),
        conv_w=conv_w[l], a_log=a_log[l], dt_bias=dt_bias[l], gdn_norm_w=gdn_norm_w[l],
        w_br_a=w_br_a[l].astype(BF16), w_br_b=w_br_b[l].astype(BF16), w_o=w_o[l].astype(BF16),
        ln1_g=ln1_g[l], ln1_b=ln1_b[l], ln2_g=ln2_g[l], ln2_b=ln2_b[l],
        w_r=jnp.concatenate([w_grp[l], w_rtr[l], jnp.zeros((D, nr), F32)], 1),
        b_r=jnp.concatenate([b_grp[l], b_rtr[l], jnp.zeros((nr,), F32)]).reshape(1, LANE),
        w_up=w_up[l].astype(BF16), w_down=w_down[l].astype(BF16),
    )

    zc = jnp.zeros((Bp, CONV_W - 1, GDN_CONV_CH), F32)
    zs = jnp.zeros((Bp, GDN_HEADS, GDN_DK, GDN_DV), F32)
    zk = jnp.zeros((Bp, 0, ATT_KV_HEADS, ATT_HD), F32)
    zik = jnp.zeros((Bp, 0, IDX_HD), F32)
    Tp, Ts = Bp * Lp, Bs * Ls
    t_all = Tp + Ts
    h2_all = jnp.zeros((t_all, D), F32)
    x1p, h2_all, (c1, g1, k1, v1, i1) = _mixers(x_prompt, mod3, row0_p, zc, zs, zk, zk, zik, wp, h2_all, t_all, 0)
    x1s, h2_all, (c2, g2, k2, v2, i2) = _mixers(x_sample, mod3, 0, state_conv[l], state_gdn[l], cache_k[l],
                                                 cache_v[l], cache_idx_k[l], wp, h2_all, t_all, Tp)

    ids, wts = _router(h2_all, wp["w_r"], wp["b_r"])
    dest, slot_tok, blk_e, nused = _route_metadata(ids, t_all, MOE_BM)
    yb = _experts(h2_all, blk_e, slot_tok, nused, wp["w_up"], wp["w_down"], MOE_BM)
    yp = _final(dest, yb, x1p, mod3, row0_p, wts, 0, wp["ln2_g"], wp["ln2_b"])
    ys = _final(dest, yb, x1s, mod3, 0, wts, Tp, wp["ln2_g"], wp["ln2_b"])
    st = lambda t: t[None]
    return (yp, ys, st(c1), st(g1), st(k1), st(v1), st(i1), st(c2), st(g2), st(k2), st(v2), st(i2))
```

```python
import functools

import jax
import jax.numpy as jnp
from jax import lax
from jax.experimental import pallas as pl
from jax.experimental.pallas import tpu as pltpu

F32 = jnp.float32
BF16 = jnp.bfloat16

D_MODEL = 4096
CHUNK = 64
GDN_HEADS = D_MODEL // 256
GDN_DK = 128
GDN_DV = 128
GDN_KEY = GDN_HEADS * GDN_DK
GDN_VAL = GDN_HEADS * GDN_DV
GDN_CONV_CH = 2 * GDN_KEY + GDN_VAL
CONV_W = 4
ATT_HEADS = D_MODEL // 256
ATT_KV_HEADS = ATT_HEADS // 4
ATT_REP = ATT_HEADS // ATT_KV_HEADS
ATT_HD = 128
ATT_Q = ATT_HEADS * ATT_HD
ATT_KV = ATT_KV_HEADS * ATT_HD
IDX_HEADS = D_MODEL // 128
IDX_HD = 128
TOPK_KEYS = 256
Q_BLOCK = 128
ROPE_THETA = 500000.0
ROPE_ROT = ATT_HD // 4
ROPE_HALF = ROPE_ROT // 2
N_GROUPS = 4
EXP_PER_GROUP = 8
N_EXPERTS = N_GROUPS * EXP_PER_GROUP
D_EXPERT = D_MODEL // 4
DEPTH = 1
ALPHA = (2.0 * DEPTH) ** 0.25
LN_EPS = 1e-5
RMS_EPS = 1e-6

C_QKV = 0
C_Z = C_QKV + GDN_CONV_CH
C_IQ = C_Z + GDN_VAL
C_Q = C_IQ + IDX_HEADS * IDX_HD
C_K = C_Q + ATT_Q
C_V = C_K + ATT_KV
C_IK = C_V + ATT_KV
C_SM = C_IK + IDX_HD
SM_W = 384
C_GA = C_SM + SM_W
C_GB = C_GA + D_MODEL
NP = C_GB + D_MODEL
SM_A, SM_B, SM_IW = 0, GDN_HEADS, 2 * GDN_HEADS

LANE = 128
NEG_BIG = -1e30
VMEM_LIMIT = 56 * 1024 * 1024

TN_IN = 512
TM_IN = 1024
TM_MIX = 256
TM_OPROJ = 512
TN_MIX = 512
TK_ATT = 512
FC_MOE = 256
MOE_BM = 256


def _cparams(sem):
    return pltpu.CompilerParams(dimension_semantics=sem, vmem_limit_bytes=VMEM_LIMIT)


def _dot(a, b):
    return jnp.dot(a, b, preferred_element_type=F32)


def _dot_nt(a, b):
    return lax.dot_general(a, b, (((1,), (1,)), ((), ())), preferred_element_type=F32)


def _dot_tn(a, b):
    return lax.dot_general(a, b, (((0,), (0,)), ((), ())), preferred_element_type=F32)


def _split_bf16(a):
    hi = a.astype(BF16)
    lo = (a - hi.astype(F32)).astype(BF16)
    return hi, lo


def _mm3(a, b, dot=_dot):
    ah, al = _split_bf16(a)
    bh, bl = _split_bf16(b)
    return dot(ah, bh) + (dot(ah, bl) + dot(al, bh))


def _mm1(a, b):
    return _dot(a.astype(BF16), b.astype(BF16))


def _mm1_nt(a, b):
    return _dot_nt(a.astype(BF16), b.astype(BF16))


def _sigmoid(x):
    return 1.0 / (1.0 + jnp.exp(-x))


def _silu(x):
    return x * _sigmoid(x)


def _softplus(x):
    return jnp.maximum(x, 0.0) + jnp.log(1.0 + jnp.exp(-jnp.abs(x)))


def _ada_kernel(c_ref, w_ref, b_ref, o_ref):
    s = _silu(c_ref[...])
    o_ref[...] = _dot(s.astype(BF16), w_ref[...].astype(BF16)) + b_ref[...]


def _ada(c_all, w_ada, b_ada):
    R, D = c_all.shape
    N = w_ada.shape[1]
    tn = 512
    return pl.pallas_call(
        _ada_kernel,
        grid=(N // tn,),
        in_specs=[
            pl.BlockSpec((R, D), lambda j: (0, 0)),
            pl.BlockSpec((D, tn), lambda j: (0, j)),
            pl.BlockSpec((1, tn), lambda j: (0, j)),
        ],
        out_specs=pl.BlockSpec((R, tn), lambda j: (0, j)),
        out_shape=jax.ShapeDtypeStruct((R, N), F32),
        compiler_params=_cparams(("parallel",)),
        name="ada",
    )(c_all, w_ada, b_ada.reshape(1, N))


def _inproj_kernel(x_ref, sc_ref, sh_ref, w_ref, o_ref, h_ref):
    @pl.when(pl.program_id(1) == 0)
    def _():
        h = x_ref[...] * (1.0 + sc_ref[...]) + sh_ref[...]
        h_ref[...] = h.reshape(h_ref.shape).astype(BF16)

    o_ref[...] = _dot(h_ref[...], w_ref[...])


def _row_tiling(B, L, tm):
    if L >= tm:
        return 1, tm
    bb = max(1, min(B, tm // L))
    while B % bb:
        bb -= 1
    return bb, L


def _inproj(x3, mod3, row0, w_perm):
    B, L, D = x3.shape
    bb, tl = _row_tiling(B, L, TM_IN)
    nl = L // tl
    rows = bb * tl
    mrow = row0 // bb
    return pl.pallas_call(
        _inproj_kernel,
        grid=((B // bb) * nl, NP // TN_IN),
        in_specs=[
            pl.BlockSpec((bb, tl, D), lambda i, j: (i // nl, i % nl, 0), pipeline_mode=pl.Buffered(1)),
            pl.BlockSpec((bb, 1, D), lambda i, j: (mrow + i // nl, 0, 1)),
            pl.BlockSpec((bb, 1, D), lambda i, j: (mrow + i // nl, 0, 0)),
            pl.BlockSpec((D, TN_IN), lambda i, j: (0, j)),
        ],
        out_specs=pl.BlockSpec((rows, TN_IN), lambda i, j: (i, j)),
        out_shape=jax.ShapeDtypeStruct((B * L, NP), F32),
        scratch_shapes=[pltpu.VMEM((rows, D), BF16)],
        compiler_params=_cparams(("parallel", "arbitrary")),
        name="inproj",
    )(x3, mod3, mod3, w_perm)


GDN_GROUP = 4


def _split3_bf16(a):
    hi = a.astype(BF16)
    r = a - hi.astype(F32)
    mid = r.astype(BF16)
    lo = (r - mid.astype(F32)).astype(BF16)
    return hi, mid, lo


def _dot_exact01(a, b01, a_is_01=False):
    if a_is_01:
        h, m, l = _split3_bf16(b01)
        return _dot(a, h) + (_dot(a, m) + _dot(a, l))
    h, m, l = _split3_bf16(a)
    return _dot(h, b01) + (_dot(m, b01) + _dot(l, b01))


def _mm3p(a_hl, b_hl, dot=_dot):
    (ah, al), (bh, bl) = a_hl, b_hl
    return dot(ah, bh) + (dot(ah, bl) + dot(al, bh))


def _block_rows_hl(x_hl, nblk, mask01):
    return tuple(jnp.concatenate([p] * nblk, axis=0) * mask01 for p in x_hl)


def _gdn_kernel(qkv_ref, z_ref, sm_ref, cst_ref, cw_ref, alr_ref, dtr_ref, nw_ref, s0_ref,
                o_ref, sfin_ref, s_ref, ext_ref, mt_ref, mk_ref, mx_ref, ml_ref, *, C, nchunks):
    n = pl.program_id(1)
    H = GDN_HEADS
    G = GDN_GROUP
    NG = H // G
    GC = G * C
    PADR = 8
    logc = C.bit_length() - 1

    @pl.when(n == 0)
    def _():
        s_ref[...] = s0_ref[0]
        ext_ref[PADR - (CONV_W - 1):PADR, :] = cst_ref[0]

    ext_ref[PADR:PADR + C, :] = qkv_ref[...]
    cw = cw_ref[...]
    y = ext_ref[PADR - 3:PADR - 3 + C, :] * cw[0:1]
    for jw in range(1, CONV_W):
        y = y + ext_ref[PADR - 3 + jw:PADR - 3 + jw + C, :] * cw[jw:jw + 1]
    y = _silu(y)
    hist = ext_ref[PADR + C - (CONV_W - 1):PADR + C, :]
    ext_ref[PADR - (CONV_W - 1):PADR, :] = hist

    sm = sm_ref[...]
    g_col = -jnp.exp(alr_ref[...]) * _softplus(sm[:, SM_A:SM_A + H] + dtr_ref[...])
    beta_col = _sigmoid(sm[:, SM_B:SM_B + H])

    ii = lax.broadcasted_iota(jnp.int32, (C, C), 0)
    jj = lax.broadcasted_iota(jnp.int32, (C, C), 1)
    tril01 = jnp.where(jj <= ii, 1.0, 0.0).astype(BF16)
    gc_col = _dot_exact01(tril01, g_col, a_is_01=True)
    egc_col = jnp.exp(gc_col)
    gc_last = gc_col[C - 1:C, :]
    ekd_col = jnp.exp(gc_last - gc_col)
    egl = jnp.exp(gc_last)

    W = H * C
    hrow = lax.broadcasted_iota(jnp.int32, (H, W), 0)
    hlane = jnp.right_shift(lax.broadcasted_iota(jnp.int32, (H, W), 1), logc)
    e_seg = jnp.where(hrow == hlane, 1.0, 0.0).astype(BF16)
    gseg = _dot_exact01(gc_col, e_seg)
    ri = lax.broadcasted_iota(jnp.int32, (C, W), 0)
    cj = jnp.bitwise_and(lax.broadcasted_iota(jnp.int32, (C, W), 1), C - 1)
    grow = jnp.sum(jnp.where(ri == cj, gseg, 0.0), axis=0, keepdims=True)
    decay_all = jnp.where(cj <= ri, jnp.exp(gseg - grow), 0.0)

    gi = lax.broadcasted_iota(jnp.int32, (C, GC), 0)
    gj = jnp.bitwise_and(lax.broadcasted_iota(jnp.int32, (C, GC), 1), C - 1)
    strict = gj < gi
    eye = jnp.where(gi == gj, 1.0, 0.0)
    nlev = logc

    def lev_mask(lev):
        return ((jnp.right_shift(gi, lev + 1) == jnp.right_shift(gj, lev + 1))
                & (jnp.bitwise_and(jnp.right_shift(gi, lev), 1) == 1)
                & (jnp.bitwise_and(jnp.right_shift(gj, lev), 1) == 0))

    @pl.when(n == 0)
    def _():
        def own(ncols, col_head):
            r = jnp.right_shift(lax.broadcasted_iota(jnp.int32, (G * C, ncols), 0), logc)
            c = col_head(lax.broadcasted_iota(jnp.int32, (G * C, ncols), 1))
            return jnp.where(r == c, 1.0, 0.0).astype(BF16)

        mt_ref[...] = own(GC, lambda c: jnp.right_shift(c, logc))
        mk_ref[...] = own(G * GDN_DK, lambda c: jnp.right_shift(c, 7))
        mx_ref[...] = own(2 * G * GDN_DK, lambda c: jnp.bitwise_and(jnp.right_shift(c, 7), G - 1))
        for lev in range(1, nlev):
            ml_ref[lev] = jnp.where(lev_mask(lev), 1.0, 0.0).astype(BF16)

    nw = nw_ref[...]
    z = z_ref[...]

    lmats, lm_hls, intras, rhss, qds, kds = [], [], [], [], [], []
    for g in range(NG):
        qs, ks, kbs, vbs, kes = [], [], [], [], []
        for t in range(G):
            h = g * G + t
            qh = y[:, h * GDN_DK:(h + 1) * GDN_DK]
            kh = y[:, GDN_KEY + h * GDN_DK:GDN_KEY + (h + 1) * GDN_DK]
            vh = y[:, 2 * GDN_KEY + h * GDN_DV:2 * GDN_KEY + (h + 1) * GDN_DV]
            q = qh * lax.rsqrt(jnp.sum(qh * qh, -1, keepdims=True) + RMS_EPS) * (GDN_DK ** -0.5)
            k = kh * lax.rsqrt(jnp.sum(kh * kh, -1, keepdims=True) + RMS_EPS)
            beta = beta_col[:, h:h + 1]
            egc = egc_col[:, h:h + 1]
            kb = k * beta
            qs.append(q)
            ks.append(k)
            kbs.append(kb)
            vbs.append(vh * beta)
            kes.append(kb * egc)
            qds.append(q * egc)
            kds.append(k * ekd_col[:, h:h + 1])
        bdk_hl = _block_rows_hl(_split_bf16(jnp.concatenate(ks, axis=-1)), G, mk_ref[...])
        dec = decay_all[:, g * GC:(g + 1) * GC]
        kk = _mm3p(_split_bf16(jnp.concatenate(kbs, axis=-1)), bdk_hl, _dot_nt)
        qk = _dot_nt(jnp.concatenate(qs, axis=-1).astype(BF16), bdk_hl[0])
        lmat = jnp.where(strict, kk * dec, 0.0)
        lmats.append(lmat)
        lm_hls.append(_split_bf16(lmat))
        intras.append((qk * dec).astype(BF16))
        rhss.append(_split_bf16(jnp.concatenate(vbs + kes, axis=-1)))

    tinvs = [eye - jnp.where(lev_mask(0), lmats[g], 0.0) for g in range(NG)]
    for lev in range(1, nlev):
        ml = ml_ref[lev]
        for g in range(NG):
            t_hl = _split_bf16(tinvs[g])
            b_hl = (lm_hls[g][0] * ml, lm_hls[g][1] * ml)
            p = _mm3p(b_hl, _block_rows_hl(t_hl, G, mt_ref[...]))
            tinvs[g] = tinvs[g] - _mm3p(t_hl, _block_rows_hl(_split_bf16(p), G, mt_ref[...]))

    for g in range(NG):
        x = _mm3p(_split_bf16(tinvs[g]), _block_rows_hl(rhss[g], G, mx_ref[...]))
        u_cat = x[:, :G * GDN_DV]
        w_cat = x[:, G * GDN_DV:]
        for t in range(G):
            h = g * G + t
            s = s_ref[h]
            s_hl = _split_bf16(s)
            lanes = slice(t * GDN_DV, (t + 1) * GDN_DV)
            v_new = u_cat[:, lanes] - _mm3p(_split_bf16(w_cat[:, lanes]), s_hl)
            v_hl = _split_bf16(v_new)
            o = _dot(qds[h].astype(BF16), s_hl[0]) + _dot(intras[g][:, t * C:(t + 1) * C], v_hl[0])
            s_ref[h] = s * egl[:, h:h + 1] + _mm3p(_split_bf16(kds[h]), v_hl, _dot_tn)
            zh = z[:, h * GDN_DV:(h + 1) * GDN_DV]
            on = o * lax.rsqrt(jnp.mean(o * o, -1, keepdims=True) + RMS_EPS) * nw * _silu(zh)
            o_ref[:, h * GDN_DV:(h + 1) * GDN_DV] = on.astype(o_ref.dtype)

    @pl.when(n == nchunks - 1)
    def _():
        sfin_ref[0] = s_ref[...]


def _gdn(proj, B, L, conv_state, gdn_state, conv_w, a_log, dt_bias, norm_w):
    C = min(CHUNK, L)
    N = L // C
    H = GDN_HEADS
    kern = functools.partial(_gdn_kernel, C=C, nchunks=N)
    return pl.pallas_call(
        kern,
        grid=(B, N),
        in_specs=[
            pl.BlockSpec((C, GDN_CONV_CH), lambda b, n: (b * N + n, C_QKV // GDN_CONV_CH)),
            pl.BlockSpec((C, GDN_VAL), lambda b, n: (b * N + n, C_Z // GDN_VAL)),
            pl.BlockSpec((C, LANE), lambda b, n: (b * N + n, C_SM // LANE)),
            pl.BlockSpec((1, CONV_W - 1, GDN_CONV_CH), lambda b, n: (b, 0, 0)),
            pl.BlockSpec((CONV_W, GDN_CONV_CH), lambda b, n: (0, 0)),
            pl.BlockSpec((1, H), lambda b, n: (0, 0)),
            pl.BlockSpec((1, H), lambda b, n: (0, 0)),
            pl.BlockSpec((1, GDN_DV), lambda b, n: (0, 0)),
            pl.BlockSpec((1, H, GDN_DK, GDN_DV), lambda b, n: (b, 0, 0, 0)),
        ],
        out_specs=[
            pl.BlockSpec((C, GDN_VAL), lambda b, n: (b * N + n, 0)),
            pl.BlockSpec((1, H, GDN_DK, GDN_DV), lambda b, n: (b, 0, 0, 0)),
        ],
        out_shape=[
            jax.ShapeDtypeStruct((B * L, GDN_VAL), BF16),
            jax.ShapeDtypeStruct((B, H, GDN_DK, GDN_DV), F32),
        ],
        scratch_shapes=[
            pltpu.VMEM((H, GDN_DK, GDN_DV), F32),
            pltpu.VMEM((8 + C, GDN_CONV_CH), F32),
            pltpu.VMEM((GDN_GROUP * C, GDN_GROUP * C), BF16),
            pltpu.VMEM((GDN_GROUP * C, GDN_GROUP * GDN_DK), BF16),
            pltpu.VMEM((GDN_GROUP * C, 2 * GDN_GROUP * GDN_DK), BF16),
            pltpu.VMEM((C.bit_length() - 1, C, GDN_GROUP * C), BF16),
        ],
        compiler_params=_cparams(("parallel", "arbitrary")),
        name="gdn",
    )(proj, proj, proj, conv_state, conv_w, a_log.reshape(1, H), dt_bias.reshape(1, H),
      norm_w.reshape(1, GDN_DV), gdn_state)


def _rope_kernel(q_ref, k_ref, v_ref, iq_ref, ik_ref, cos_ref, sa_ref, sb_ref,
                 qo_ref, kf_ref, kb_ref, vf_ref, vb_ref, iqo_ref, ikf_ref, ikb_ref):
    cosf = cos_ref[...]
    sa = sa_ref[...]
    sb = sb_ref[...]

    def rope(x):
        return (x * cosf + pltpu.roll(x, LANE - ROPE_HALF, 1) * sa + pltpu.roll(x, ROPE_HALF, 1) * sb)

    for h in range(ATT_HEADS):
        qo_ref[0, h] = rope(q_ref[:, h * ATT_HD:(h + 1) * ATT_HD]).astype(BF16)
    for h in range(ATT_KV_HEADS):
        kr = rope(k_ref[:, h * ATT_HD:(h + 1) * ATT_HD])
        kf_ref[:, h * ATT_HD:(h + 1) * ATT_HD] = kr
        kb_ref[:, h * ATT_HD:(h + 1) * ATT_HD] = kr.astype(BF16)
    v = v_ref[...]
    vf_ref[...] = v
    vb_ref[...] = v.astype(BF16)
    for h in range(IDX_HEADS):
        iqo_ref[0, h] = rope(iq_ref[:, h * IDX_HD:(h + 1) * IDX_HD]).astype(BF16)
    ikr = rope(ik_ref[...])
    ikf_ref[...] = ikr
    ikb_ref[...] = ikr.astype(BF16)


def _rope_tables(L, pos0, reps):
    inv = jnp.power(ROPE_THETA, -jnp.arange(ROPE_HALF, dtype=F32) * (2.0 / ROPE_ROT))
    ang = (pos0 + jnp.arange(L)).astype(F32)[:, None] * inv[None, :]
    cos, sin = jnp.cos(ang), jnp.sin(ang)
    z16 = jnp.zeros((L, ROPE_HALF), F32)
    rest0 = jnp.zeros((L, ATT_HD - ROPE_ROT), F32)
    cosf = jnp.concatenate([cos, cos, jnp.ones((L, ATT_HD - ROPE_ROT), F32)], -1)
    sa = jnp.concatenate([-sin, z16, rest0], -1)
    sb = jnp.concatenate([z16, sin, rest0], -1)
    if reps > 1:
        cosf, sa, sb = (jnp.tile(t, (reps, 1)) for t in (cosf, sa, sb))
    return cosf, sa, sb


def _rope(proj, B, L, pos0, qb):
    T = B * L
    reps = 1
    tab_rows = L
    cosf, sa, sb = _rope_tables(L, pos0, reps)
    nq = L // qb
    tspec = pl.BlockSpec((qb, LANE), lambda i: (i % nq, 0))
    outs = pl.pallas_call(
        _rope_kernel,
        grid=(T // qb,),
        in_specs=[
            pl.BlockSpec((qb, ATT_Q), lambda i: (i, C_Q // ATT_Q)),
            pl.BlockSpec((qb, ATT_KV), lambda i: (i, C_K // ATT_KV)),
            pl.BlockSpec((qb, ATT_KV), lambda i: (i, C_V // ATT_KV)),
            pl.BlockSpec((qb, IDX_HEADS * IDX_HD), lambda i: (i, C_IQ // (IDX_HEADS * IDX_HD))),
            pl.BlockSpec((qb, IDX_HD), lambda i: (i, C_IK // IDX_HD)),
            tspec, tspec, tspec,
        ],
        out_specs=[
            pl.BlockSpec((1, ATT_HEADS, qb, ATT_HD), lambda i: (i, 0, 0, 0)),
            pl.BlockSpec((qb, ATT_KV), lambda i: (i, 0)),
            pl.BlockSpec((qb, ATT_KV), lambda i: (i, 0)),
            pl.BlockSpec((qb, ATT_KV), lambda i: (i, 0)),
            pl.BlockSpec((qb, ATT_KV), lambda i: (i, 0)),
            pl.BlockSpec((1, IDX_HEADS, qb, IDX_HD), lambda i: (i, 0, 0, 0)),
            pl.BlockSpec((qb, IDX_HD), lambda i: (i, 0)),
            pl.BlockSpec((qb, IDX_HD), lambda i: (i, 0)),
        ],
        out_shape=[
            jax.ShapeDtypeStruct((T // qb, ATT_HEADS, qb, ATT_HD), BF16),
            jax.ShapeDtypeStruct((T, ATT_KV), F32),
            jax.ShapeDtypeStruct((T, ATT_KV), BF16),
            jax.ShapeDtypeStruct((T, ATT_KV), F32),
            jax.ShapeDtypeStruct((T, ATT_KV), BF16),
            jax.ShapeDtypeStruct((T // qb, IDX_HEADS, qb, IDX_HD), BF16),
            jax.ShapeDtypeStruct((T, IDX_HD), F32),
            jax.ShapeDtypeStruct((T, IDX_HD), BF16),
        ],
        compiler_params=_cparams(("parallel",)),
        name="rope",
    )(proj, proj, proj, proj, proj, cosf, sa, sb)
    return outs


def _dsa_kernel(iq_ref, sm_ref, q_ref, ik_ref, k_ref, v_ref, o_ref, sc_ref, *, qb, S, pos0, n_sel):
    j = pl.program_id(1)
    TK = TK_ATT
    q_last = pos0 + (j + 1) * qb - 1
    lim = jnp.minimum(((q_last // CHUNK) + 1) * CHUNK, S)
    nkt = (lim + TK - 1) // TK
    qpos = pos0 + j * qb + lax.broadcasted_iota(jnp.int32, (qb, 1), 0)
    qlim = jnp.minimum((jnp.right_shift(qpos, CHUNK.bit_length() - 1) + 1) * CHUNK, S)
    iw = sm_ref[:, SM_IW:SM_IW + IDX_HEADS] * ((IDX_HEADS ** -0.5) * (IDX_HD ** -0.5))
    HG = 4

    def score_tile(kt, carry):
        ks = pl.multiple_of(kt * TK, TK)
        ik_t = ik_ref[0, pl.ds(ks, TK), :]
        acc = jnp.zeros((qb, TK), F32)
        for hg in range(IDX_HEADS // HG):
            iq_g = iq_ref[0, hg * HG:(hg + 1) * HG].reshape(HG * qb, IDX_HD)
            r = jnp.maximum(_dot_nt(iq_g, ik_t), 0.0)
            for t in range(HG):
                hh = hg * HG + t
                acc = acc + iw[:, hh:hh + 1] * r[t * qb:(t + 1) * qb]
        kpos = ks + lax.broadcasted_iota(jnp.int32, (qb, TK), 1)
        sc_ref[kt] = jnp.where(kpos < qlim, acc, -jnp.inf)
        return carry

    lax.fori_loop(0, nkt, score_tile, 0)

    def lane_fold(m):
        p = m[:, 0:LANE]
        for t in range(1, TK // LANE):
            p = p + m[:, t * LANE:(t + 1) * LANE]
        return p

    def count_ge(x):
        def body(kt, c):
            return c + lane_fold(jnp.where(sc_ref[kt] >= x, 1.0, 0.0))
        part = lax.fori_loop(0, nkt, body, jnp.zeros((qb, LANE), F32))
        return jnp.sum(part, axis=1, keepdims=True)

    def minmax(kt, c):
        lo, hi = c
        t = sc_ref[kt]
        lo = jnp.minimum(lo, jnp.min(jnp.where(t > -jnp.inf, t, jnp.inf), axis=1, keepdims=True))
        hi = jnp.maximum(hi, jnp.max(t, axis=1, keepdims=True))
        return lo, hi

    lo0, hi0 = lax.fori_loop(0, nkt, minmax,
                             (jnp.full((qb, 1), jnp.inf, F32), jnp.full((qb, 1), -jnp.inf, F32)))
    kf = float(n_sel)
    cnt_all = count_ge(lo0)
    cnt_hi = count_ge(hi0)
    top_tied = cnt_hi >= kf
    fixed = (cnt_all <= kf) | top_tied
    lo_init = jnp.where(top_tied, hi0, lo0)

    def probe(lo, hi, clo, done):
        mid = lo + (hi - lo) * 0.5
        stop = (mid <= lo) | (mid >= hi) | (done > 0.0)
        c = count_ge(mid)
        up = jnp.logical_and(jnp.logical_not(stop), c >= kf)
        dn = jnp.logical_and(jnp.logical_not(stop), c < kf)
        done2 = jnp.where(stop | (c == kf), 1.0, 0.0)
        return jnp.where(up, mid, lo), jnp.where(dn, mid, hi), jnp.where(up, c, clo), done2

    def cond(st):
        return jnp.logical_and(st[4] > 0, st[5] < 128)

    def body(st):
        lo, hi, clo, done, _, it = st
        lo, hi, clo, done = probe(lo, hi, clo, done)
        lo, hi, clo, done = probe(lo, hi, clo, done)
        nact = jnp.sum(jnp.where(done > 0.0, 0, 1))
        return lo, hi, clo, done, nact, it + 1

    done0 = jnp.where(fixed, 1.0, 0.0)
    clo0 = jnp.where(top_tied, cnt_hi, cnt_all)
    thr, _, cthr, _, _, _ = lax.while_loop(cond, body, (lo_init, hi0, clo0, done0, jnp.int32(1), jnp.int32(0)))

    s_end = sc_ref.shape[0] * TK

    def count_tied(op, x):
        def body(kt, c):
            t = sc_ref[kt]
            if op == "gt":
                hit = t > thr
            else:
                kpos = kt * TK + lax.broadcasted_iota(jnp.int32, (qb, TK), 1)
                hit = jnp.logical_and(t == thr, kpos < x)
            return c + lane_fold(jnp.where(hit, 1.0, 0.0))
        part = lax.fori_loop(0, nkt, body, jnp.zeros((qb, LANE), F32))
        return jnp.sum(part, axis=1, keepdims=True)

    def tie_bound(_):
        need = kf - count_tied("gt", None)

        def step(_, st):
            lo_i, hi_i = st
            mid = jnp.right_shift(lo_i + hi_i, 1)
            ge = count_tied("eq", mid) >= need
            return jnp.where(ge, lo_i, mid + 1), jnp.where(ge, mid, hi_i)

        _, hi_i = lax.fori_loop(0, s_end.bit_length(), step,
                                (jnp.zeros((qb, 1), jnp.int32), jnp.full((qb, 1), s_end, jnp.int32)))
        return jnp.where(cthr > kf, hi_i, s_end)

    n_over = jnp.sum(jnp.where(cthr > kf, 1, 0))
    ibound = lax.cond(n_over > 0, tie_bound, lambda _: jnp.full((qb, 1), s_end, jnp.int32), 0)

    def to_bias(kt, carry):
        t = sc_ref[kt]
        kpos = kt * TK + lax.broadcasted_iota(jnp.int32, (qb, TK), 1)
        keep = (t > thr) | ((t == thr) & (kpos < ibound))
        sc_ref[kt] = jnp.where(keep, 0.0, NEG_BIG)
        return carry

    lax.fori_loop(0, nkt, to_bias, 0)

    R = ATT_REP
    for g in range(ATT_KV_HEADS):
        qg = q_ref[0, g * R:(g + 1) * R].reshape(R * qb, ATT_HD)

        def att_tile(kt, carry, g=g, qg=qg):
            m, l, acc = carry
            ks = pl.multiple_of(kt * TK, TK)
            k_t = k_ref[0, pl.ds(ks, TK), g * ATT_HD:(g + 1) * ATT_HD]
            v_t = v_ref[0, pl.ds(ks, TK), g * ATT_HD:(g + 1) * ATT_HD]
            s = _dot_nt(qg, k_t) * (ATT_HD ** -0.5)
            s = (s.reshape(R, qb, TK) + sc_ref[kt][None]).reshape(R * qb, TK)
            m_new = jnp.maximum(m, jnp.max(s, axis=1, keepdims=True))
            p = jnp.exp(s - m_new)
            a = jnp.exp(m - m_new)
            l = a * l + jnp.sum(p, axis=1, keepdims=True)
            acc = a * acc + _dot(p.astype(BF16), v_t)
            return m_new, l, acc

        def att_pair(i, carry, att_tile=att_tile):
            return att_tile(2 * i + 1, att_tile(2 * i, carry))

        m0 = jnp.full((R * qb, 1), NEG_BIG, F32)
        l0 = jnp.zeros((R * qb, 1), F32)
        a0 = jnp.zeros((R * qb, ATT_HD), F32)
        carry = lax.fori_loop(0, nkt // 2, att_pair, (m0, l0, a0))
        _, l, acc = lax.cond(nkt % 2 == 1, lambda c, att_tile=att_tile: att_tile(nkt - 1, c), lambda c: c, carry)
        out = acc / l
        for r in range(R):
            hh = g * R + r
            o_ref[:, hh * ATT_HD:(hh + 1) * ATT_HD] = out[r * qb:(r + 1) * qb].astype(o_ref.dtype)


def _dsa(iq_hm, proj, q_hm, ik_all, k_all, v_all, B, L, S, pos0):
    qb = min(Q_BLOCK, L)
    nq = L // qb
    S_pad = ik_all.shape[1]
    n_sel = min(TOPK_KEYS, S // 4)
    kern = functools.partial(_dsa_kernel, qb=qb, S=S, pos0=pos0, n_sel=n_sel)
    return pl.pallas_call(
        kern,
        grid=(B, nq),
        in_specs=[
            pl.BlockSpec((1, IDX_HEADS, qb, IDX_HD), lambda b, j: (b * nq + j, 0, 0, 0)),
            pl.BlockSpec((qb, LANE), lambda b, j: (b * nq + j, C_SM // LANE)),
            pl.BlockSpec((1, ATT_HEADS, qb, ATT_HD), lambda b, j: (b * nq + j, 0, 0, 0)),
            pl.BlockSpec((1, S_pad, IDX_HD), lambda b, j: (b, 0, 0)),
            pl.BlockSpec((1, S_pad, ATT_KV), lambda b, j: (b, 0, 0)),
            pl.BlockSpec((1, S_pad, ATT_KV), lambda b, j: (b, 0, 0)),
        ],
        out_specs=pl.BlockSpec((qb, ATT_Q), lambda b, j: (b * nq + j, 0)),
        out_shape=jax.ShapeDtypeStruct((B * L, ATT_Q), BF16),
        scratch_shapes=[pltpu.VMEM((S_pad // TK_ATT, qb, TK_ATT), F32)],
        compiler_params=_cparams(("parallel", "arbitrary")),
        name="dsa",
    )(iq_hm, proj, q_hm, ik_all, k_all, v_all)


def _merge_kernel(oa_ref, ob_ref, wa_ref, wb_ref, ga_ref, gb_ref, o_ref):
    ya = _dot(oa_ref[...], wa_ref[...])
    yb = _dot(ob_ref[...], wb_ref[...])
    o_ref[...] = (_sigmoid(ga_ref[...]) * ya + _sigmoid(gb_ref[...]) * yb).astype(o_ref.dtype)


def _merge(oa, ob, wa, wb, proj):
    T = oa.shape[0]
    tm = min(512, T)
    tn = TN_MIX
    return pl.pallas_call(
        _merge_kernel,
        grid=(T // tm, D_MODEL // tn),
        in_specs=[
            pl.BlockSpec((tm, GDN_VAL), lambda i, j: (i, 0)),
            pl.BlockSpec((tm, ATT_Q), lambda i, j: (i, 0)),
            pl.BlockSpec((GDN_VAL, tn), lambda i, j: (0, j)),
            pl.BlockSpec((ATT_Q, tn), lambda i, j: (0, j)),
            pl.BlockSpec((tm, tn), lambda i, j: (i, C_GA // tn + j)),
            pl.BlockSpec((tm, tn), lambda i, j: (i, C_GB // tn + j)),
        ],
        out_specs=pl.BlockSpec((tm, tn), lambda i, j: (i, j)),
        out_shape=jax.ShapeDtypeStruct((T, D_MODEL), BF16),
        compiler_params=_cparams(("parallel", "arbitrary")),
        name="merge",
    )(oa, ob, wa, wb, proj, proj)


def _oproj_kernel(m_ref, w_ref, x_ref, g1_ref, lg_ref, lb_ref, x1_ref, acc_ref, *, nj):
    j = pl.program_id(1)
    acc_ref[j] = _dot(m_ref[...], w_ref[...])

    @pl.when(j == nj - 1)
    def _():
        bb, tl, D = x_ref.shape
        tn = acc_ref.shape[2]
        sl = lambda t: slice(t * tn, (t + 1) * tn)
        s1 = jnp.zeros((bb, tl, 1), F32)
        for t in range(nj):
            v = ALPHA * x_ref[:, :, sl(t)] + g1_ref[:, :, sl(t)] * acc_ref[t].reshape(bb, tl, tn)
            acc_ref[t] = v.reshape(bb * tl, tn)
            s1 = s1 + jnp.sum(v, -1, keepdims=True)
        mu = s1 * (1.0 / D)
        s2 = jnp.zeros((bb, tl, 1), F32)
        for t in range(nj):
            d = acc_ref[t].reshape(bb, tl, tn) - mu
            s2 = s2 + jnp.sum(d * d, -1, keepdims=True)
        rstd = lax.rsqrt(s2 * (1.0 / D) + LN_EPS)
        for t in range(nj):
            x1_ref[:, :, sl(t)] = ((acc_ref[t].reshape(bb, tl, tn) - mu) * rstd * lg_ref[:, :, sl(t)]
                                   + lb_ref[:, :, sl(t)])


def _oproj(m, w_o, x3, mod3, row0, ln_g, ln_b):
    B, L, D = x3.shape
    bb, tl = _row_tiling(B, L, TM_OPROJ)
    nl = L // tl
    rows = bb * tl
    mrow = row0 // bb
    tn = TN_MIX
    nj = D // tn
    in_specs = [
        pl.BlockSpec((rows, D), lambda i, j: (i, 0), pipeline_mode=pl.Buffered(1)),
        pl.BlockSpec((D, tn), lambda i, j: (0, j)),
        pl.BlockSpec((bb, tl, D), lambda i, j: (i // nl, i % nl, 0), pipeline_mode=pl.Buffered(1)),
        pl.BlockSpec((bb, 1, D), lambda i, j: (mrow + i // nl, 0, 2)),
        pl.BlockSpec((1, 1, D), lambda i, j: (0, 0, 0)),
        pl.BlockSpec((1, 1, D), lambda i, j: (0, 0, 0)),
    ]
    args = [m, w_o, x3, mod3, ln_g.reshape(1, 1, D), ln_b.reshape(1, 1, D)]
    return pl.pallas_call(
        functools.partial(_oproj_kernel, nj=nj),
        grid=((B // bb) * nl, nj),
        in_specs=in_specs,
        out_specs=pl.BlockSpec((bb, tl, D), lambda i, j: (i // nl, i % nl, 0)),
        out_shape=jax.ShapeDtypeStruct((B, L, D), F32),
        scratch_shapes=[pltpu.VMEM((nj, rows, tn), F32)],
        compiler_params=_cparams(("parallel", "arbitrary")),
        name="oproj_ln1",
    )(*args)


def _router_kernel(x1_ref, sc2_ref, sh2_ref, w_ref, b_ref, h2_in, id_in, wt_in, h2_ref, id_ref, wt_ref):
    del h2_in, id_in, wt_in
    h2 = (x1_ref[...] * (1.0 + sc2_ref[...]) + sh2_ref[...]).reshape(h2_ref.shape)
    h2_ref[...] = h2
    logits = jnp.dot(h2, w_ref[...], precision=lax.Precision.HIGHEST, preferred_element_type=F32) + b_ref[...]
    rows = logits.shape[0]
    lane = lax.broadcasted_iota(jnp.int32, (rows, LANE), 1)
    is_g = lane < N_GROUPS
    gl = jnp.where(is_g, logits, -jnp.inf)
    gmax = jnp.max(gl, axis=1, keepdims=True)
    g_sel = jnp.min(jnp.where(gl == gmax, lane, LANE), axis=1, keepdims=True)
    p_grp = 1.0 / jnp.sum(jnp.where(is_g, jnp.exp(gl - gmax), 0.0), axis=1, keepdims=True)
    e_lo = N_GROUPS + g_sel * EXP_PER_GROUP
    in_g = (lane >= e_lo) & (lane < e_lo + EXP_PER_GROUP)
    el = jnp.where(in_g, logits, -jnp.inf)
    m1 = jnp.max(el, axis=1, keepdims=True)
    i1 = jnp.min(jnp.where(el == m1, lane, LANE), axis=1, keepdims=True)
    el2 = jnp.where(lane == i1, -jnp.inf, el)
    m2 = jnp.max(el2, axis=1, keepdims=True)
    i2 = jnp.min(jnp.where(el2 == m2, lane, LANE), axis=1, keepdims=True)
    e21 = jnp.exp(m2 - m1)
    w1 = p_grp / (1.0 + e21)
    w2 = p_grp * e21 / (1.0 + e21)
    id_ref[...] = jnp.where(lane == 0, i1 - N_GROUPS, jnp.where(lane == 1, i2 - N_GROUPS, 0))
    wt_ref[...] = jnp.where(lane == 0, w1, jnp.where(lane == 1, w2, 0.0))


def _router(x1, mod3, row0, w_r, b_r, bufs, t_off):
    B, L, D = x1.shape
    bb, tl = _row_tiling(B, L, TM_MIX)
    nl = L // tl
    rows = bb * tl
    mrow = row0 // bb
    assert t_off % rows == 0
    boff = t_off // rows
    t_all = bufs[0].shape[0]
    modspec = lambda c: pl.BlockSpec((bb, 1, D), lambda i: (mrow + i // nl, 0, c))
    anyspec = pl.BlockSpec(memory_space=pl.ANY)
    return pl.pallas_call(
        _router_kernel,
        grid=((B // bb) * nl,),
        in_specs=[
            pl.BlockSpec((bb, tl, D), lambda i: (i // nl, i % nl, 0)),
            modspec(4), modspec(3),
            pl.BlockSpec((D, LANE), lambda i: (0, 0)),
            pl.BlockSpec((1, LANE), lambda i: (0, 0)),
            anyspec, anyspec, anyspec,
        ],
        out_specs=[
            pl.BlockSpec((rows, D), lambda i: (boff + i, 0)),
            pl.BlockSpec((rows, LANE), lambda i: (boff + i, 0)),
            pl.BlockSpec((rows, LANE), lambda i: (boff + i, 0)),
        ],
        out_shape=[
            jax.ShapeDtypeStruct((t_all, D), F32),
            jax.ShapeDtypeStruct((t_all, LANE), jnp.int32),
            jax.ShapeDtypeStruct((t_all, LANE), F32),
        ],
        input_output_aliases={5: 0, 6: 1, 7: 2},
        compiler_params=_cparams(("parallel",)),
        name="router",
    )(x1, mod3, mod3, w_r, b_r, *bufs)


def _expert_kernel(blk_e_ref, tok_ref, nused_ref, h_hbm, wg_ref, wu_ref, wd_ref, o_ref,
                   xbuf_ref, xb16_ref, act_ref, sem_ref, *, BM, nfc):
    i = pl.program_id(0)
    c = pl.program_id(1)
    nused = nused_ref[0]
    slot = lax.rem(i, 2)

    def start_gather(blk, sl):
        def body(r, carry):
            tok = tok_ref[blk * BM + r]
            pltpu.make_async_copy(h_hbm.at[pl.ds(tok, 1)], xbuf_ref.at[sl, pl.ds(r, 1)], sem_ref.at[sl]).start()
            return carry
        lax.fori_loop(0, BM, body, 0, unroll=8)

    def wait_gather(sl):
        pltpu.make_async_copy(xbuf_ref.at[sl], xbuf_ref.at[sl], sem_ref.at[sl]).wait()

    @pl.when(jnp.logical_and(c == 0, i < nused))
    def _():
        @pl.when(i == 0)
        def _():
            start_gather(0, 0)

        @pl.when(i + 1 < nused)
        def _():
            start_gather(i + 1, 1 - slot)

        wait_gather(slot)
        xb16_ref[...] = xbuf_ref[slot].astype(BF16)

    @pl.when(i < nused)
    def _():
        x = xb16_ref[...]
        gate = _dot(x, wg_ref[0])
        up = _dot(x, wu_ref[0])
        act_ref[c] = (_silu(gate) * up).astype(BF16)

        @pl.when(c == nfc - 1)
        def _():
            act = jnp.concatenate([act_ref[t] for t in range(nfc)], axis=-1)
            o_ref[...] = _dot(act, wd_ref[0])

    @pl.when(jnp.logical_and(i >= nused, c == nfc - 1))
    def _():
        o_ref[...] = jnp.zeros_like(o_ref)


def _experts(h2, blk_e, slot_tok, nused, w_up16, w_down16, BM):
    T, D = h2.shape
    P = slot_tok.shape[0]
    nblk = P // BM
    fc = FC_MOE
    nfc = D_EXPERT // fc
    kern = functools.partial(_expert_kernel, BM=BM, nfc=nfc)

    def eidx(i, nu):
        return jnp.minimum(i, nu[0] - 1)

    def cidx(i, c, nu):
        return jnp.where(i < nu[0], c, nfc - 1)

    grid_spec = pltpu.PrefetchScalarGridSpec(
        num_scalar_prefetch=3,
        grid=(nblk, nfc),
        in_specs=[
            pl.BlockSpec(memory_space=pl.ANY),
            pl.BlockSpec((1, D, fc), lambda i, c, be, tk, nu: (be[eidx(i, nu)], 0, cidx(i, c, nu))),
            pl.BlockSpec((1, D, fc), lambda i, c, be, tk, nu: (be[eidx(i, nu)], 0, nfc + cidx(i, c, nu))),
            pl.BlockSpec((1, D_EXPERT, D), lambda i, c, be, tk, nu: (be[eidx(i, nu)], 0, 0)),
        ],
        out_specs=pl.BlockSpec((BM, D), lambda i, c, be, tk, nu: (i, 0)),
        scratch_shapes=[
            pltpu.VMEM((2, BM, D), F32),
            pltpu.VMEM((BM, D), BF16),
            pltpu.VMEM((nfc, BM, fc), BF16),
            pltpu.SemaphoreType.DMA((2,)),
        ],
    )
    return pl.pallas_call(
        kern,
        grid_spec=grid_spec,
        out_shape=jax.ShapeDtypeStruct((P, D), F32),
        compiler_params=_cparams(("arbitrary", "arbitrary")),
        name="experts",
    )(blk_e, slot_tok, nused, h2, w_up16, w_up16, w_down16)


def _route_metadata(ids, T, BM):
    expert = ids[:, 0:2].reshape(-1)
    A = 2 * T
    onehot = (expert[:, None] == jnp.arange(N_EXPERTS, dtype=jnp.int32)[None, :]).astype(jnp.int32)
    csum = jnp.cumsum(onehot, axis=0)
    counts = csum[-1]
    rank = jnp.sum((csum - onehot) * onehot, axis=1)
    padded = ((counts + BM - 1) // BM) * BM
    pend = jnp.cumsum(padded)
    pstart = pend - padded
    dest = (pstart[expert] + rank).astype(jnp.int32)
    P = ((A + BM - 1) // BM) * BM + N_EXPERTS * BM
    nblk = P // BM
    tok = jnp.arange(A, dtype=jnp.int32) // 2
    slot_tok = jnp.zeros((P,), jnp.int32).at[dest].set(tok)
    starts = jnp.arange(nblk, dtype=jnp.int32) * BM
    blk_e = jnp.minimum(jnp.sum((pend[None, :] <= starts[:, None]).astype(jnp.int32), axis=1), N_EXPERTS - 1)
    nused = (pend[-1] // BM).astype(jnp.int32).reshape(1)
    return dest, slot_tok, blk_e, nused


def _final_kernel(dest_ref, yb_hbm, x1_ref, g2_ref, wt_ref, lg_ref, lb_ref, o_ref, buf_ref, sem_ref, *, rows, ntiles):
    i = pl.program_id(0)
    slot = lax.rem(i, 2)

    def start_gather(tile, sl):
        def body(r, carry):
            d = dest_ref[tile * 2 * rows + r]
            pltpu.make_async_copy(yb_hbm.at[pl.ds(d, 1)], buf_ref.at[sl, pl.ds(r, 1)], sem_ref.at[sl]).start()
            return carry
        lax.fori_loop(0, 2 * rows, body, 0, unroll=8)

    def wait_gather(sl):
        pltpu.make_async_copy(buf_ref.at[sl], buf_ref.at[sl], sem_ref.at[sl]).wait()

    @pl.when(i == 0)
    def _():
        start_gather(0, 0)

    @pl.when(i + 1 < ntiles)
    def _():
        start_gather(i + 1, 1 - slot)

    wait_gather(slot)
    bb, tl, D = x1_ref.shape
    tn = TN_MIX
    sl = lambda t: slice(t * tn, (t + 1) * tn)
    w0 = wt_ref[:, 0:1]
    w1 = wt_ref[:, 1:2]
    s1 = jnp.zeros((bb, tl, 1), F32)
    for t in range(D // tn):
        f = w0 * buf_ref[slot, 0:rows, sl(t)] + w1 * buf_ref[slot, rows:2 * rows, sl(t)]
        v = ALPHA * x1_ref[:, :, sl(t)] + g2_ref[:, :, sl(t)] * f.reshape(bb, tl, tn)
        o_ref[:, :, sl(t)] = v
        s1 = s1 + jnp.sum(v, -1, keepdims=True)
    mu = s1 * (1.0 / D)
    s2 = jnp.zeros((bb, tl, 1), F32)
    for t in range(D // tn):
        d = o_ref[:, :, sl(t)] - mu
        s2 = s2 + jnp.sum(d * d, -1, keepdims=True)
    rstd = lax.rsqrt(s2 * (1.0 / D) + LN_EPS)
    for t in range(D // tn):
        o_ref[:, :, sl(t)] = (o_ref[:, :, sl(t)] - mu) * rstd * lg_ref[:, :, sl(t)] + lb_ref[:, :, sl(t)]


def _final(dest, yb, x1, mod3, row0, wts, t_off, ln_g, ln_b):
    B, L, D = x1.shape
    T = B * L
    bb, tl = _row_tiling(B, L, TM_MIX)
    nl = L // tl
    rows = bb * tl
    mrow = row0 // bb
    ntiles = (B // bb) * nl
    woff = t_off // rows
    dest_tiles = dest[2 * t_off:2 * (t_off + T)].reshape(T // rows, rows, 2).transpose(0, 2, 1).reshape(-1)
    kern = functools.partial(_final_kernel, rows=rows, ntiles=ntiles)
    grid_spec = pltpu.PrefetchScalarGridSpec(
        num_scalar_prefetch=1,
        grid=(ntiles,),
        in_specs=[
            pl.BlockSpec(memory_space=pl.ANY),
            pl.BlockSpec((bb, tl, D), lambda i, d: (i // nl, i % nl, 0)),
            pl.BlockSpec((bb, 1, D), lambda i, d: (mrow + i // nl, 0, 5)),
            pl.BlockSpec((rows, LANE), lambda i, d: (woff + i, 0)),
            pl.BlockSpec((1, 1, D), lambda i, d: (0, 0, 0)),
            pl.BlockSpec((1, 1, D), lambda i, d: (0, 0, 0)),
        ],
        out_specs=pl.BlockSpec((bb, tl, D), lambda i, d: (i // nl, i % nl, 0)),
        scratch_shapes=[
            pltpu.VMEM((2, 2 * rows, D), F32),
            pltpu.SemaphoreType.DMA((2,)),
        ],
    )
    return pl.pallas_call(
        kern,
        grid_spec=grid_spec,
        out_shape=jax.ShapeDtypeStruct((B, L, D), F32),
        compiler_params=_cparams(("arbitrary",)),
        name="combine_ln2",
    )(dest_tiles, yb, x1, mod3, wts, ln_g.reshape(1, 1, D), ln_b.reshape(1, 1, D))


def _pad_keys(t, S_pad):
    S = t.shape[1]
    if S == S_pad:
        return t
    return jnp.pad(t, ((0, 0), (0, S_pad - S), (0, 0)))


def _mixers(x3, mod3, row0, conv_state, gdn_state, past_k, past_v, past_ik, wp):
    B, L, D = x3.shape
    pos0 = past_k.shape[1]
    S = pos0 + L
    proj = _inproj(x3, mod3, row0, wp["w_in"])

    oa, new_gdn = _gdn(proj, B, L, conv_state, gdn_state, wp["conv_w"], wp["a_log"], wp["dt_bias"],
                       wp["gdn_norm_w"])
    new_conv = proj.reshape(B, L, NP)[:, L - (CONV_W - 1):, C_QKV:C_QKV + GDN_CONV_CH]

    qb = min(Q_BLOCK, L)
    q_hm, kf, k16, vf, v16, iq_hm, ikf, ik16 = _rope(proj, B, L, pos0, qb)
    S_pad = ((S + TK_ATT - 1) // TK_ATT) * TK_ATT
    k_all = _pad_keys(jnp.concatenate([past_k.reshape(B, pos0, ATT_KV).astype(BF16), k16.reshape(B, L, ATT_KV)], 1), S_pad)
    v_all = _pad_keys(jnp.concatenate([past_v.reshape(B, pos0, ATT_KV).astype(BF16), v16.reshape(B, L, ATT_KV)], 1), S_pad)
    ik_all = _pad_keys(jnp.concatenate([past_ik.astype(BF16), ik16.reshape(B, L, IDX_HD)], 1), S_pad)
    ob = _dsa(iq_hm, proj, q_hm, ik_all, k_all, v_all, B, L, S, pos0)

    m = _merge(oa, ob, wp["w_br_a"], wp["w_br_b"], proj)
    x1 = _oproj(m, wp["w_o"], x3, mod3, row0, wp["ln1_g"], wp["ln1_b"])

    kb_out = kf.reshape(B, L, ATT_KV_HEADS, ATT_HD)
    vb_out = vf.reshape(B, L, ATT_KV_HEADS, ATT_HD)
    ik_out = ikf.reshape(B, L, IDX_HD)
    return x1, (new_conv, new_gdn, kb_out, vb_out, ik_out)


WP_TN = 512
WP_TR = 1024
WP_NSRC = WP_TN // LANE + 1


def _w_in_source_columns():
    sizes = (GDN_CONV_CH, GDN_HEADS, GDN_HEADS, GDN_VAL, ATT_Q, ATT_KV, ATT_KV, IDX_HEADS * IDX_HD, IDX_HD,
             IDX_HEADS, 2 * D_MODEL)
    offs, off = [], 0
    for s in sizes:
        offs.append(off)
        off += s
    o_qkv, o_a, o_b, o_z, o_q, o_k, o_v, o_iq, o_ik, o_iw, o_g = offs
    srcs = []
    for dst, src, width in ((C_QKV, o_qkv, GDN_CONV_CH), (C_Z, o_z, GDN_VAL), (C_IQ, o_iq, IDX_HEADS * IDX_HD),
                            (C_Q, o_q, ATT_Q), (C_K, o_k, ATT_KV), (C_V, o_v, ATT_KV), (C_IK, -1, WP_TN),
                            (C_GA, o_g, 2 * D_MODEL)):
        assert dst == len(srcs) * WP_TN and width % WP_TN == 0
        srcs += [src + t * WP_TN if src >= 0 else -1 for t in range(width // WP_TN)]
    assert len(srcs) * WP_TN == NP
    return srcs, (o_a, o_b, o_ik, o_iw)


def _narrow_pieces():
    o_a, o_b, o_ik, o_iw = _W_IN_NARROW
    return ((o_ik, IDX_HD), (o_a, GDN_HEADS), (o_b, GDN_HEADS), (o_iw, IDX_HEADS))


def _narrow_blocks():
    blks = set()
    for o, w in _narrow_pieces():
        blks.update(range(o // LANE, (o + w - 1) // LANE + 1))
    return sorted(blks)


def _wperm_kernel(base_ref, shift_ref, *refs):
    src_refs = refs[:WP_NSRC]
    nb = dict(zip(_narrow_blocks(), refs[WP_NSRC:-1]))
    o_ref = refs[-1]
    shift = shift_ref[pl.program_id(0)]

    @pl.when(shift < 0)
    def _():
        def piece(o, w):
            blk, ln = o // LANE, o % LANE
            if ln + w <= LANE:
                return nb[blk][:, ln:ln + w]
            return jnp.concatenate([nb[blk][:, ln:], nb[blk + 1][:, :ln + w - LANE]], axis=1)

        parts = [piece(o, w) for o, w in _narrow_pieces()]
        used = sum(w for _, w in _narrow_pieces())
        parts.append(jnp.zeros((o_ref.shape[0], WP_TN - used), o_ref.dtype))
        o_ref[...] = jnp.concatenate(parts, axis=1)

    for sh in sorted(set(s for s in _W_IN_SHIFTS if s >= 0)):
        @pl.when(shift == sh)
        def _(sh=sh):
            a = jnp.concatenate([r[...] for r in src_refs], axis=1)
            o_ref[...] = a[:, sh:sh + WP_TN]


_W_IN_SRCS, _W_IN_NARROW = _w_in_source_columns()
_W_IN_SHIFTS = [s % LANE if s >= 0 else -1 for s in _W_IN_SRCS]


def _permute_w_in(w_in):
    D, ncols = w_in.shape
    base = jnp.asarray([max(s, 0) // LANE for s in _W_IN_SRCS], jnp.int32)
    shift = jnp.asarray(_W_IN_SHIFTS, jnp.int32)
    last = (ncols - 1) // LANE

    def src_spec(t):
        return pl.BlockSpec((WP_TR, LANE), lambda j, r, base, shift: (r, jnp.minimum(base[j] + t, last)))

    def narrow_spec(blk):
        return pl.BlockSpec((WP_TR, LANE), lambda j, r, base, shift: (r, blk))

    nblks = _narrow_blocks()
    grid_spec = pltpu.PrefetchScalarGridSpec(
        num_scalar_prefetch=2,
        grid=(NP // WP_TN, D // WP_TR),
        in_specs=[src_spec(t) for t in range(WP_NSRC)] + [narrow_spec(blk) for blk in nblks],
        out_specs=pl.BlockSpec((WP_TR, WP_TN), lambda j, r, base, shift: (r, j)),
    )
    return pl.pallas_call(
        _wperm_kernel,
        grid_spec=grid_spec,
        out_shape=jax.ShapeDtypeStruct((D, NP), BF16),
        compiler_params=_cparams(("parallel", "parallel")),
        name="w_in_permute",
    )(base, shift, *([w_in.astype(BF16)] * (WP_NSRC + len(nblks))))


def kernel(x_prompt, x_sample, c_prompt, c_sample, state_conv, state_gdn, cache_k, cache_v, cache_idx_k,
           w_ada, b_ada, w_in, conv_w, a_log, dt_bias, gdn_norm_w, w_br_a, w_br_b, w_o,
           ln1_g, ln1_b, ln2_g, ln2_b, w_grp, b_grp, w_rtr, b_rtr, w_up, w_down):
    Bp, Lp, D = x_prompt.shape
    Bs, Ls, _ = x_sample.shape
    l = 0
    row0_p = Bs
    nrows = ((Bs + Bp + 7) // 8) * 8
    c_all = jnp.concatenate([c_sample, c_prompt, jnp.zeros((nrows - Bs - Bp, D), F32)], 0)
    mod = _ada(c_all, w_ada[l], b_ada[l])
    mod3 = mod.reshape(nrows, 1, 6 * D)

    nr = LANE - N_GROUPS - N_EXPERTS
    wp = dict(
        w_in=_permute_w_in(w_in[l]),
        conv_w=conv_w[l], a_log=a_log[l], dt_bias=dt_bias[l], gdn_norm_w=gdn_norm_w[l],
        w_br_a=w_br_a[l].astype(BF16), w_br_b=w_br_b[l].astype(BF16), w_o=w_o[l].astype(BF16),
        ln1_g=ln1_g[l], ln1_b=ln1_b[l], ln2_g=ln2_g[l], ln2_b=ln2_b[l],
        w_r=jnp.concatenate([w_grp[l], w_rtr[l], jnp.zeros((D, nr), F32)], 1),
        b_r=jnp.concatenate([b_grp[l], b_rtr[l], jnp.zeros((nr,), F32)]).reshape(1, LANE),
        w_up=w_up[l].astype(BF16), w_down=w_down[l].astype(BF16),
    )

    zc = jnp.zeros((Bp, CONV_W - 1, GDN_CONV_CH), F32)
    zs = jnp.zeros((Bp, GDN_HEADS, GDN_DK, GDN_DV), F32)
    zk = jnp.zeros((Bp, 0, ATT_KV_HEADS, ATT_HD), F32)
    zik = jnp.zeros((Bp, 0, IDX_HD), F32)
    Tp, Ts = Bp * Lp, Bs * Ls
    t_all = Tp + Ts
    x1p, (c1, g1, k1, v1, i1) = _mixers(x_prompt, mod3, row0_p, zc, zs, zk, zk, zik, wp)
    x1s, (c2, g2, k2, v2, i2) = _mixers(x_sample, mod3, 0, state_conv[l], state_gdn[l], cache_k[l],
                                        cache_v[l], cache_idx_k[l], wp)

    bufs = (jnp.zeros((t_all, D), F32), jnp.zeros((t_all, LANE), jnp.int32), jnp.zeros((t_all, LANE), F32))
    bufs = _router(x1p, mod3, row0_p, wp["w_r"], wp["b_r"], bufs, 0)
    h2_all, ids, wts = _router(x1s, mod3, 0, wp["w_r"], wp["b_r"], bufs, Tp)
    dest, slot_tok, blk_e, nused = _route_metadata(ids, t_all, MOE_BM)
    yb = _experts(h2_all, blk_e, slot_tok, nused, wp["w_up"], wp["w_down"], MOE_BM)
    yp = _final(dest, yb, x1p, mod3, row0_p, wts, 0, wp["ln2_g"], wp["ln2_b"])
    ys = _final(dest, yb, x1s, mod3, 0, wts, Tp, wp["ln2_g"], wp["ln2_b"])
    st = lambda t: t[None]
    return (yp, ys, st(c1), st(g1), st(k1), st(v1), st(i1), st(c2), st(g2), st(k2), st(v2), st(i2))
```

```python
import functools

import jax
import jax.numpy as jnp
from jax import lax
from jax.experimental import pallas as pl
from jax.experimental.pallas import tpu as pltpu

F32 = jnp.float32
BF16 = jnp.bfloat16

D_MODEL = 4096
CHUNK = 64
GDN_HEADS = D_MODEL // 256
GDN_DK = 128
GDN_DV = 128
GDN_KEY = GDN_HEADS * GDN_DK
GDN_VAL = GDN_HEADS * GDN_DV
GDN_CONV_CH = 2 * GDN_KEY + GDN_VAL
CONV_W = 4
ATT_HEADS = D_MODEL // 256
ATT_KV_HEADS = ATT_HEADS // 4
ATT_REP = ATT_HEADS // ATT_KV_HEADS
ATT_HD = 128
ATT_Q = ATT_HEADS * ATT_HD
ATT_KV = ATT_KV_HEADS * ATT_HD
IDX_HEADS = D_MODEL // 128
IDX_HD = 128
TOPK_KEYS = 256
Q_BLOCK = 128
ROPE_THETA = 500000.0
ROPE_ROT = ATT_HD // 4
ROPE_HALF = ROPE_ROT // 2
N_GROUPS = 4
EXP_PER_GROUP = 8
N_EXPERTS = N_GROUPS * EXP_PER_GROUP
D_EXPERT = D_MODEL // 4
DEPTH = 1
ALPHA = (2.0 * DEPTH) ** 0.25
LN_EPS = 1e-5
RMS_EPS = 1e-6

C_QKV = 0
C_Z = C_QKV + GDN_CONV_CH
C_IQ = C_Z + GDN_VAL
C_Q = C_IQ + IDX_HEADS * IDX_HD
C_K = C_Q + ATT_Q
C_V = C_K + ATT_KV
C_IK = C_V + ATT_KV
C_SM = C_IK + IDX_HD
SM_W = 384
C_GA = C_SM + SM_W
C_GB = C_GA + D_MODEL
NP = C_GB + D_MODEL
SM_A, SM_B, SM_IW = 0, GDN_HEADS, 2 * GDN_HEADS

LANE = 128
V7X_VMEM_BYTES = 64 * 1024 * 1024
VMEM_LIMIT = V7X_VMEM_BYTES * 7 // 8
NEG_BIG = -1e30

TN_IN = 512
TM_IN = 1024
TM_MIX = 256
TM_OPROJ = 512
TM_MERGE = 1024
TN_MIX = 512
TK_ATT = 512
FC_MOE = 256
MOE_BM = 256


def _cparams(sem):
    return pltpu.CompilerParams(dimension_semantics=sem, vmem_limit_bytes=VMEM_LIMIT)


def _dot(a, b):
    return jnp.dot(a, b, preferred_element_type=F32)


def _dot_nt(a, b):
    return lax.dot_general(a, b, (((1,), (1,)), ((), ())), preferred_element_type=F32)


def _dot_tn(a, b):
    return lax.dot_general(a, b, (((0,), (0,)), ((), ())), preferred_element_type=F32)


def _split_bf16(a):
    hi = a.astype(BF16)
    lo = (a - hi.astype(F32)).astype(BF16)
    return hi, lo


def _mm3(a, b, dot=_dot):
    ah, al = _split_bf16(a)
    bh, bl = _split_bf16(b)
    return dot(ah, bh) + (dot(ah, bl) + dot(al, bh))


def _mm1(a, b):
    return _dot(a.astype(BF16), b.astype(BF16))


def _mm1_nt(a, b):
    return _dot_nt(a.astype(BF16), b.astype(BF16))


def _sigmoid(x):
    return 1.0 / (1.0 + jnp.exp(-x))


def _silu(x):
    return x * _sigmoid(x)


def _softplus(x):
    return jnp.maximum(x, 0.0) + jnp.log(1.0 + jnp.exp(-jnp.abs(x)))


def _ada_kernel(c_ref, w_ref, b_ref, o_ref):
    s = _silu(c_ref[...])
    o_ref[...] = _dot(s.astype(BF16), w_ref[...].astype(BF16)) + b_ref[...]


def _ada(c_all, w_ada, b_ada):
    R, D = c_all.shape
    N = w_ada.shape[1]
    tn = 512
    return pl.pallas_call(
        _ada_kernel,
        grid=(N // tn,),
        in_specs=[
            pl.BlockSpec((R, D), lambda j: (0, 0)),
            pl.BlockSpec((D, tn), lambda j: (0, j)),
            pl.BlockSpec((1, tn), lambda j: (0, j)),
        ],
        out_specs=pl.BlockSpec((R, tn), lambda j: (0, j)),
        out_shape=jax.ShapeDtypeStruct((R, N), F32),
        compiler_params=_cparams(("parallel",)),
        name="ada",
    )(c_all, w_ada, b_ada.reshape(1, N))


def _inproj_kernel(x_ref, sc_ref, sh_ref, w_ref, o_ref, h_ref):
    @pl.when(pl.program_id(1) == 0)
    def _():
        h = x_ref[...] * (1.0 + sc_ref[...]) + sh_ref[...]
        h_ref[...] = h.reshape(h_ref.shape).astype(BF16)

    o_ref[...] = _dot(h_ref[...], w_ref[...])


def _row_tiling(B, L, tm):
    if L >= tm:
        return 1, tm
    bb = max(1, min(B, tm // L))
    while B % bb:
        bb -= 1
    return bb, L


def _inproj(x3, mod3, row0, w_perm):
    B, L, D = x3.shape
    bb, tl = _row_tiling(B, L, TM_IN)
    nl = L // tl
    rows = bb * tl
    mrow = row0 // bb
    return pl.pallas_call(
        _inproj_kernel,
        grid=((B // bb) * nl, NP // TN_IN),
        in_specs=[
            pl.BlockSpec((bb, tl, D), lambda i, j: (i // nl, i % nl, 0), pipeline_mode=pl.Buffered(1)),
            pl.BlockSpec((bb, 1, D), lambda i, j: (mrow + i // nl, 0, 1)),
            pl.BlockSpec((bb, 1, D), lambda i, j: (mrow + i // nl, 0, 0)),
            pl.BlockSpec((D, TN_IN), lambda i, j: (0, j)),
        ],
        out_specs=pl.BlockSpec((rows, TN_IN), lambda i, j: (i, j)),
        out_shape=jax.ShapeDtypeStruct((B * L, NP), F32),
        scratch_shapes=[pltpu.VMEM((rows, D), BF16)],
        compiler_params=_cparams(("parallel", "arbitrary")),
        name="inproj",
    )(x3, mod3, mod3, w_perm)


GDN_GROUP = 4


def _split3_bf16(a):
    hi = a.astype(BF16)
    r = a - hi.astype(F32)
    mid = r.astype(BF16)
    lo = (r - mid.astype(F32)).astype(BF16)
    return hi, mid, lo


def _dot_exact01(a, b01, a_is_01=False):
    if a_is_01:
        h, m, l = _split3_bf16(b01)
        return _dot(a, h) + (_dot(a, m) + _dot(a, l))
    h, m, l = _split3_bf16(a)
    return _dot(h, b01) + (_dot(m, b01) + _dot(l, b01))


def _mm3p(a_hl, b_hl, dot=_dot):
    (ah, al), (bh, bl) = a_hl, b_hl
    return dot(ah, bh) + (dot(ah, bl) + dot(al, bh))


def _block_rows_hl(x_hl, nblk, mask01):
    return tuple(jnp.concatenate([p] * nblk, axis=0) * mask01 for p in x_hl)


def _gdn_kernel(qkv_ref, z_ref, sm_ref, cst_ref, cw_ref, alr_ref, dtr_ref, nw_ref, s0_ref,
                o_ref, sfin_ref, s_ref, ext_ref, mt_ref, mk_ref, mx_ref, ml_ref, *, C, nchunks):
    n = pl.program_id(1)
    H = GDN_HEADS
    G = GDN_GROUP
    NG = H // G
    GC = G * C
    PADR = 8
    logc = C.bit_length() - 1

    @pl.when(n == 0)
    def _():
        s_ref[...] = s0_ref[0]
        ext_ref[PADR - (CONV_W - 1):PADR, :] = cst_ref[0]

    ext_ref[PADR:PADR + C, :] = qkv_ref[...]
    cw = cw_ref[...]
    y = ext_ref[PADR - 3:PADR - 3 + C, :] * cw[0:1]
    for jw in range(1, CONV_W):
        y = y + ext_ref[PADR - 3 + jw:PADR - 3 + jw + C, :] * cw[jw:jw + 1]
    y = _silu(y)
    hist = ext_ref[PADR + C - (CONV_W - 1):PADR + C, :]
    ext_ref[PADR - (CONV_W - 1):PADR, :] = hist

    sm = sm_ref[...]
    g_col = -jnp.exp(alr_ref[...]) * _softplus(sm[:, SM_A:SM_A + H] + dtr_ref[...])
    beta_col = _sigmoid(sm[:, SM_B:SM_B + H])

    ii = lax.broadcasted_iota(jnp.int32, (C, C), 0)
    jj = lax.broadcasted_iota(jnp.int32, (C, C), 1)
    tril01 = jnp.where(jj <= ii, 1.0, 0.0).astype(BF16)
    gc_col = _dot_exact01(tril01, g_col, a_is_01=True)
    egc_col = jnp.exp(gc_col)
    gc_last = gc_col[C - 1:C, :]
    ekd_col = jnp.exp(gc_last - gc_col)
    egl = jnp.exp(gc_last)

    W = H * C
    hrow = lax.broadcasted_iota(jnp.int32, (H, W), 0)
    hlane = jnp.right_shift(lax.broadcasted_iota(jnp.int32, (H, W), 1), logc)
    e_seg = jnp.where(hrow == hlane, 1.0, 0.0).astype(BF16)
    gseg = _dot_exact01(gc_col, e_seg)
    ri = lax.broadcasted_iota(jnp.int32, (C, W), 0)
    cj = jnp.bitwise_and(lax.broadcasted_iota(jnp.int32, (C, W), 1), C - 1)
    grow = jnp.sum(jnp.where(ri == cj, gseg, 0.0), axis=0, keepdims=True)
    decay_all = jnp.where(cj <= ri, jnp.exp(gseg - grow), 0.0)

    gi = lax.broadcasted_iota(jnp.int32, (C, GC), 0)
    gj = jnp.bitwise_and(lax.broadcasted_iota(jnp.int32, (C, GC), 1), C - 1)
    strict = gj < gi
    eye = jnp.where(gi == gj, 1.0, 0.0)
    nlev = logc

    def lev_mask(lev):
        return ((jnp.right_shift(gi, lev + 1) == jnp.right_shift(gj, lev + 1))
                & (jnp.bitwise_and(jnp.right_shift(gi, lev), 1) == 1)
                & (jnp.bitwise_and(jnp.right_shift(gj, lev), 1) == 0))

    @pl.when(n == 0)
    def _():
        def own(ncols, col_head):
            r = jnp.right_shift(lax.broadcasted_iota(jnp.int32, (G * C, ncols), 0), logc)
            c = col_head(lax.broadcasted_iota(jnp.int32, (G * C, ncols), 1))
            return jnp.where(r == c, 1.0, 0.0).astype(BF16)

        mt_ref[...] = own(GC, lambda c: jnp.right_shift(c, logc))
        mk_ref[...] = own(G * GDN_DK, lambda c: jnp.right_shift(c, 7))
        mx_ref[...] = own(2 * G * GDN_DK, lambda c: jnp.bitwise_and(jnp.right_shift(c, 7), G - 1))
        for lev in range(1, nlev):
            ml_ref[lev] = jnp.where(lev_mask(lev), 1.0, 0.0).astype(BF16)

    nw = nw_ref[...]
    z = z_ref[...]

    lmats, lm_hls, intras, rhss, qds, kds = [], [], [], [], [], []
    for g in range(NG):
        qs, ks, kbs, vbs, kes = [], [], [], [], []
        for t in range(G):
            h = g * G + t
            qh = y[:, h * GDN_DK:(h + 1) * GDN_DK]
            kh = y[:, GDN_KEY + h * GDN_DK:GDN_KEY + (h + 1) * GDN_DK]
            vh = y[:, 2 * GDN_KEY + h * GDN_DV:2 * GDN_KEY + (h + 1) * GDN_DV]
            q = qh * lax.rsqrt(jnp.sum(qh * qh, -1, keepdims=True) + RMS_EPS) * (GDN_DK ** -0.5)
            k = kh * lax.rsqrt(jnp.sum(kh * kh, -1, keepdims=True) + RMS_EPS)
            beta = beta_col[:, h:h + 1]
            egc = egc_col[:, h:h + 1]
            kb = k * beta
            qs.append(q)
            ks.append(k)
            kbs.append(kb)
            vbs.append(vh * beta)
            kes.append(kb * egc)
            qds.append(q * egc)
            kds.append(k * ekd_col[:, h:h + 1])
        bdk_hl = _block_rows_hl(_split_bf16(jnp.concatenate(ks, axis=-1)), G, mk_ref[...])
        dec = decay_all[:, g * GC:(g + 1) * GC]
        kk = _mm3p(_split_bf16(jnp.concatenate(kbs, axis=-1)), bdk_hl, _dot_nt)
        qk = _dot_nt(jnp.concatenate(qs, axis=-1).astype(BF16), bdk_hl[0])
        lmat = jnp.where(strict, kk * dec, 0.0)
        lmats.append(lmat)
        lm_hls.append(_split_bf16(lmat))
        intras.append((qk * dec).astype(BF16))
        rhss.append(_split_bf16(jnp.concatenate(vbs + kes, axis=-1)))

    tinvs = [eye - jnp.where(lev_mask(0), lmats[g], 0.0) for g in range(NG)]
    for lev in range(1, nlev):
        ml = ml_ref[lev]
        for g in range(NG):
            t_hl = _split_bf16(tinvs[g])
            b_hl = (lm_hls[g][0] * ml, lm_hls[g][1] * ml)
            p = _mm3p(b_hl, _block_rows_hl(t_hl, G, mt_ref[...]))
            tinvs[g] = tinvs[g] - _mm3p(t_hl, _block_rows_hl(_split_bf16(p), G, mt_ref[...]))

    for g in range(NG):
        x = _mm3p(_split_bf16(tinvs[g]), _block_rows_hl(rhss[g], G, mx_ref[...]))
        u_cat = x[:, :G * GDN_DV]
        w_cat = x[:, G * GDN_DV:]
        for t in range(G):
            h = g * G + t
            s = s_ref[h]
            s_hl = _split_bf16(s)
            lanes = slice(t * GDN_DV, (t + 1) * GDN_DV)
            v_new = u_cat[:, lanes] - _mm3p(_split_bf16(w_cat[:, lanes]), s_hl)
            v_hl = _split_bf16(v_new)
            o = _dot(qds[h].astype(BF16), s_hl[0]) + _dot(intras[g][:, t * C:(t + 1) * C], v_hl[0])
            s_ref[h] = s * egl[:, h:h + 1] + _mm3p(_split_bf16(kds[h]), v_hl, _dot_tn)
            zh = z[:, h * GDN_DV:(h + 1) * GDN_DV]
            on = o * lax.rsqrt(jnp.mean(o * o, -1, keepdims=True) + RMS_EPS) * nw * _silu(zh)
            o_ref[:, h * GDN_DV:(h + 1) * GDN_DV] = on.astype(o_ref.dtype)

    @pl.when(n == nchunks - 1)
    def _():
        sfin_ref[0] = s_ref[...]


def _gdn(proj, B, L, conv_state, gdn_state, conv_w, a_log, dt_bias, norm_w):
    C = min(CHUNK, L)
    N = L // C
    H = GDN_HEADS
    kern = functools.partial(_gdn_kernel, C=C, nchunks=N)
    return pl.pallas_call(
        kern,
        grid=(B, N),
        in_specs=[
            pl.BlockSpec((C, GDN_CONV_CH), lambda b, n: (b * N + n, C_QKV // GDN_CONV_CH)),
            pl.BlockSpec((C, GDN_VAL), lambda b, n: (b * N + n, C_Z // GDN_VAL)),
            pl.BlockSpec((C, LANE), lambda b, n: (b * N + n, C_SM // LANE)),
            pl.BlockSpec((1, CONV_W - 1, GDN_CONV_CH), lambda b, n: (b, 0, 0)),
            pl.BlockSpec((CONV_W, GDN_CONV_CH), lambda b, n: (0, 0)),
            pl.BlockSpec((1, H), lambda b, n: (0, 0)),
            pl.BlockSpec((1, H), lambda b, n: (0, 0)),
            pl.BlockSpec((1, GDN_DV), lambda b, n: (0, 0)),
            pl.BlockSpec((1, H, GDN_DK, GDN_DV), lambda b, n: (b, 0, 0, 0)),
        ],
        out_specs=[
            pl.BlockSpec((C, GDN_VAL), lambda b, n: (b * N + n, 0)),
            pl.BlockSpec((1, H, GDN_DK, GDN_DV), lambda b, n: (b, 0, 0, 0)),
        ],
        out_shape=[
            jax.ShapeDtypeStruct((B * L, GDN_VAL), BF16),
            jax.ShapeDtypeStruct((B, H, GDN_DK, GDN_DV), F32),
        ],
        scratch_shapes=[
            pltpu.VMEM((H, GDN_DK, GDN_DV), F32),
            pltpu.VMEM((8 + C, GDN_CONV_CH), F32),
            pltpu.VMEM((GDN_GROUP * C, GDN_GROUP * C), BF16),
            pltpu.VMEM((GDN_GROUP * C, GDN_GROUP * GDN_DK), BF16),
            pltpu.VMEM((GDN_GROUP * C, 2 * GDN_GROUP * GDN_DK), BF16),
            pltpu.VMEM((C.bit_length() - 1, C, GDN_GROUP * C), BF16),
        ],
        compiler_params=_cparams(("parallel", "arbitrary")),
        name="gdn",
    )(proj, proj, proj, conv_state, conv_w, a_log.reshape(1, H), dt_bias.reshape(1, H),
      norm_w.reshape(1, GDN_DV), gdn_state)


def _rope_kernel(q_ref, k_ref, v_ref, iq_ref, ik_ref, cos_ref, sa_ref, sb_ref,
                 qo_ref, kf_ref, kb_ref, vf_ref, vb_ref, iqo_ref, ikf_ref, ikb_ref):
    cosf = cos_ref[...]
    sa = sa_ref[...]
    sb = sb_ref[...]

    def rope(x):
        return (x * cosf + pltpu.roll(x, LANE - ROPE_HALF, 1) * sa + pltpu.roll(x, ROPE_HALF, 1) * sb)

    for h in range(ATT_HEADS):
        qo_ref[0, h] = rope(q_ref[:, h * ATT_HD:(h + 1) * ATT_HD]).astype(BF16)
    for h in range(ATT_KV_HEADS):
        kr = rope(k_ref[:, h * ATT_HD:(h + 1) * ATT_HD])
        kf_ref[:, h * ATT_HD:(h + 1) * ATT_HD] = kr
        kb_ref[:, h * ATT_HD:(h + 1) * ATT_HD] = kr.astype(BF16)
    v = v_ref[...]
    vf_ref[...] = v
    vb_ref[...] = v.astype(BF16)
    for h in range(IDX_HEADS):
        iqo_ref[0, h] = rope(iq_ref[:, h * IDX_HD:(h + 1) * IDX_HD]).astype(BF16)
    ikr = rope(ik_ref[...])
    ikf_ref[...] = ikr
    ikb_ref[...] = ikr.astype(BF16)


def _rope_tables(L, pos0, reps):
    inv = jnp.power(ROPE_THETA, -jnp.arange(ROPE_HALF, dtype=F32) * (2.0 / ROPE_ROT))
    ang = (pos0 + jnp.arange(L)).astype(F32)[:, None] * inv[None, :]
    cos, sin = jnp.cos(ang), jnp.sin(ang)
    z16 = jnp.zeros((L, ROPE_HALF), F32)
    rest0 = jnp.zeros((L, ATT_HD - ROPE_ROT), F32)
    cosf = jnp.concatenate([cos, cos, jnp.ones((L, ATT_HD - ROPE_ROT), F32)], -1)
    sa = jnp.concatenate([-sin, z16, rest0], -1)
    sb = jnp.concatenate([z16, sin, rest0], -1)
    if reps > 1:
        cosf, sa, sb = (jnp.tile(t, (reps, 1)) for t in (cosf, sa, sb))
    return cosf, sa, sb


def _rope(proj, B, L, pos0, qb):
    T = B * L
    reps = 1
    tab_rows = L
    cosf, sa, sb = _rope_tables(L, pos0, reps)
    nq = L // qb
    tspec = pl.BlockSpec((qb, LANE), lambda i: (i % nq, 0))
    outs = pl.pallas_call(
        _rope_kernel,
        grid=(T // qb,),
        in_specs=[
            pl.BlockSpec((qb, ATT_Q), lambda i: (i, C_Q // ATT_Q)),
            pl.BlockSpec((qb, ATT_KV), lambda i: (i, C_K // ATT_KV)),
            pl.BlockSpec((qb, ATT_KV), lambda i: (i, C_V // ATT_KV)),
            pl.BlockSpec((qb, IDX_HEADS * IDX_HD), lambda i: (i, C_IQ // (IDX_HEADS * IDX_HD))),
            pl.BlockSpec((qb, IDX_HD), lambda i: (i, C_IK // IDX_HD)),
            tspec, tspec, tspec,
        ],
        out_specs=[
            pl.BlockSpec((1, ATT_HEADS, qb, ATT_HD), lambda i: (i, 0, 0, 0)),
            pl.BlockSpec((qb, ATT_KV), lambda i: (i, 0)),
            pl.BlockSpec((qb, ATT_KV), lambda i: (i, 0)),
            pl.BlockSpec((qb, ATT_KV), lambda i: (i, 0)),
            pl.BlockSpec((qb, ATT_KV), lambda i: (i, 0)),
            pl.BlockSpec((1, IDX_HEADS, qb, IDX_HD), lambda i: (i, 0, 0, 0)),
            pl.BlockSpec((qb, IDX_HD), lambda i: (i, 0)),
            pl.BlockSpec((qb, IDX_HD), lambda i: (i, 0)),
        ],
        out_shape=[
            jax.ShapeDtypeStruct((T // qb, ATT_HEADS, qb, ATT_HD), BF16),
            jax.ShapeDtypeStruct((T, ATT_KV), F32),
            jax.ShapeDtypeStruct((T, ATT_KV), BF16),
            jax.ShapeDtypeStruct((T, ATT_KV), F32),
            jax.ShapeDtypeStruct((T, ATT_KV), BF16),
            jax.ShapeDtypeStruct((T // qb, IDX_HEADS, qb, IDX_HD), BF16),
            jax.ShapeDtypeStruct((T, IDX_HD), F32),
            jax.ShapeDtypeStruct((T, IDX_HD), BF16),
        ],
        compiler_params=_cparams(("parallel",)),
        name="rope",
    )(proj, proj, proj, proj, proj, cosf, sa, sb)
    return outs


def _dsa_kernel(iq_ref, sm_ref, q_ref, ik_ref, k_ref, v_ref, o_ref, sc_ref, *, qb, S, pos0, n_sel):
    j = pl.program_id(1)
    TK = TK_ATT
    q_last = pos0 + (j + 1) * qb - 1
    lim = jnp.minimum(((q_last // CHUNK) + 1) * CHUNK, S)
    nkt = (lim + TK - 1) // TK
    qpos = pos0 + j * qb + lax.broadcasted_iota(jnp.int32, (qb, 1), 0)
    qlim = jnp.minimum((jnp.right_shift(qpos, CHUNK.bit_length() - 1) + 1) * CHUNK, S)
    iw = sm_ref[:, SM_IW:SM_IW + IDX_HEADS] * ((IDX_HEADS ** -0.5) * (IDX_HD ** -0.5))
    HG = 4

    def score_tile(kt, carry):
        ks = pl.multiple_of(kt * TK, TK)
        ik_t = ik_ref[0, pl.ds(ks, TK), :]
        acc = jnp.zeros((qb, TK), F32)
        for hg in range(IDX_HEADS // HG):
            iq_g = iq_ref[0, hg * HG:(hg + 1) * HG].reshape(HG * qb, IDX_HD)
            r = jnp.maximum(_dot_nt(iq_g, ik_t), 0.0)
            for t in range(HG):
                hh = hg * HG + t
                acc = acc + iw[:, hh:hh + 1] * r[t * qb:(t + 1) * qb]
        kpos = ks + lax.broadcasted_iota(jnp.int32, (qb, TK), 1)
        sc_ref[kt] = jnp.where(kpos < qlim, acc, -jnp.inf)
        return carry

    lax.fori_loop(0, nkt, score_tile, 0)

    def lane_fold(m):
        p = m[:, 0:LANE]
        for t in range(1, TK // LANE):
            p = p + m[:, t * LANE:(t + 1) * LANE]
        return p

    def count_ge(x):
        def body(kt, c):
            return c + lane_fold(jnp.where(sc_ref[kt] >= x, 1.0, 0.0))
        part = lax.fori_loop(0, nkt, body, jnp.zeros((qb, LANE), F32))
        return jnp.sum(part, axis=1, keepdims=True)

    def minmax(kt, c):
        lo, hi = c
        t = sc_ref[kt]
        lo = jnp.minimum(lo, jnp.min(jnp.where(t > -jnp.inf, t, jnp.inf), axis=1, keepdims=True))
        hi = jnp.maximum(hi, jnp.max(t, axis=1, keepdims=True))
        return lo, hi

    lo0, hi0 = lax.fori_loop(0, nkt, minmax,
                             (jnp.full((qb, 1), jnp.inf, F32), jnp.full((qb, 1), -jnp.inf, F32)))
    kf = float(n_sel)
    cnt_all = count_ge(lo0)
    cnt_hi = count_ge(hi0)
    top_tied = cnt_hi >= kf
    fixed = (cnt_all <= kf) | top_tied
    lo_init = jnp.where(top_tied, hi0, lo0)

    def probe(lo, hi, clo, done):
        mid = lo + (hi - lo) * 0.5
        stop = (mid <= lo) | (mid >= hi) | (done > 0.0)
        c = count_ge(mid)
        up = jnp.logical_and(jnp.logical_not(stop), c >= kf)
        dn = jnp.logical_and(jnp.logical_not(stop), c < kf)
        done2 = jnp.where(stop | (c == kf), 1.0, 0.0)
        return jnp.where(up, mid, lo), jnp.where(dn, mid, hi), jnp.where(up, c, clo), done2

    def cond(st):
        return jnp.logical_and(st[4] > 0, st[5] < 128)

    def body(st):
        lo, hi, clo, done, _, it = st
        for _ in range(3):
            lo, hi, clo, done = probe(lo, hi, clo, done)
        nact = jnp.sum(jnp.where(done > 0.0, 0, 1))
        return lo, hi, clo, done, nact, it + 1

    done0 = jnp.where(fixed, 1.0, 0.0)
    clo0 = jnp.where(top_tied, cnt_hi, cnt_all)
    thr, _, cthr, _, _, _ = lax.while_loop(cond, body, (lo_init, hi0, clo0, done0, jnp.int32(1), jnp.int32(0)))

    s_end = sc_ref.shape[0] * TK

    def count_tied(op, x):
        def body(kt, c):
            t = sc_ref[kt]
            if op == "gt":
                hit = t > thr
            else:
                kpos = kt * TK + lax.broadcasted_iota(jnp.int32, (qb, TK), 1)
                hit = jnp.logical_and(t == thr, kpos < x)
            return c + lane_fold(jnp.where(hit, 1.0, 0.0))
        part = lax.fori_loop(0, nkt, body, jnp.zeros((qb, LANE), F32))
        return jnp.sum(part, axis=1, keepdims=True)

    def tie_bound(_):
        need = kf - count_tied("gt", None)

        def step(_, st):
            lo_i, hi_i = st
            mid = jnp.right_shift(lo_i + hi_i, 1)
            ge = count_tied("eq", mid) >= need
            return jnp.where(ge, lo_i, mid + 1), jnp.where(ge, mid, hi_i)

        _, hi_i = lax.fori_loop(0, s_end.bit_length(), step,
                                (jnp.zeros((qb, 1), jnp.int32), jnp.full((qb, 1), s_end, jnp.int32)))
        return jnp.where(cthr > kf, hi_i, s_end)

    n_over = jnp.sum(jnp.where(cthr > kf, 1, 0))
    ibound = lax.cond(n_over > 0, tie_bound, lambda _: jnp.full((qb, 1), s_end, jnp.int32), 0)

    def to_bias(kt, carry):
        t = sc_ref[kt]
        kpos = kt * TK + lax.broadcasted_iota(jnp.int32, (qb, TK), 1)
        keep = (t > thr) | ((t == thr) & (kpos < ibound))
        sc_ref[kt] = jnp.where(keep, 0.0, NEG_BIG)
        return carry

    lax.fori_loop(0, nkt, to_bias, 0)

    R = ATT_REP
    for g in range(ATT_KV_HEADS):
        qg = q_ref[0, g * R:(g + 1) * R].reshape(R * qb, ATT_HD)

        def att_tile(kt, carry, g=g, qg=qg):
            m, l, acc = carry
            ks = pl.multiple_of(kt * TK, TK)
            k_t = k_ref[0, pl.ds(ks, TK), g * ATT_HD:(g + 1) * ATT_HD]
            v_t = v_ref[0, pl.ds(ks, TK), g * ATT_HD:(g + 1) * ATT_HD]
            s = _dot_nt(qg, k_t) * (ATT_HD ** -0.5)
            s = (s.reshape(R, qb, TK) + sc_ref[kt][None]).reshape(R * qb, TK)
            m_new = jnp.maximum(m, jnp.max(s, axis=1, keepdims=True))
            p = jnp.exp(s - m_new)
            a = jnp.exp(m - m_new)
            l = a * l + jnp.sum(p, axis=1, keepdims=True)
            acc = a * acc + _dot(p.astype(BF16), v_t)
            return m_new, l, acc

        def att_pair(i, carry, att_tile=att_tile):
            return att_tile(2 * i + 1, att_tile(2 * i, carry))

        m0 = jnp.full((R * qb, 1), NEG_BIG, F32)
        l0 = jnp.zeros((R * qb, 1), F32)
        a0 = jnp.zeros((R * qb, ATT_HD), F32)
        carry = lax.fori_loop(0, nkt // 2, att_pair, (m0, l0, a0))
        _, l, acc = lax.cond(nkt % 2 == 1, lambda c, att_tile=att_tile: att_tile(nkt - 1, c), lambda c: c, carry)
        out = acc / l
        for r in range(R):
            hh = g * R + r
            o_ref[:, hh * ATT_HD:(hh + 1) * ATT_HD] = out[r * qb:(r + 1) * qb].astype(o_ref.dtype)


def _dsa(iq_hm, proj, q_hm, ik_all, k_all, v_all, B, L, S, pos0):
    qb = min(Q_BLOCK, L)
    nq = L // qb
    S_pad = ik_all.shape[1]
    n_sel = min(TOPK_KEYS, S // 4)
    kern = functools.partial(_dsa_kernel, qb=qb, S=S, pos0=pos0, n_sel=n_sel)
    return pl.pallas_call(
        kern,
        grid=(B, nq),
        in_specs=[
            pl.BlockSpec((1, IDX_HEADS, qb, IDX_HD), lambda b, j: (b * nq + j, 0, 0, 0)),
            pl.BlockSpec((qb, LANE), lambda b, j: (b * nq + j, C_SM // LANE)),
            pl.BlockSpec((1, ATT_HEADS, qb, ATT_HD), lambda b, j: (b * nq + j, 0, 0, 0)),
            pl.BlockSpec((1, S_pad, IDX_HD), lambda b, j: (b, 0, 0)),
            pl.BlockSpec((1, S_pad, ATT_KV), lambda b, j: (b, 0, 0)),
            pl.BlockSpec((1, S_pad, ATT_KV), lambda b, j: (b, 0, 0)),
        ],
        out_specs=pl.BlockSpec((qb, ATT_Q), lambda b, j: (b * nq + j, 0)),
        out_shape=jax.ShapeDtypeStruct((B * L, ATT_Q), BF16),
        scratch_shapes=[pltpu.VMEM((S_pad // TK_ATT, qb, TK_ATT), F32)],
        compiler_params=_cparams(("parallel", "arbitrary")),
        name="dsa",
    )(iq_hm, proj, q_hm, ik_all, k_all, v_all)


def _merge_kernel(oa_ref, ob_ref, wa_ref, wb_ref, ga_ref, gb_ref, o_ref):
    ya = _dot(oa_ref[...], wa_ref[...])
    yb = _dot(ob_ref[...], wb_ref[...])
    o_ref[...] = (_sigmoid(ga_ref[...]) * ya + _sigmoid(gb_ref[...]) * yb).astype(o_ref.dtype)


def _merge(oa, ob, wa, wb, proj):
    T = oa.shape[0]
    tm = TM_MERGE
    while T % tm:
        tm //= 2
    tn = TN_MIX
    return pl.pallas_call(
        _merge_kernel,
        grid=(T // tm, D_MODEL // tn),
        in_specs=[
            pl.BlockSpec((tm, GDN_VAL), lambda i, j: (i, 0)),
            pl.BlockSpec((tm, ATT_Q), lambda i, j: (i, 0)),
            pl.BlockSpec((GDN_VAL, tn), lambda i, j: (0, j)),
            pl.BlockSpec((ATT_Q, tn), lambda i, j: (0, j)),
            pl.BlockSpec((tm, tn), lambda i, j: (i, C_GA // tn + j)),
            pl.BlockSpec((tm, tn), lambda i, j: (i, C_GB // tn + j)),
        ],
        out_specs=pl.BlockSpec((tm, tn), lambda i, j: (i, j)),
        out_shape=jax.ShapeDtypeStruct((T, D_MODEL), BF16),
        compiler_params=_cparams(("parallel", "arbitrary")),
        name="merge",
    )(oa, ob, wa, wb, proj, proj)


def _oproj_kernel(m_ref, w_ref, x_ref, g1_ref, lg_ref, lb_ref, x1_ref, acc_ref, *, nj):
    j = pl.program_id(1)
    acc_ref[j] = _dot(m_ref[...], w_ref[...])

    @pl.when(j == nj - 1)
    def _():
        bb, tl, D = x_ref.shape
        tn = acc_ref.shape[2]
        sl = lambda t: slice(t * tn, (t + 1) * tn)
        s1 = jnp.zeros((bb, tl, 1), F32)
        for t in range(nj):
            v = ALPHA * x_ref[:, :, sl(t)] + g1_ref[:, :, sl(t)] * acc_ref[t].reshape(bb, tl, tn)
            acc_ref[t] = v.reshape(bb * tl, tn)
            s1 = s1 + jnp.sum(v, -1, keepdims=True)
        mu = s1 * (1.0 / D)
        s2 = jnp.zeros((bb, tl, 1), F32)
        for t in range(nj):
            d = acc_ref[t].reshape(bb, tl, tn) - mu
            s2 = s2 + jnp.sum(d * d, -1, keepdims=True)
        rstd = lax.rsqrt(s2 * (1.0 / D) + LN_EPS)
        for t in range(nj):
            x1_ref[:, :, sl(t)] = ((acc_ref[t].reshape(bb, tl, tn) - mu) * rstd * lg_ref[:, :, sl(t)]
                                   + lb_ref[:, :, sl(t)])


def _oproj(m, w_o, x3, mod3, row0, ln_g, ln_b):
    B, L, D = x3.shape
    bb, tl = _row_tiling(B, L, TM_OPROJ)
    nl = L // tl
    rows = bb * tl
    mrow = row0 // bb
    tn = TN_MIX
    nj = D // tn
    in_specs = [
        pl.BlockSpec((rows, D), lambda i, j: (i, 0), pipeline_mode=pl.Buffered(1)),
        pl.BlockSpec((D, tn), lambda i, j: (0, j)),
        pl.BlockSpec((bb, tl, D), lambda i, j: (i // nl, i % nl, 0), pipeline_mode=pl.Buffered(1)),
        pl.BlockSpec((bb, 1, D), lambda i, j: (mrow + i // nl, 0, 2)),
        pl.BlockSpec((1, 1, D), lambda i, j: (0, 0, 0)),
        pl.BlockSpec((1, 1, D), lambda i, j: (0, 0, 0)),
    ]
    args = [m, w_o, x3, mod3, ln_g.reshape(1, 1, D), ln_b.reshape(1, 1, D)]
    return pl.pallas_call(
        functools.partial(_oproj_kernel, nj=nj),
        grid=((B // bb) * nl, nj),
        in_specs=in_specs,
        out_specs=pl.BlockSpec((bb, tl, D), lambda i, j: (i // nl, i % nl, 0)),
        out_shape=jax.ShapeDtypeStruct((B, L, D), F32),
        scratch_shapes=[pltpu.VMEM((nj, rows, tn), F32)],
        compiler_params=_cparams(("parallel", "arbitrary")),
        name="oproj_ln1",
    )(*args)


def _router_kernel(x1_ref, sc2_ref, sh2_ref, w_ref, b_ref, h2_in, id_in, wt_in, h2_ref, id_ref, wt_ref):
    del h2_in, id_in, wt_in
    h2 = (x1_ref[...] * (1.0 + sc2_ref[...]) + sh2_ref[...]).reshape(h2_ref.shape)
    h2_ref[...] = h2
    logits = jnp.dot(h2, w_ref[...], precision=lax.Precision.HIGHEST, preferred_element_type=F32) + b_ref[...]
    rows = logits.shape[0]
    lane = lax.broadcasted_iota(jnp.int32, (rows, LANE), 1)
    is_g = lane < N_GROUPS
    gl = jnp.where(is_g, logits, -jnp.inf)
    gmax = jnp.max(gl, axis=1, keepdims=True)
    g_sel = jnp.min(jnp.where(gl == gmax, lane, LANE), axis=1, keepdims=True)
    p_grp = 1.0 / jnp.sum(jnp.where(is_g, jnp.exp(gl - gmax), 0.0), axis=1, keepdims=True)
    e_lo = N_GROUPS + g_sel * EXP_PER_GROUP
    in_g = (lane >= e_lo) & (lane < e_lo + EXP_PER_GROUP)
    el = jnp.where(in_g, logits, -jnp.inf)
    m1 = jnp.max(el, axis=1, keepdims=True)
    i1 = jnp.min(jnp.where(el == m1, lane, LANE), axis=1, keepdims=True)
    el2 = jnp.where(lane == i1, -jnp.inf, el)
    m2 = jnp.max(el2, axis=1, keepdims=True)
    i2 = jnp.min(jnp.where(el2 == m2, lane, LANE), axis=1, keepdims=True)
    e21 = jnp.exp(m2 - m1)
    w1 = p_grp / (1.0 + e21)
    w2 = p_grp * e21 / (1.0 + e21)
    id_ref[...] = jnp.where(lane == 0, i1 - N_GROUPS, jnp.where(lane == 1, i2 - N_GROUPS, 0))
    wt_ref[...] = jnp.where(lane == 0, w1, jnp.where(lane == 1, w2, 0.0))


def _router(x1, mod3, row0, w_r, b_r, bufs, t_off):
    B, L, D = x1.shape
    bb, tl = _row_tiling(B, L, TM_MIX)
    nl = L // tl
    rows = bb * tl
    mrow = row0 // bb
    assert t_off % rows == 0
    boff = t_off // rows
    t_all = bufs[0].shape[0]
    modspec = lambda c: pl.BlockSpec((bb, 1, D), lambda i: (mrow + i // nl, 0, c))
    anyspec = pl.BlockSpec(memory_space=pl.ANY)
    return pl.pallas_call(
        _router_kernel,
        grid=((B // bb) * nl,),
        in_specs=[
            pl.BlockSpec((bb, tl, D), lambda i: (i // nl, i % nl, 0)),
            modspec(4), modspec(3),
            pl.BlockSpec((D, LANE), lambda i: (0, 0)),
            pl.BlockSpec((1, LANE), lambda i: (0, 0)),
            anyspec, anyspec, anyspec,
        ],
        out_specs=[
            pl.BlockSpec((rows, D), lambda i: (boff + i, 0)),
            pl.BlockSpec((rows, LANE), lambda i: (boff + i, 0)),
            pl.BlockSpec((rows, LANE), lambda i: (boff + i, 0)),
        ],
        out_shape=[
            jax.ShapeDtypeStruct((t_all, D), F32),
            jax.ShapeDtypeStruct((t_all, LANE), jnp.int32),
            jax.ShapeDtypeStruct((t_all, LANE), F32),
        ],
        input_output_aliases={5: 0, 6: 1, 7: 2},
        compiler_params=_cparams(("parallel",)),
        name="router",
    )(x1, mod3, mod3, w_r, b_r, *bufs)


def _expert_kernel(blk_e_ref, tok_ref, nused_ref, h_hbm, wg_ref, wu_ref, wd_ref, o_ref,
                   xbuf_ref, xb16_ref, act_ref, sem_ref, *, BM, nfc):
    i = pl.program_id(0)
    c = pl.program_id(1)
    nused = nused_ref[0]
    slot = lax.rem(i, 2)

    def start_gather(blk, sl):
        def body(r, carry):
            tok = tok_ref[blk * BM + r]
            pltpu.make_async_copy(h_hbm.at[pl.ds(tok, 1)], xbuf_ref.at[sl, pl.ds(r, 1)], sem_ref.at[sl]).start()
            return carry
        lax.fori_loop(0, BM, body, 0, unroll=8)

    def wait_gather(sl):
        pltpu.make_async_copy(xbuf_ref.at[sl], xbuf_ref.at[sl], sem_ref.at[sl]).wait()

    @pl.when(jnp.logical_and(c == 0, i < nused))
    def _():
        @pl.when(i == 0)
        def _():
            start_gather(0, 0)

        @pl.when(i + 1 < nused)
        def _():
            start_gather(i + 1, 1 - slot)

        wait_gather(slot)
        xb16_ref[...] = xbuf_ref[slot].astype(BF16)

    @pl.when(i < nused)
    def _():
        x = xb16_ref[...]
        gate = _dot(x, wg_ref[0])
        up = _dot(x, wu_ref[0])
        act_ref[c] = (_silu(gate) * up).astype(BF16)

        @pl.when(c == nfc - 1)
        def _():
            act = jnp.concatenate([act_ref[t] for t in range(nfc)], axis=-1)
            o_ref[...] = _dot(act, wd_ref[0])

    @pl.when(jnp.logical_and(i >= nused, c == nfc - 1))
    def _():
        o_ref[...] = jnp.zeros_like(o_ref)


def _experts(h2, blk_e, slot_tok, nused, w_up16, w_down16, BM):
    T, D = h2.shape
    P = slot_tok.shape[0]
    nblk = P // BM
    fc = FC_MOE
    nfc = D_EXPERT // fc
    kern = functools.partial(_expert_kernel, BM=BM, nfc=nfc)

    def eidx(i, nu):
        return jnp.minimum(i, nu[0] - 1)

    def cidx(i, c, nu):
        return jnp.where(i < nu[0], c, nfc - 1)

    grid_spec = pltpu.PrefetchScalarGridSpec(
        num_scalar_prefetch=3,
        grid=(nblk, nfc),
        in_specs=[
            pl.BlockSpec(memory_space=pl.ANY),
            pl.BlockSpec((1, D, fc), lambda i, c, be, tk, nu: (be[eidx(i, nu)], 0, cidx(i, c, nu))),
            pl.BlockSpec((1, D, fc), lambda i, c, be, tk, nu: (be[eidx(i, nu)], 0, nfc + cidx(i, c, nu))),
            pl.BlockSpec((1, D_EXPERT, D), lambda i, c, be, tk, nu: (be[eidx(i, nu)], 0, 0)),
        ],
        out_specs=pl.BlockSpec((BM, D), lambda i, c, be, tk, nu: (i, 0)),
        scratch_shapes=[
            pltpu.VMEM((2, BM, D), F32),
            pltpu.VMEM((BM, D), BF16),
            pltpu.VMEM((nfc, BM, fc), BF16),
            pltpu.SemaphoreType.DMA((2,)),
        ],
    )
    return pl.pallas_call(
        kern,
        grid_spec=grid_spec,
        out_shape=jax.ShapeDtypeStruct((P, D), F32),
        compiler_params=_cparams(("arbitrary", "arbitrary")),
        name="experts",
    )(blk_e, slot_tok, nused, h2, w_up16, w_up16, w_down16)


def _route_metadata(ids, T, BM):
    expert = ids[:, 0:2].reshape(-1)
    A = 2 * T
    onehot = (expert[:, None] == jnp.arange(N_EXPERTS, dtype=jnp.int32)[None, :]).astype(jnp.int32)
    csum = jnp.cumsum(onehot, axis=0)
    counts = csum[-1]
    rank = jnp.sum((csum - onehot) * onehot, axis=1)
    padded = ((counts + BM - 1) // BM) * BM
    pend = jnp.cumsum(padded)
    pstart = pend - padded
    dest = (pstart[expert] + rank).astype(jnp.int32)
    P = ((A + BM - 1) // BM) * BM + N_EXPERTS * BM
    nblk = P // BM
    tok = jnp.arange(A, dtype=jnp.int32) // 2
    slot_tok = jnp.zeros((P,), jnp.int32).at[dest].set(tok)
    starts = jnp.arange(nblk, dtype=jnp.int32) * BM
    blk_e = jnp.minimum(jnp.sum((pend[None, :] <= starts[:, None]).astype(jnp.int32), axis=1), N_EXPERTS - 1)
    nused = (pend[-1] // BM).astype(jnp.int32).reshape(1)
    return dest, slot_tok, blk_e, nused


def _final_kernel(dest_ref, yb_hbm, x1_ref, g2_ref, wt_ref, lg_ref, lb_ref, o_ref, buf_ref, sem_ref, *, rows, ntiles):
    i = pl.program_id(0)
    slot = lax.rem(i, 2)

    def start_gather(tile, sl):
        def body(r, carry):
            d = dest_ref[tile * 2 * rows + r]
            pltpu.make_async_copy(yb_hbm.at[pl.ds(d, 1)], buf_ref.at[sl, pl.ds(r, 1)], sem_ref.at[sl]).start()
            return carry
        lax.fori_loop(0, 2 * rows, body, 0, unroll=8)

    def wait_gather(sl):
        pltpu.make_async_copy(buf_ref.at[sl], buf_ref.at[sl], sem_ref.at[sl]).wait()

    @pl.when(i == 0)
    def _():
        start_gather(0, 0)

    @pl.when(i + 1 < ntiles)
    def _():
        start_gather(i + 1, 1 - slot)

    wait_gather(slot)
    bb, tl, D = x1_ref.shape
    tn = TN_MIX
    sl = lambda t: slice(t * tn, (t + 1) * tn)
    w0 = wt_ref[:, 0:1]
    w1 = wt_ref[:, 1:2]
    s1 = jnp.zeros((bb, tl, 1), F32)
    for t in range(D // tn):
        f = w0 * buf_ref[slot, 0:rows, sl(t)] + w1 * buf_ref[slot, rows:2 * rows, sl(t)]
        v = ALPHA * x1_ref[:, :, sl(t)] + g2_ref[:, :, sl(t)] * f.reshape(bb, tl, tn)
        o_ref[:, :, sl(t)] = v
        s1 = s1 + jnp.sum(v, -1, keepdims=True)
    mu = s1 * (1.0 / D)
    s2 = jnp.zeros((bb, tl, 1), F32)
    for t in range(D // tn):
        d = o_ref[:, :, sl(t)] - mu
        s2 = s2 + jnp.sum(d * d, -1, keepdims=True)
    rstd = lax.rsqrt(s2 * (1.0 / D) + LN_EPS)
    for t in range(D // tn):
        o_ref[:, :, sl(t)] = (o_ref[:, :, sl(t)] - mu) * rstd * lg_ref[:, :, sl(t)] + lb_ref[:, :, sl(t)]


def _final(dest, yb, x1, mod3, row0, wts, t_off, ln_g, ln_b):
    B, L, D = x1.shape
    T = B * L
    bb, tl = _row_tiling(B, L, TM_MIX)
    nl = L // tl
    rows = bb * tl
    mrow = row0 // bb
    ntiles = (B // bb) * nl
    woff = t_off // rows
    dest_tiles = dest[2 * t_off:2 * (t_off + T)].reshape(T // rows, rows, 2).transpose(0, 2, 1).reshape(-1)
    kern = functools.partial(_final_kernel, rows=rows, ntiles=ntiles)
    grid_spec = pltpu.PrefetchScalarGridSpec(
        num_scalar_prefetch=1,
        grid=(ntiles,),
        in_specs=[
            pl.BlockSpec(memory_space=pl.ANY),
            pl.BlockSpec((bb, tl, D), lambda i, d: (i // nl, i % nl, 0)),
            pl.BlockSpec((bb, 1, D), lambda i, d: (mrow + i // nl, 0, 5)),
            pl.BlockSpec((rows, LANE), lambda i, d: (woff + i, 0)),
            pl.BlockSpec((1, 1, D), lambda i, d: (0, 0, 0)),
            pl.BlockSpec((1, 1, D), lambda i, d: (0, 0, 0)),
        ],
        out_specs=pl.BlockSpec((bb, tl, D), lambda i, d: (i // nl, i % nl, 0)),
        scratch_shapes=[
            pltpu.VMEM((2, 2 * rows, D), F32),
            pltpu.SemaphoreType.DMA((2,)),
        ],
    )
    return pl.pallas_call(
        kern,
        grid_spec=grid_spec,
        out_shape=jax.ShapeDtypeStruct((B, L, D), F32),
        compiler_params=_cparams(("arbitrary",)),
        name="combine_ln2",
    )(dest_tiles, yb, x1, mod3, wts, ln_g.reshape(1, 1, D), ln_b.reshape(1, 1, D))


def _pad_keys(t, S_pad):
    S = t.shape[1]
    if S == S_pad:
        return t
    return jnp.pad(t, ((0, 0), (0, S_pad - S), (0, 0)))


def _mixers(x3, mod3, row0, conv_state, gdn_state, past_k, past_v, past_ik, wp):
    B, L, D = x3.shape
    pos0 = past_k.shape[1]
    S = pos0 + L
    proj = _inproj(x3, mod3, row0, wp["w_in"])

    oa, new_gdn = _gdn(proj, B, L, conv_state, gdn_state, wp["conv_w"], wp["a_log"], wp["dt_bias"],
                       wp["gdn_norm_w"])
    new_conv = proj.reshape(B, L, NP)[:, L - (CONV_W - 1):, C_QKV:C_QKV + GDN_CONV_CH]

    qb = min(Q_BLOCK, L)
    q_hm, kf, k16, vf, v16, iq_hm, ikf, ik16 = _rope(proj, B, L, pos0, qb)
    S_pad = ((S + TK_ATT - 1) // TK_ATT) * TK_ATT
    k_all = _pad_keys(jnp.concatenate([past_k.reshape(B, pos0, ATT_KV).astype(BF16), k16.reshape(B, L, ATT_KV)], 1), S_pad)
    v_all = _pad_keys(jnp.concatenate([past_v.reshape(B, pos0, ATT_KV).astype(BF16), v16.reshape(B, L, ATT_KV)], 1), S_pad)
    ik_all = _pad_keys(jnp.concatenate([past_ik.astype(BF16), ik16.reshape(B, L, IDX_HD)], 1), S_pad)
    ob = _dsa(iq_hm, proj, q_hm, ik_all, k_all, v_all, B, L, S, pos0)

    m = _merge(oa, ob, wp["w_br_a"], wp["w_br_b"], proj)
    x1 = _oproj(m, wp["w_o"], x3, mod3, row0, wp["ln1_g"], wp["ln1_b"])

    kb_out = kf.reshape(B, L, ATT_KV_HEADS, ATT_HD)
    vb_out = vf.reshape(B, L, ATT_KV_HEADS, ATT_HD)
    ik_out = ikf.reshape(B, L, IDX_HD)
    return x1, (new_conv, new_gdn, kb_out, vb_out, ik_out)


WP_TN = 512
WP_TR = 1024
WP_NSRC = WP_TN // LANE + 1


def _w_in_source_columns():
    sizes = (GDN_CONV_CH, GDN_HEADS, GDN_HEADS, GDN_VAL, ATT_Q, ATT_KV, ATT_KV, IDX_HEADS * IDX_HD, IDX_HD,
             IDX_HEADS, 2 * D_MODEL)
    offs, off = [], 0
    for s in sizes:
        offs.append(off)
        off += s
    o_qkv, o_a, o_b, o_z, o_q, o_k, o_v, o_iq, o_ik, o_iw, o_g = offs
    srcs = []
    for dst, src, width in ((C_QKV, o_qkv, GDN_CONV_CH), (C_Z, o_z, GDN_VAL), (C_IQ, o_iq, IDX_HEADS * IDX_HD),
                            (C_Q, o_q, ATT_Q), (C_K, o_k, ATT_KV), (C_V, o_v, ATT_KV), (C_IK, -1, WP_TN),
                            (C_GA, o_g, 2 * D_MODEL)):
        assert dst == len(srcs) * WP_TN and width % WP_TN == 0
        srcs += [src + t * WP_TN if src >= 0 else -1 for t in range(width // WP_TN)]
    assert len(srcs) * WP_TN == NP
    return srcs, (o_a, o_b, o_ik, o_iw)


def _narrow_pieces():
    o_a, o_b, o_ik, o_iw = _W_IN_NARROW
    return ((o_ik, IDX_HD), (o_a, GDN_HEADS), (o_b, GDN_HEADS), (o_iw, IDX_HEADS))


def _narrow_blocks():
    blks = set()
    for o, w in _narrow_pieces():
        blks.update(range(o // LANE, (o + w - 1) // LANE + 1))
    return sorted(blks)


def _wperm_kernel(base_ref, shift_ref, *refs):
    src_refs = refs[:WP_NSRC]
    nb = dict(zip(_narrow_blocks(), refs[WP_NSRC:-1]))
    o_ref = refs[-1]
    shift = shift_ref[pl.program_id(0)]

    @pl.when(shift < 0)
    def _():
        def piece(o, w):
            blk, ln = o // LANE, o % LANE
            if ln + w <= LANE:
                return nb[blk][:, ln:ln + w]
            return jnp.concatenate([nb[blk][:, ln:], nb[blk + 1][:, :ln + w - LANE]], axis=1)

        parts = [piece(o, w) for o, w in _narrow_pieces()]
        used = sum(w for _, w in _narrow_pieces())
        parts.append(jnp.zeros((o_ref.shape[0], WP_TN - used), o_ref.dtype))
        o_ref[...] = jnp.concatenate(parts, axis=1)

    for sh in sorted(set(s for s in _W_IN_SHIFTS if s >= 0)):
        @pl.when(shift == sh)
        def _(sh=sh):
            a = jnp.concatenate([r[...] for r in src_refs], axis=1)
            o_ref[...] = a[:, sh:sh + WP_TN]


_W_IN_SRCS, _W_IN_NARROW = _w_in_source_columns()
_W_IN_SHIFTS = [s % LANE if s >= 0 else -1 for s in _W_IN_SRCS]


def _permute_w_in(w_in):
    D, ncols = w_in.shape
    base = jnp.asarray([max(s, 0) // LANE for s in _W_IN_SRCS], jnp.int32)
    shift = jnp.asarray(_W_IN_SHIFTS, jnp.int32)
    last = (ncols - 1) // LANE

    def src_spec(t):
        return pl.BlockSpec((WP_TR, LANE), lambda j, r, base, shift: (r, jnp.minimum(base[j] + t, last)))

    def narrow_spec(blk):
        return pl.BlockSpec((WP_TR, LANE), lambda j, r, base, shift: (r, blk))

    nblks = _narrow_blocks()
    grid_spec = pltpu.PrefetchScalarGridSpec(
        num_scalar_prefetch=2,
        grid=(NP // WP_TN, D // WP_TR),
        in_specs=[src_spec(t) for t in range(WP_NSRC)] + [narrow_spec(blk) for blk in nblks],
        out_specs=pl.BlockSpec((WP_TR, WP_TN), lambda j, r, base, shift: (r, j)),
    )
    return pl.pallas_call(
        _wperm_kernel,
        grid_spec=grid_spec,
        out_shape=jax.ShapeDtypeStruct((D, NP), BF16),
        compiler_params=_cparams(("parallel", "parallel")),
        name="w_in_permute",
    )(base, shift, *([w_in.astype(BF16)] * (WP_NSRC + len(nblks))))


def kernel(x_prompt, x_sample, c_prompt, c_sample, state_conv, state_gdn, cache_k, cache_v, cache_idx_k,
           w_ada, b_ada, w_in, conv_w, a_log, dt_bias, gdn_norm_w, w_br_a, w_br_b, w_o,
           ln1_g, ln1_b, ln2_g, ln2_b, w_grp, b_grp, w_rtr, b_rtr, w_up, w_down):
    Bp, Lp, D = x_prompt.shape
    Bs, Ls, _ = x_sample.shape
    l = 0
    row0_p = Bs
    nrows = ((Bs + Bp + 7) // 8) * 8
    c_all = jnp.concatenate([c_sample, c_prompt, jnp.zeros((nrows - Bs - Bp, D), F32)], 0)
    mod = _ada(c_all, w_ada[l], b_ada[l])
    mod3 = mod.reshape(nrows, 1, 6 * D)

    nr = LANE - N_GROUPS - N_EXPERTS
    wp = dict(
        w_in=_permute_w_in(w_in[l]),
        conv_w=conv_w[l], a_log=a_log[l], dt_bias=dt_bias[l], gdn_norm_w=gdn_norm_w[l],
        w_br_a=w_br_a[l].astype(BF16), w_br_b=w_br_b[l].astype(BF16), w_o=w_o[l].astype(BF16),
        ln1_g=ln1_g[l], ln1_b=ln1_b[l], ln2_g=ln2_g[l], ln2_b=ln2_b[l],
        w_r=jnp.concatenate([w_grp[l], w_rtr[l], jnp.zeros((D, nr), F32)], 1),
        b_r=jnp.concatenate([b_grp[l], b_rtr[l], jnp.zeros((nr,), F32)]).reshape(1, LANE),
        w_up=w_up[l].astype(BF16), w_down=w_down[l].astype(BF16),
    )

    zc = jnp.zeros((Bp, CONV_W - 1, GDN_CONV_CH), F32)
    zs = jnp.zeros((Bp, GDN_HEADS, GDN_DK, GDN_DV), F32)
    zk = jnp.zeros((Bp, 0, ATT_KV_HEADS, ATT_HD), F32)
    zik = jnp.zeros((Bp, 0, IDX_HD), F32)
    Tp, Ts = Bp * Lp, Bs * Ls
    t_all = Tp + Ts
    x1p, (c1, g1, k1, v1, i1) = _mixers(x_prompt, mod3, row0_p, zc, zs, zk, zk, zik, wp)
    x1s, (c2, g2, k2, v2, i2) = _mixers(x_sample, mod3, 0, state_conv[l], state_gdn[l], cache_k[l],
                                        cache_v[l], cache_idx_k[l], wp)

    bufs = (jnp.zeros((t_all, D), F32), jnp.zeros((t_all, LANE), jnp.int32), jnp.zeros((t_all, LANE), F32))
    bufs = _router(x1p, mod3, row0_p, wp["w_r"], wp["b_r"], bufs, 0)
    h2_all, ids, wts = _router(x1s, mod3, 0, wp["w_r"], wp["b_r"], bufs, Tp)
    dest, slot_tok, blk_e, nused = _route_metadata(ids, t_all, MOE_BM)
    yb = _experts(h2_all, blk_e, slot_tok, nused, wp["w_up"], wp["w_down"], MOE_BM)
    yp = _final(dest, yb, x1p, mod3, row0_p, wts, 0, wp["ln2_g"], wp["ln2_b"])
    ys = _final(dest, yb, x1s, mod3, 0, wts, Tp, wp["ln2_g"], wp["ln2_b"])
    st = lambda t: t[None]
    return (yp, ys, st(c1), st(g1), st(k1), st(v1), st(i1), st(c2), st(g2), st(k2), st(v2), st(i2))
```

```python
import functools

import jax
import jax.numpy as jnp
from jax import lax
from jax.experimental import pallas as pl
from jax.experimental.pallas import tpu as pltpu

F32 = jnp.float32
BF16 = jnp.bfloat16

D_MODEL = 4096
CHUNK = 64
GDN_HEADS = D_MODEL // 256
GDN_DK = 128
GDN_DV = 128
GDN_KEY = GDN_HEADS * GDN_DK
GDN_VAL = GDN_HEADS * GDN_DV
GDN_CONV_CH = 2 * GDN_KEY + GDN_VAL
CONV_W = 4
ATT_HEADS = D_MODEL // 256
ATT_KV_HEADS = ATT_HEADS // 4
ATT_REP = ATT_HEADS // ATT_KV_HEADS
ATT_HD = 128
ATT_Q = ATT_HEADS * ATT_HD
ATT_KV = ATT_KV_HEADS * ATT_HD
IDX_HEADS = D_MODEL // 128
IDX_HD = 128
TOPK_KEYS = 256
Q_BLOCK = 128
ROPE_THETA = 500000.0
ROPE_ROT = ATT_HD // 4
ROPE_HALF = ROPE_ROT // 2
N_GROUPS = 4
EXP_PER_GROUP = 8
N_EXPERTS = N_GROUPS * EXP_PER_GROUP
D_EXPERT = D_MODEL // 4
DEPTH = 1
ALPHA = (2.0 * DEPTH) ** 0.25
LN_EPS = 1e-5
RMS_EPS = 1e-6

C_QKV = 0
C_Z = C_QKV + GDN_CONV_CH
C_IQ = C_Z + GDN_VAL
C_Q = C_IQ + IDX_HEADS * IDX_HD
C_K = C_Q + ATT_Q
C_V = C_K + ATT_KV
C_IK = C_V + ATT_KV
C_SM = C_IK + IDX_HD
SM_W = 384
C_GA = C_SM + SM_W
C_GB = C_GA + D_MODEL
NP = C_GB + D_MODEL
SM_A, SM_B, SM_IW = 0, GDN_HEADS, 2 * GDN_HEADS

LANE = 128
V7X_VMEM_BYTES = 64 * 1024 * 1024
VMEM_LIMIT = V7X_VMEM_BYTES * 7 // 8
NEG_BIG = -1e30

TN_IN = 512
TM_IN = 1024
TM_MIX = 256
TM_OPROJ = 512
TM_MERGE = 1024
TN_MIX = 512
TK_ATT = 512
FC_MOE = 256
MOE_BM = 256


def _cparams(sem):
    return pltpu.CompilerParams(dimension_semantics=sem, vmem_limit_bytes=VMEM_LIMIT)


def _dot(a, b):
    return jnp.dot(a, b, preferred_element_type=F32)


def _dot_nt(a, b):
    return lax.dot_general(a, b, (((1,), (1,)), ((), ())), preferred_element_type=F32)


def _dot_tn(a, b):
    return lax.dot_general(a, b, (((0,), (0,)), ((), ())), preferred_element_type=F32)


def _split_bf16(a):
    hi = a.astype(BF16)
    lo = (a - hi.astype(F32)).astype(BF16)
    return hi, lo


def _mm3(a, b, dot=_dot):
    ah, al = _split_bf16(a)
    bh, bl = _split_bf16(b)
    return dot(ah, bh) + (dot(ah, bl) + dot(al, bh))


def _mm1(a, b):
    return _dot(a.astype(BF16), b.astype(BF16))


def _mm1_nt(a, b):
    return _dot_nt(a.astype(BF16), b.astype(BF16))


def _sigmoid(x):
    return 1.0 / (1.0 + jnp.exp(-x))


def _silu(x):
    return x * _sigmoid(x)


def _softplus(x):
    return jnp.maximum(x, 0.0) + jnp.log(1.0 + jnp.exp(-jnp.abs(x)))


def _ada_kernel(c_ref, w_ref, b_ref, o_ref):
    s = _silu(c_ref[...])
    o_ref[...] = _dot(s.astype(BF16), w_ref[...].astype(BF16)) + b_ref[...]


def _ada(c_all, w_ada, b_ada):
    R, D = c_all.shape
    N = w_ada.shape[1]
    tn = 512
    return pl.pallas_call(
        _ada_kernel,
        grid=(N // tn,),
        in_specs=[
            pl.BlockSpec((R, D), lambda j: (0, 0)),
            pl.BlockSpec((D, tn), lambda j: (0, j)),
            pl.BlockSpec((1, tn), lambda j: (0, j)),
        ],
        out_specs=pl.BlockSpec((R, tn), lambda j: (0, j)),
        out_shape=jax.ShapeDtypeStruct((R, N), F32),
        compiler_params=_cparams(("parallel",)),
        name="ada",
    )(c_all, w_ada, b_ada.reshape(1, N))


def _inproj_kernel(x_ref, sc_ref, sh_ref, w_ref, o_ref, h_ref):
    @pl.when(pl.program_id(1) == 0)
    def _():
        h = x_ref[...] * (1.0 + sc_ref[...]) + sh_ref[...]
        h_ref[...] = h.reshape(h_ref.shape).astype(BF16)

    o_ref[...] = _dot(h_ref[...], w_ref[...])


def _row_tiling(B, L, tm):
    if L >= tm:
        return 1, tm
    bb = max(1, min(B, tm // L))
    while B % bb:
        bb -= 1
    return bb, L


def _inproj(x3, mod3, row0, w_perm):
    B, L, D = x3.shape
    bb, tl = _row_tiling(B, L, TM_IN)
    nl = L // tl
    rows = bb * tl
    mrow = row0 // bb
    return pl.pallas_call(
        _inproj_kernel,
        grid=((B // bb) * nl, NP // TN_IN),
        in_specs=[
            pl.BlockSpec((bb, tl, D), lambda i, j: (i // nl, i % nl, 0), pipeline_mode=pl.Buffered(1)),
            pl.BlockSpec((bb, 1, D), lambda i, j: (mrow + i // nl, 0, 1)),
            pl.BlockSpec((bb, 1, D), lambda i, j: (mrow + i // nl, 0, 0)),
            pl.BlockSpec((D, TN_IN), lambda i, j: (0, j)),
        ],
        out_specs=pl.BlockSpec((rows, TN_IN), lambda i, j: (i, j)),
        out_shape=jax.ShapeDtypeStruct((B * L, NP), F32),
        scratch_shapes=[pltpu.VMEM((rows, D), BF16)],
        compiler_params=_cparams(("parallel", "arbitrary")),
        name="inproj",
    )(x3, mod3, mod3, w_perm)


GDN_GROUP = 4


def _split3_bf16(a):
    hi = a.astype(BF16)
    r = a - hi.astype(F32)
    mid = r.astype(BF16)
    lo = (r - mid.astype(F32)).astype(BF16)
    return hi, mid, lo


def _dot_exact01(a, b01, a_is_01=False):
    if a_is_01:
        h, m, l = _split3_bf16(b01)
        return _dot(a, h) + (_dot(a, m) + _dot(a, l))
    h, m, l = _split3_bf16(a)
    return _dot(h, b01) + (_dot(m, b01) + _dot(l, b01))


def _mm3p(a_hl, b_hl, dot=_dot):
    (ah, al), (bh, bl) = a_hl, b_hl
    return dot(ah, bh) + (dot(ah, bl) + dot(al, bh))


def _block_rows_hl(x_hl, nblk, mask01):
    return tuple(jnp.concatenate([p] * nblk, axis=0) * mask01 for p in x_hl)


def _gdn_kernel(qkv_ref, z_ref, sm_ref, cst_ref, cw_ref, alr_ref, dtr_ref, nw_ref, s0_ref,
                o_ref, sfin_ref, s_ref, ext_ref, mt_ref, mk_ref, mx_ref, ml_ref, *, C, nchunks):
    n = pl.program_id(1)
    H = GDN_HEADS
    G = GDN_GROUP
    NG = H // G
    GC = G * C
    PADR = 8
    logc = C.bit_length() - 1

    @pl.when(n == 0)
    def _():
        s_ref[...] = s0_ref[0]
        ext_ref[PADR - (CONV_W - 1):PADR, :] = cst_ref[0]

    ext_ref[PADR:PADR + C, :] = qkv_ref[...]

    def conv_silu(c0):
        cols = slice(c0, c0 + LANE)
        yc = ext_ref[PADR - 3:PADR - 3 + C, cols] * cw_ref[0:1, cols]
        for jw in range(1, CONV_W):
            yc = yc + ext_ref[PADR - 3 + jw:PADR - 3 + jw + C, cols] * cw_ref[jw:jw + 1, cols]
        return _silu(yc)

    sm = sm_ref[...]
    g_col = -jnp.exp(alr_ref[...]) * _softplus(sm[:, SM_A:SM_A + H] + dtr_ref[...])
    beta_col = _sigmoid(sm[:, SM_B:SM_B + H])

    ii = lax.broadcasted_iota(jnp.int32, (C, C), 0)
    jj = lax.broadcasted_iota(jnp.int32, (C, C), 1)
    tril01 = jnp.where(jj <= ii, 1.0, 0.0).astype(BF16)
    gc_col = _dot_exact01(tril01, g_col, a_is_01=True)
    egc_col = jnp.exp(gc_col)
    gc_last = gc_col[C - 1:C, :]
    ekd_col = jnp.exp(gc_last - gc_col)
    egl = jnp.exp(gc_last)

    W = H * C
    hrow = lax.broadcasted_iota(jnp.int32, (H, W), 0)
    hlane = jnp.right_shift(lax.broadcasted_iota(jnp.int32, (H, W), 1), logc)
    e_seg = jnp.where(hrow == hlane, 1.0, 0.0).astype(BF16)
    gseg = _dot_exact01(gc_col, e_seg)
    ri = lax.broadcasted_iota(jnp.int32, (C, W), 0)
    cj = jnp.bitwise_and(lax.broadcasted_iota(jnp.int32, (C, W), 1), C - 1)
    grow = jnp.sum(jnp.where(ri == cj, gseg, 0.0), axis=0, keepdims=True)
    decay_all = jnp.where(cj <= ri, jnp.exp(gseg - grow), 0.0)

    gi = lax.broadcasted_iota(jnp.int32, (C, GC), 0)
    gj = jnp.bitwise_and(lax.broadcasted_iota(jnp.int32, (C, GC), 1), C - 1)
    strict = gj < gi
    eye = jnp.where(gi == gj, 1.0, 0.0)
    nlev = logc

    def lev_mask(lev):
        return ((jnp.right_shift(gi, lev + 1) == jnp.right_shift(gj, lev + 1))
                & (jnp.bitwise_and(jnp.right_shift(gi, lev), 1) == 1)
                & (jnp.bitwise_and(jnp.right_shift(gj, lev), 1) == 0))

    @pl.when(n == 0)
    def _():
        def own(ncols, col_head):
            r = jnp.right_shift(lax.broadcasted_iota(jnp.int32, (G * C, ncols), 0), logc)
            c = col_head(lax.broadcasted_iota(jnp.int32, (G * C, ncols), 1))
            return jnp.where(r == c, 1.0, 0.0).astype(BF16)

        mt_ref[...] = own(GC, lambda c: jnp.right_shift(c, logc))
        mk_ref[...] = own(G * GDN_DK, lambda c: jnp.right_shift(c, 7))
        mx_ref[...] = own(2 * G * GDN_DK, lambda c: jnp.bitwise_and(jnp.right_shift(c, 7), G - 1))
        for lev in range(1, nlev):
            ml_ref[lev] = jnp.where(lev_mask(lev), 1.0, 0.0).astype(BF16)

    nw = nw_ref[...]
    z = z_ref[...]

    lmats, lm_hls, intras, rhss, qds, kds = [], [], [], [], [], []
    for g in range(NG):
        qs, ks, kbs, vbs, kes = [], [], [], [], []
        for t in range(G):
            h = g * G + t
            qh = conv_silu(h * GDN_DK)
            kh = conv_silu(GDN_KEY + h * GDN_DK)
            vh = conv_silu(2 * GDN_KEY + h * GDN_DV)
            q = qh * lax.rsqrt(jnp.sum(qh * qh, -1, keepdims=True) + RMS_EPS) * (GDN_DK ** -0.5)
            k = kh * lax.rsqrt(jnp.sum(kh * kh, -1, keepdims=True) + RMS_EPS)
            beta = beta_col[:, h:h + 1]
            egc = egc_col[:, h:h + 1]
            kb = k * beta
            qs.append(q)
            ks.append(k)
            kbs.append(kb)
            vbs.append(vh * beta)
            kes.append(kb * egc)
            qds.append(q * egc)
            kds.append(k * ekd_col[:, h:h + 1])
        bdk_hl = _block_rows_hl(_split_bf16(jnp.concatenate(ks, axis=-1)), G, mk_ref[...])
        dec = decay_all[:, g * GC:(g + 1) * GC]
        kk = _mm3p(_split_bf16(jnp.concatenate(kbs, axis=-1)), bdk_hl, _dot_nt)
        qk = _dot_nt(jnp.concatenate(qs, axis=-1).astype(BF16), bdk_hl[0])
        lmat = jnp.where(strict, kk * dec, 0.0)
        lmats.append(lmat)
        lm_hls.append(_split_bf16(lmat))
        intras.append((qk * dec).astype(BF16))
        rhss.append(_split_bf16(jnp.concatenate(vbs + kes, axis=-1)))

    hist = ext_ref[PADR + C - (CONV_W - 1):PADR + C, :]
    ext_ref[PADR - (CONV_W - 1):PADR, :] = hist

    tinvs = [eye - jnp.where(lev_mask(0), lmats[g], 0.0) for g in range(NG)]
    for lev in range(1, nlev):
        ml = ml_ref[lev]
        t_hls = [_split_bf16(tinvs[g]) for g in range(NG)]
        ps = [_mm3p((lm_hls[g][0] * ml, lm_hls[g][1] * ml), _block_rows_hl(t_hls[g], G, mt_ref[...]))
              for g in range(NG)]
        tinvs = [tinvs[g] - _mm3p(t_hls[g], _block_rows_hl(_split_bf16(ps[g]), G, mt_ref[...]))
                 for g in range(NG)]

    xs = [_mm3p(_split_bf16(tinvs[g]), _block_rows_hl(rhss[g], G, mx_ref[...])) for g in range(NG)]
    s_olds = [s_ref[h] for h in range(H)]
    s_hls = [_split_bf16(s) for s in s_olds]
    v_hls = []
    for h in range(H):
        g, t = divmod(h, G)
        u = xs[g][:, t * GDN_DV:(t + 1) * GDN_DV]
        w = xs[g][:, (G + t) * GDN_DV:(G + t + 1) * GDN_DV]
        v_hls.append(_split_bf16(u - _mm3p(_split_bf16(w), s_hls[h])))
    for h in range(H):
        s_ref[h] = s_olds[h] * egl[:, h:h + 1] + _mm3p(_split_bf16(kds[h]), v_hls[h], _dot_tn)
    for h in range(H):
        g, t = divmod(h, G)
        o = _dot(qds[h].astype(BF16), s_hls[h][0]) + _dot(intras[g][:, t * C:(t + 1) * C], v_hls[h][0])
        zh = z[:, h * GDN_DV:(h + 1) * GDN_DV]
        on = o * lax.rsqrt(jnp.mean(o * o, -1, keepdims=True) + RMS_EPS) * nw * _silu(zh)
        o_ref[:, h * GDN_DV:(h + 1) * GDN_DV] = on.astype(o_ref.dtype)

    @pl.when(n == nchunks - 1)
    def _():
        sfin_ref[0] = s_ref[...]


def _gdn(proj, B, L, conv_state, gdn_state, conv_w, a_log, dt_bias, norm_w):
    C = min(CHUNK, L)
    N = L // C
    H = GDN_HEADS
    kern = functools.partial(_gdn_kernel, C=C, nchunks=N)
    return pl.pallas_call(
        kern,
        grid=(B, N),
        in_specs=[
            pl.BlockSpec((C, GDN_CONV_CH), lambda b, n: (b * N + n, C_QKV // GDN_CONV_CH)),
            pl.BlockSpec((C, GDN_VAL), lambda b, n: (b * N + n, C_Z // GDN_VAL)),
            pl.BlockSpec((C, LANE), lambda b, n: (b * N + n, C_SM // LANE)),
            pl.BlockSpec((1, CONV_W - 1, GDN_CONV_CH), lambda b, n: (b, 0, 0)),
            pl.BlockSpec((CONV_W, GDN_CONV_CH), lambda b, n: (0, 0)),
            pl.BlockSpec((1, H), lambda b, n: (0, 0)),
            pl.BlockSpec((1, H), lambda b, n: (0, 0)),
            pl.BlockSpec((1, GDN_DV), lambda b, n: (0, 0)),
            pl.BlockSpec((1, H, GDN_DK, GDN_DV), lambda b, n: (b, 0, 0, 0)),
        ],
        out_specs=[
            pl.BlockSpec((C, GDN_VAL), lambda b, n: (b * N + n, 0)),
            pl.BlockSpec((1, H, GDN_DK, GDN_DV), lambda b, n: (b, 0, 0, 0)),
        ],
        out_shape=[
            jax.ShapeDtypeStruct((B * L, GDN_VAL), BF16),
            jax.ShapeDtypeStruct((B, H, GDN_DK, GDN_DV), F32),
        ],
        scratch_shapes=[
            pltpu.VMEM((H, GDN_DK, GDN_DV), F32),
            pltpu.VMEM((8 + C, GDN_CONV_CH), F32),
            pltpu.VMEM((GDN_GROUP * C, GDN_GROUP * C), BF16),
            pltpu.VMEM((GDN_GROUP * C, GDN_GROUP * GDN_DK), BF16),
            pltpu.VMEM((GDN_GROUP * C, 2 * GDN_GROUP * GDN_DK), BF16),
            pltpu.VMEM((C.bit_length() - 1, C, GDN_GROUP * C), BF16),
        ],
        compiler_params=_cparams(("parallel", "arbitrary")),
        name="gdn",
    )(proj, proj, proj, conv_state, conv_w, a_log.reshape(1, H), dt_bias.reshape(1, H),
      norm_w.reshape(1, GDN_DV), gdn_state)


def _rope_kernel(q_ref, k_ref, v_ref, iq_ref, ik_ref, cos_ref, sa_ref, sb_ref,
                 qo_ref, kf_ref, kb_ref, vf_ref, vb_ref, iqo_ref, ikf_ref, ikb_ref):
    cosf = cos_ref[...]
    sa = sa_ref[...]
    sb = sb_ref[...]

    def rope(x):
        return (x * cosf + pltpu.roll(x, LANE - ROPE_HALF, 1) * sa + pltpu.roll(x, ROPE_HALF, 1) * sb)

    for h in range(ATT_HEADS):
        qo_ref[0, h] = rope(q_ref[:, h * ATT_HD:(h + 1) * ATT_HD]).astype(BF16)
    for h in range(ATT_KV_HEADS):
        kr = rope(k_ref[:, h * ATT_HD:(h + 1) * ATT_HD])
        kf_ref[:, h * ATT_HD:(h + 1) * ATT_HD] = kr
        kb_ref[:, h * ATT_HD:(h + 1) * ATT_HD] = kr.astype(BF16)
    v = v_ref[...]
    vf_ref[...] = v
    vb_ref[...] = v.astype(BF16)
    for h in range(IDX_HEADS):
        iqo_ref[0, h] = rope(iq_ref[:, h * IDX_HD:(h + 1) * IDX_HD]).astype(BF16)
    ikr = rope(ik_ref[...])
    ikf_ref[...] = ikr
    ikb_ref[...] = ikr.astype(BF16)


def _rope_tables(L, pos0, reps):
    inv = jnp.power(ROPE_THETA, -jnp.arange(ROPE_HALF, dtype=F32) * (2.0 / ROPE_ROT))
    ang = (pos0 + jnp.arange(L)).astype(F32)[:, None] * inv[None, :]
    cos, sin = jnp.cos(ang), jnp.sin(ang)
    z16 = jnp.zeros((L, ROPE_HALF), F32)
    rest0 = jnp.zeros((L, ATT_HD - ROPE_ROT), F32)
    cosf = jnp.concatenate([cos, cos, jnp.ones((L, ATT_HD - ROPE_ROT), F32)], -1)
    sa = jnp.concatenate([-sin, z16, rest0], -1)
    sb = jnp.concatenate([z16, sin, rest0], -1)
    if reps > 1:
        cosf, sa, sb = (jnp.tile(t, (reps, 1)) for t in (cosf, sa, sb))
    return cosf, sa, sb


def _rope(proj, B, L, pos0, qb):
    T = B * L
    reps = 1
    tab_rows = L
    cosf, sa, sb = _rope_tables(L, pos0, reps)
    nq = L // qb
    tspec = pl.BlockSpec((qb, LANE), lambda i: (i % nq, 0))
    outs = pl.pallas_call(
        _rope_kernel,
        grid=(T // qb,),
        in_specs=[
            pl.BlockSpec((qb, ATT_Q), lambda i: (i, C_Q // ATT_Q)),
            pl.BlockSpec((qb, ATT_KV), lambda i: (i, C_K // ATT_KV)),
            pl.BlockSpec((qb, ATT_KV), lambda i: (i, C_V // ATT_KV)),
            pl.BlockSpec((qb, IDX_HEADS * IDX_HD), lambda i: (i, C_IQ // (IDX_HEADS * IDX_HD))),
            pl.BlockSpec((qb, IDX_HD), lambda i: (i, C_IK // IDX_HD)),
            tspec, tspec, tspec,
        ],
        out_specs=[
            pl.BlockSpec((1, ATT_HEADS, qb, ATT_HD), lambda i: (i, 0, 0, 0)),
            pl.BlockSpec((qb, ATT_KV), lambda i: (i, 0)),
            pl.BlockSpec((qb, ATT_KV), lambda i: (i, 0)),
            pl.BlockSpec((qb, ATT_KV), lambda i: (i, 0)),
            pl.BlockSpec((qb, ATT_KV), lambda i: (i, 0)),
            pl.BlockSpec((1, IDX_HEADS, qb, IDX_HD), lambda i: (i, 0, 0, 0)),
            pl.BlockSpec((qb, IDX_HD), lambda i: (i, 0)),
            pl.BlockSpec((qb, IDX_HD), lambda i: (i, 0)),
        ],
        out_shape=[
            jax.ShapeDtypeStruct((T // qb, ATT_HEADS, qb, ATT_HD), BF16),
            jax.ShapeDtypeStruct((T, ATT_KV), F32),
            jax.ShapeDtypeStruct((T, ATT_KV), BF16),
            jax.ShapeDtypeStruct((T, ATT_KV), F32),
            jax.ShapeDtypeStruct((T, ATT_KV), BF16),
            jax.ShapeDtypeStruct((T // qb, IDX_HEADS, qb, IDX_HD), BF16),
            jax.ShapeDtypeStruct((T, IDX_HD), F32),
            jax.ShapeDtypeStruct((T, IDX_HD), BF16),
        ],
        compiler_params=_cparams(("parallel",)),
        name="rope",
    )(proj, proj, proj, proj, proj, cosf, sa, sb)
    return outs


def _dsa_kernel(iq_ref, sm_ref, q_ref, ik_ref, k_ref, v_ref, o_ref, sc_ref, *, qb, S, pos0, n_sel):
    j = pl.program_id(1)
    TK = TK_ATT
    q_last = pos0 + (j + 1) * qb - 1
    lim = jnp.minimum(((q_last // CHUNK) + 1) * CHUNK, S)
    nkt = (lim + TK - 1) // TK
    qpos = pos0 + j * qb + lax.broadcasted_iota(jnp.int32, (qb, 1), 0)
    qlim = jnp.minimum((jnp.right_shift(qpos, CHUNK.bit_length() - 1) + 1) * CHUNK, S)
    iw = sm_ref[:, SM_IW:SM_IW + IDX_HEADS] * ((IDX_HEADS ** -0.5) * (IDX_HD ** -0.5))
    HG = 4

    def score_tile(kt, carry):
        ks = pl.multiple_of(kt * TK, TK)
        ik_t = ik_ref[0, pl.ds(ks, TK), :]
        acc = jnp.zeros((qb, TK), F32)
        for hg in range(IDX_HEADS // HG):
            iq_g = iq_ref[0, hg * HG:(hg + 1) * HG].reshape(HG * qb, IDX_HD)
            r = jnp.maximum(_dot_nt(iq_g, ik_t), 0.0)
            for t in range(HG):
                hh = hg * HG + t
                acc = acc + iw[:, hh:hh + 1] * r[t * qb:(t + 1) * qb]
        kpos = ks + lax.broadcasted_iota(jnp.int32, (qb, TK), 1)
        sc_ref[kt] = jnp.where(kpos < qlim, acc, -jnp.inf)
        return carry

    lax.fori_loop(0, nkt, score_tile, 0)

    def lane_fold(m):
        p = m[:, 0:LANE]
        for t in range(1, TK // LANE):
            p = p + m[:, t * LANE:(t + 1) * LANE]
        return p

    def count_ge(x):
        def body(kt, c):
            return c + lane_fold(jnp.where(sc_ref[kt] >= x, 1.0, 0.0))
        part = lax.fori_loop(0, nkt, body, jnp.zeros((qb, LANE), F32))
        return jnp.sum(part, axis=1, keepdims=True)

    def minmax(kt, c):
        lo, hi = c
        t = sc_ref[kt]
        lo = jnp.minimum(lo, jnp.min(jnp.where(t > -jnp.inf, t, jnp.inf), axis=1, keepdims=True))
        hi = jnp.maximum(hi, jnp.max(t, axis=1, keepdims=True))
        return lo, hi

    lo0, hi0 = lax.fori_loop(0, nkt, minmax,
                             (jnp.full((qb, 1), jnp.inf, F32), jnp.full((qb, 1), -jnp.inf, F32)))
    kf = float(n_sel)
    cnt_all = count_ge(lo0)
    cnt_hi = count_ge(hi0)
    top_tied = cnt_hi >= kf
    fixed = (cnt_all <= kf) | top_tied
    lo_init = jnp.where(top_tied, hi0, lo0)

    def probe(lo, hi, clo, done):
        mid = lo + (hi - lo) * 0.5
        stop = (mid <= lo) | (mid >= hi) | (done > 0.0)
        c = count_ge(mid)
        up = jnp.logical_and(jnp.logical_not(stop), c >= kf)
        dn = jnp.logical_and(jnp.logical_not(stop), c < kf)
        done2 = jnp.where(stop | (c == kf), 1.0, 0.0)
        return jnp.where(up, mid, lo), jnp.where(dn, mid, hi), jnp.where(up, c, clo), done2

    def cond(st):
        return jnp.logical_and(st[4] > 0, st[5] < 128)

    def body(st):
        lo, hi, clo, done, _, it = st
        for _ in range(3):
            lo, hi, clo, done = probe(lo, hi, clo, done)
        nact = jnp.sum(jnp.where(done > 0.0, 0, 1))
        return lo, hi, clo, done, nact, it + 1

    done0 = jnp.where(fixed, 1.0, 0.0)
    clo0 = jnp.where(top_tied, cnt_hi, cnt_all)
    thr, _, cthr, _, _, _ = lax.while_loop(cond, body, (lo_init, hi0, clo0, done0, jnp.int32(1), jnp.int32(0)))

    s_end = sc_ref.shape[0] * TK

    def count_tied(op, x):
        def body(kt, c):
            t = sc_ref[kt]
            if op == "gt":
                hit = t > thr
            else:
                kpos = kt * TK + lax.broadcasted_iota(jnp.int32, (qb, TK), 1)
                hit = jnp.logical_and(t == thr, kpos < x)
            return c + lane_fold(jnp.where(hit, 1.0, 0.0))
        part = lax.fori_loop(0, nkt, body, jnp.zeros((qb, LANE), F32))
        return jnp.sum(part, axis=1, keepdims=True)

    def tie_bound(_):
        need = kf - count_tied("gt", None)

        def step(_, st):
            lo_i, hi_i = st
            mid = jnp.right_shift(lo_i + hi_i, 1)
            ge = count_tied("eq", mid) >= need
            return jnp.where(ge, lo_i, mid + 1), jnp.where(ge, mid, hi_i)

        _, hi_i = lax.fori_loop(0, s_end.bit_length(), step,
                                (jnp.zeros((qb, 1), jnp.int32), jnp.full((qb, 1), s_end, jnp.int32)))
        return jnp.where(cthr > kf, hi_i, s_end)

    n_over = jnp.sum(jnp.where(cthr > kf, 1, 0))
    ibound = lax.cond(n_over > 0, tie_bound, lambda _: jnp.full((qb, 1), s_end, jnp.int32), 0)

    def to_bias(kt, carry):
        t = sc_ref[kt]
        kpos = kt * TK + lax.broadcasted_iota(jnp.int32, (qb, TK), 1)
        keep = (t > thr) | ((t == thr) & (kpos < ibound))
        sc_ref[kt] = jnp.where(keep, 0.0, NEG_BIG)
        return carry

    lax.fori_loop(0, nkt, to_bias, 0)

    R = ATT_REP
    for g in range(ATT_KV_HEADS):
        qg = q_ref[0, g * R:(g + 1) * R].reshape(R * qb, ATT_HD)

        def att_tile(kt, carry, g=g, qg=qg):
            m, l, acc = carry
            ks = pl.multiple_of(kt * TK, TK)
            k_t = k_ref[0, pl.ds(ks, TK), g * ATT_HD:(g + 1) * ATT_HD]
            v_t = v_ref[0, pl.ds(ks, TK), g * ATT_HD:(g + 1) * ATT_HD]
            s = _dot_nt(qg, k_t) * (ATT_HD ** -0.5)
            s = (s.reshape(R, qb, TK) + sc_ref[kt][None]).reshape(R * qb, TK)
            m_new = jnp.maximum(m, jnp.max(s, axis=1, keepdims=True))
            p = jnp.exp(s - m_new)
            a = jnp.exp(m - m_new)
            l = a * l + jnp.sum(p, axis=1, keepdims=True)
            acc = a * acc + _dot(p.astype(BF16), v_t)
            return m_new, l, acc

        def att_pair(i, carry, att_tile=att_tile):
            return att_tile(2 * i + 1, att_tile(2 * i, carry))

        m0 = jnp.full((R * qb, 1), NEG_BIG, F32)
        l0 = jnp.zeros((R * qb, 1), F32)
        a0 = jnp.zeros((R * qb, ATT_HD), F32)
        carry = lax.fori_loop(0, nkt // 2, att_pair, (m0, l0, a0))
        _, l, acc = lax.cond(nkt % 2 == 1, lambda c, att_tile=att_tile: att_tile(nkt - 1, c), lambda c: c, carry)
        out = acc / l
        for r in range(R):
            hh = g * R + r
            o_ref[:, hh * ATT_HD:(hh + 1) * ATT_HD] = out[r * qb:(r + 1) * qb].astype(o_ref.dtype)


def _dsa(iq_hm, proj, q_hm, ik_all, k_all, v_all, B, L, S, pos0):
    qb = min(Q_BLOCK, L)
    nq = L // qb
    S_pad = ik_all.shape[1]
    n_sel = min(TOPK_KEYS, S // 4)
    kern = functools.partial(_dsa_kernel, qb=qb, S=S, pos0=pos0, n_sel=n_sel)
    return pl.pallas_call(
        kern,
        grid=(B, nq),
        in_specs=[
            pl.BlockSpec((1, IDX_HEADS, qb, IDX_HD), lambda b, j: (b * nq + j, 0, 0, 0)),
            pl.BlockSpec((qb, LANE), lambda b, j: (b * nq + j, C_SM // LANE)),
            pl.BlockSpec((1, ATT_HEADS, qb, ATT_HD), lambda b, j: (b * nq + j, 0, 0, 0)),
            pl.BlockSpec((1, S_pad, IDX_HD), lambda b, j: (b, 0, 0)),
            pl.BlockSpec((1, S_pad, ATT_KV), lambda b, j: (b, 0, 0)),
            pl.BlockSpec((1, S_pad, ATT_KV), lambda b, j: (b, 0, 0)),
        ],
        out_specs=pl.BlockSpec((qb, ATT_Q), lambda b, j: (b * nq + j, 0)),
        out_shape=jax.ShapeDtypeStruct((B * L, ATT_Q), BF16),
        scratch_shapes=[pltpu.VMEM((S_pad // TK_ATT, qb, TK_ATT), F32)],
        compiler_params=_cparams(("parallel", "arbitrary")),
        name="dsa",
    )(iq_hm, proj, q_hm, ik_all, k_all, v_all)


def _merge_kernel(oa_ref, ob_ref, wa_ref, wb_ref, ga_ref, gb_ref, o_ref):
    ya = _dot(oa_ref[...], wa_ref[...])
    yb = _dot(ob_ref[...], wb_ref[...])
    o_ref[...] = (_sigmoid(ga_ref[...]) * ya + _sigmoid(gb_ref[...]) * yb).astype(o_ref.dtype)


def _merge(oa, ob, wa, wb, proj):
    T = oa.shape[0]
    tm = TM_MERGE
    while T % tm:
        tm //= 2
    tn = TN_MIX
    return pl.pallas_call(
        _merge_kernel,
        grid=(T // tm, D_MODEL // tn),
        in_specs=[
            pl.BlockSpec((tm, GDN_VAL), lambda i, j: (i, 0)),
            pl.BlockSpec((tm, ATT_Q), lambda i, j: (i, 0)),
            pl.BlockSpec((GDN_VAL, tn), lambda i, j: (0, j)),
            pl.BlockSpec((ATT_Q, tn), lambda i, j: (0, j)),
            pl.BlockSpec((tm, tn), lambda i, j: (i, C_GA // tn + j)),
            pl.BlockSpec((tm, tn), lambda i, j: (i, C_GB // tn + j)),
        ],
        out_specs=pl.BlockSpec((tm, tn), lambda i, j: (i, j)),
        out_shape=jax.ShapeDtypeStruct((T, D_MODEL), BF16),
        compiler_params=_cparams(("parallel", "arbitrary")),
        name="merge",
    )(oa, ob, wa, wb, proj, proj)


def _oproj_kernel(m_ref, w_ref, x_ref, g1_ref, lg_ref, lb_ref, x1_ref, acc_ref, *, nj):
    j = pl.program_id(1)
    acc_ref[j] = _dot(m_ref[...], w_ref[...])

    @pl.when(j == nj - 1)
    def _():
        bb, tl, D = x_ref.shape
        tn = acc_ref.shape[2]
        sl = lambda t: slice(t * tn, (t + 1) * tn)
        s1 = jnp.zeros((bb, tl, 1), F32)
        for t in range(nj):
            v = ALPHA * x_ref[:, :, sl(t)] + g1_ref[:, :, sl(t)] * acc_ref[t].reshape(bb, tl, tn)
            acc_ref[t] = v.reshape(bb * tl, tn)
            s1 = s1 + jnp.sum(v, -1, keepdims=True)
        mu = s1 * (1.0 / D)
        s2 = jnp.zeros((bb, tl, 1), F32)
        for t in range(nj):
            d = acc_ref[t].reshape(bb, tl, tn) - mu
            s2 = s2 + jnp.sum(d * d, -1, keepdims=True)
        rstd = lax.rsqrt(s2 * (1.0 / D) + LN_EPS)
        for t in range(nj):
            x1_ref[:, :, sl(t)] = ((acc_ref[t].reshape(bb, tl, tn) - mu) * rstd * lg_ref[:, :, sl(t)]
                                   + lb_ref[:, :, sl(t)])


def _oproj(m, w_o, x3, mod3, row0, ln_g, ln_b):
    B, L, D = x3.shape
    bb, tl = _row_tiling(B, L, TM_OPROJ)
    nl = L // tl
    rows = bb * tl
    mrow = row0 // bb
    tn = TN_MIX
    nj = D // tn
    in_specs = [
        pl.BlockSpec((rows, D), lambda i, j: (i, 0), pipeline_mode=pl.Buffered(1)),
        pl.BlockSpec((D, tn), lambda i, j: (0, j)),
        pl.BlockSpec((bb, tl, D), lambda i, j: (i // nl, i % nl, 0), pipeline_mode=pl.Buffered(1)),
        pl.BlockSpec((bb, 1, D), lambda i, j: (mrow + i // nl, 0, 2)),
        pl.BlockSpec((1, 1, D), lambda i, j: (0, 0, 0)),
        pl.BlockSpec((1, 1, D), lambda i, j: (0, 0, 0)),
    ]
    args = [m, w_o, x3, mod3, ln_g.reshape(1, 1, D), ln_b.reshape(1, 1, D)]
    return pl.pallas_call(
        functools.partial(_oproj_kernel, nj=nj),
        grid=((B // bb) * nl, nj),
        in_specs=in_specs,
        out_specs=pl.BlockSpec((bb, tl, D), lambda i, j: (i // nl, i % nl, 0)),
        out_shape=jax.ShapeDtypeStruct((B, L, D), F32),
        scratch_shapes=[pltpu.VMEM((nj, rows, tn), F32)],
        compiler_params=_cparams(("parallel", "arbitrary")),
        name="oproj_ln1",
    )(*args)


def _router_kernel(x1_ref, sc2_ref, sh2_ref, w_ref, b_ref, h2_in, id_in, wt_in, h2_ref, id_ref, wt_ref):
    del h2_in, id_in, wt_in
    h2 = (x1_ref[...] * (1.0 + sc2_ref[...]) + sh2_ref[...]).reshape(h2_ref.shape)
    h2_ref[...] = h2
    logits = _mm3(h2, w_ref[...]) + b_ref[...]
    rows = logits.shape[0]
    lane = lax.broadcasted_iota(jnp.int32, (rows, LANE), 1)
    is_g = lane < N_GROUPS
    gl = jnp.where(is_g, logits, -jnp.inf)
    gmax = jnp.max(gl, axis=1, keepdims=True)
    g_sel = jnp.min(jnp.where(gl == gmax, lane, LANE), axis=1, keepdims=True)
    p_grp = 1.0 / jnp.sum(jnp.where(is_g, jnp.exp(gl - gmax), 0.0), axis=1, keepdims=True)
    e_lo = N_GROUPS + g_sel * EXP_PER_GROUP
    in_g = (lane >= e_lo) & (lane < e_lo + EXP_PER_GROUP)
    el = jnp.where(in_g, logits, -jnp.inf)
    m1 = jnp.max(el, axis=1, keepdims=True)
    i1 = jnp.min(jnp.where(el == m1, lane, LANE), axis=1, keepdims=True)
    el2 = jnp.where(lane == i1, -jnp.inf, el)
    m2 = jnp.max(el2, axis=1, keepdims=True)
    i2 = jnp.min(jnp.where(el2 == m2, lane, LANE), axis=1, keepdims=True)
    e21 = jnp.exp(m2 - m1)
    w1 = p_grp / (1.0 + e21)
    w2 = p_grp * e21 / (1.0 + e21)
    id_ref[...] = jnp.where(lane == 0, i1 - N_GROUPS, jnp.where(lane == 1, i2 - N_GROUPS, 0))
    wt_ref[...] = jnp.where(lane == 0, w1, jnp.where(lane == 1, w2, 0.0))


def _router(x1, mod3, row0, w_r, b_r, bufs, t_off):
    B, L, D = x1.shape
    bb, tl = _row_tiling(B, L, TM_MIX)
    nl = L // tl
    rows = bb * tl
    mrow = row0 // bb
    assert t_off % rows == 0
    boff = t_off // rows
    t_all = bufs[0].shape[0]
    modspec = lambda c: pl.BlockSpec((bb, 1, D), lambda i: (mrow + i // nl, 0, c))
    anyspec = pl.BlockSpec(memory_space=pl.ANY)
    return pl.pallas_call(
        _router_kernel,
        grid=((B // bb) * nl,),
        in_specs=[
            pl.BlockSpec((bb, tl, D), lambda i: (i // nl, i % nl, 0)),
            modspec(4), modspec(3),
            pl.BlockSpec((D, LANE), lambda i: (0, 0)),
            pl.BlockSpec((1, LANE), lambda i: (0, 0)),
            anyspec, anyspec, anyspec,
        ],
        out_specs=[
            pl.BlockSpec((rows, D), lambda i: (boff + i, 0)),
            pl.BlockSpec((rows, LANE), lambda i: (boff + i, 0)),
            pl.BlockSpec((rows, LANE), lambda i: (boff + i, 0)),
        ],
        out_shape=[
            jax.ShapeDtypeStruct((t_all, D), F32),
            jax.ShapeDtypeStruct((t_all, LANE), jnp.int32),
            jax.ShapeDtypeStruct((t_all, LANE), F32),
        ],
        input_output_aliases={5: 0, 6: 1, 7: 2},
        compiler_params=_cparams(("parallel",)),
        name="router",
    )(x1, mod3, mod3, w_r, b_r, *bufs)


def _expert_kernel(blk_e_ref, tok_ref, nused_ref, h_hbm, wg_ref, wu_ref, wd_ref, o_ref,
                   xbuf_ref, xb16_ref, act_ref, sem_ref, *, BM, nfc):
    i = pl.program_id(0)
    c = pl.program_id(1)
    nused = nused_ref[0]
    slot = lax.rem(i, 2)

    def start_gather(blk, sl):
        def body(r, carry):
            tok = tok_ref[blk * BM + r]
            pltpu.make_async_copy(h_hbm.at[pl.ds(tok, 1)], xbuf_ref.at[sl, pl.ds(r, 1)], sem_ref.at[sl]).start()
            return carry
        lax.fori_loop(0, BM, body, 0, unroll=8)

    def wait_gather(sl):
        pltpu.make_async_copy(xbuf_ref.at[sl], xbuf_ref.at[sl], sem_ref.at[sl]).wait()

    @pl.when(jnp.logical_and(c == 0, i < nused))
    def _():
        @pl.when(i == 0)
        def _():
            start_gather(0, 0)

        @pl.when(i + 1 < nused)
        def _():
            start_gather(i + 1, 1 - slot)

        wait_gather(slot)
        xb16_ref[...] = xbuf_ref[slot].astype(BF16)

    @pl.when(i < nused)
    def _():
        x = xb16_ref[...]
        gate = _dot(x, wg_ref[0])
        up = _dot(x, wu_ref[0])
        act_ref[c] = (_silu(gate) * up).astype(BF16)

        @pl.when(c == nfc - 1)
        def _():
            act = jnp.concatenate([act_ref[t] for t in range(nfc)], axis=-1)
            o_ref[...] = _dot(act, wd_ref[0])

    @pl.when(jnp.logical_and(i >= nused, c == nfc - 1))
    def _():
        o_ref[...] = jnp.zeros_like(o_ref)


def _experts(h2, blk_e, slot_tok, nused, w_up16, w_down16, BM):
    T, D = h2.shape
    P = slot_tok.shape[0]
    nblk = P // BM
    fc = FC_MOE
    nfc = D_EXPERT // fc
    kern = functools.partial(_expert_kernel, BM=BM, nfc=nfc)

    def eidx(i, nu):
        return jnp.minimum(i, nu[0] - 1)

    def cidx(i, c, nu):
        return jnp.where(i < nu[0], c, nfc - 1)

    grid_spec = pltpu.PrefetchScalarGridSpec(
        num_scalar_prefetch=3,
        grid=(nblk, nfc),
        in_specs=[
            pl.BlockSpec(memory_space=pl.ANY),
            pl.BlockSpec((1, D, fc), lambda i, c, be, tk, nu: (be[eidx(i, nu)], 0, cidx(i, c, nu))),
            pl.BlockSpec((1, D, fc), lambda i, c, be, tk, nu: (be[eidx(i, nu)], 0, nfc + cidx(i, c, nu))),
            pl.BlockSpec((1, D_EXPERT, D), lambda i, c, be, tk, nu: (be[eidx(i, nu)], 0, 0)),
        ],
        out_specs=pl.BlockSpec((BM, D), lambda i, c, be, tk, nu: (i, 0)),
        scratch_shapes=[
            pltpu.VMEM((2, BM, D), F32),
            pltpu.VMEM((BM, D), BF16),
            pltpu.VMEM((nfc, BM, fc), BF16),
            pltpu.SemaphoreType.DMA((2,)),
        ],
    )
    return pl.pallas_call(
        kern,
        grid_spec=grid_spec,
        out_shape=jax.ShapeDtypeStruct((P, D), F32),
        compiler_params=_cparams(("arbitrary", "arbitrary")),
        name="experts",
    )(blk_e, slot_tok, nused, h2, w_up16, w_up16, w_down16)


def _route_metadata(ids, T, BM):
    expert = ids[:, 0:2].reshape(-1)
    A = 2 * T
    onehot = (expert[:, None] == jnp.arange(N_EXPERTS, dtype=jnp.int32)[None, :]).astype(jnp.int32)
    csum = jnp.cumsum(onehot, axis=0)
    counts = csum[-1]
    rank = jnp.sum((csum - onehot) * onehot, axis=1)
    padded = ((counts + BM - 1) // BM) * BM
    pend = jnp.cumsum(padded)
    pstart = pend - padded
    dest = (pstart[expert] + rank).astype(jnp.int32)
    P = ((A + BM - 1) // BM) * BM + N_EXPERTS * BM
    nblk = P // BM
    tok = jnp.arange(A, dtype=jnp.int32) // 2
    slot_tok = jnp.zeros((P,), jnp.int32).at[dest].set(tok)
    starts = jnp.arange(nblk, dtype=jnp.int32) * BM
    blk_e = jnp.minimum(jnp.sum((pend[None, :] <= starts[:, None]).astype(jnp.int32), axis=1), N_EXPERTS - 1)
    nused = (pend[-1] // BM).astype(jnp.int32).reshape(1)
    return dest, slot_tok, blk_e, nused


def _final_kernel(dest_ref, yb_hbm, x1_ref, g2_ref, wt_ref, lg_ref, lb_ref, o_ref, buf_ref, sem_ref, *, rows, ntiles):
    i = pl.program_id(0)
    slot = lax.rem(i, 2)

    def start_gather(tile, sl):
        def body(r, carry):
            d = dest_ref[tile * 2 * rows + r]
            pltpu.make_async_copy(yb_hbm.at[pl.ds(d, 1)], buf_ref.at[sl, pl.ds(r, 1)], sem_ref.at[sl]).start()
            return carry
        lax.fori_loop(0, 2 * rows, body, 0, unroll=8)

    def wait_gather(sl):
        pltpu.make_async_copy(buf_ref.at[sl], buf_ref.at[sl], sem_ref.at[sl]).wait()

    @pl.when(i == 0)
    def _():
        start_gather(0, 0)

    @pl.when(i + 1 < ntiles)
    def _():
        start_gather(i + 1, 1 - slot)

    wait_gather(slot)
    bb, tl, D = x1_ref.shape
    tn = TN_MIX
    sl = lambda t: slice(t * tn, (t + 1) * tn)
    w0 = wt_ref[:, 0:1]
    w1 = wt_ref[:, 1:2]
    s1 = jnp.zeros((bb, tl, 1), F32)
    for t in range(D // tn):
        f = w0 * buf_ref[slot, 0:rows, sl(t)] + w1 * buf_ref[slot, rows:2 * rows, sl(t)]
        v = ALPHA * x1_ref[:, :, sl(t)] + g2_ref[:, :, sl(t)] * f.reshape(bb, tl, tn)
        o_ref[:, :, sl(t)] = v
        s1 = s1 + jnp.sum(v, -1, keepdims=True)
    mu = s1 * (1.0 / D)
    s2 = jnp.zeros((bb, tl, 1), F32)
    for t in range(D // tn):
        d = o_ref[:, :, sl(t)] - mu
        s2 = s2 + jnp.sum(d * d, -1, keepdims=True)
    rstd = lax.rsqrt(s2 * (1.0 / D) + LN_EPS)
    for t in range(D // tn):
        o_ref[:, :, sl(t)] = (o_ref[:, :, sl(t)] - mu) * rstd * lg_ref[:, :, sl(t)] + lb_ref[:, :, sl(t)]


def _final(dest, yb, x1, mod3, row0, wts, t_off, ln_g, ln_b):
    B, L, D = x1.shape
    T = B * L
    bb, tl = _row_tiling(B, L, TM_MIX)
    nl = L // tl
    rows = bb * tl
    mrow = row0 // bb
    ntiles = (B // bb) * nl
    woff = t_off // rows
    dest_tiles = dest[2 * t_off:2 * (t_off + T)].reshape(T // rows, rows, 2).transpose(0, 2, 1).reshape(-1)
    kern = functools.partial(_final_kernel, rows=rows, ntiles=ntiles)
    grid_spec = pltpu.PrefetchScalarGridSpec(
        num_scalar_prefetch=1,
        grid=(ntiles,),
        in_specs=[
            pl.BlockSpec(memory_space=pl.ANY),
            pl.BlockSpec((bb, tl, D), lambda i, d: (i // nl, i % nl, 0)),
            pl.BlockSpec((bb, 1, D), lambda i, d: (mrow + i // nl, 0, 5)),
            pl.BlockSpec((rows, LANE), lambda i, d: (woff + i, 0)),
            pl.BlockSpec((1, 1, D), lambda i, d: (0, 0, 0)),
            pl.BlockSpec((1, 1, D), lambda i, d: (0, 0, 0)),
        ],
        out_specs=pl.BlockSpec((bb, tl, D), lambda i, d: (i // nl, i % nl, 0)),
        scratch_shapes=[
            pltpu.VMEM((2, 2 * rows, D), F32),
            pltpu.SemaphoreType.DMA((2,)),
        ],
    )
    return pl.pallas_call(
        kern,
        grid_spec=grid_spec,
        out_shape=jax.ShapeDtypeStruct((B, L, D), F32),
        compiler_params=_cparams(("arbitrary",)),
        name="combine_ln2",
    )(dest_tiles, yb, x1, mod3, wts, ln_g.reshape(1, 1, D), ln_b.reshape(1, 1, D))


def _pad_keys(t, S_pad):
    S = t.shape[1]
    if S == S_pad:
        return t
    return jnp.pad(t, ((0, 0), (0, S_pad - S), (0, 0)))


def _mixers(x3, mod3, row0, conv_state, gdn_state, past_k, past_v, past_ik, wp):
    B, L, D = x3.shape
    pos0 = past_k.shape[1]
    S = pos0 + L
    proj = _inproj(x3, mod3, row0, wp["w_in"])

    oa, new_gdn = _gdn(proj, B, L, conv_state, gdn_state, wp["conv_w"], wp["a_log"], wp["dt_bias"],
                       wp["gdn_norm_w"])
    new_conv = proj.reshape(B, L, NP)[:, L - (CONV_W - 1):, C_QKV:C_QKV + GDN_CONV_CH]

    qb = min(Q_BLOCK, L)
    q_hm, kf, k16, vf, v16, iq_hm, ikf, ik16 = _rope(proj, B, L, pos0, qb)
    S_pad = ((S + TK_ATT - 1) // TK_ATT) * TK_ATT
    k_all = _pad_keys(jnp.concatenate([past_k.reshape(B, pos0, ATT_KV).astype(BF16), k16.reshape(B, L, ATT_KV)], 1), S_pad)
    v_all = _pad_keys(jnp.concatenate([past_v.reshape(B, pos0, ATT_KV).astype(BF16), v16.reshape(B, L, ATT_KV)], 1), S_pad)
    ik_all = _pad_keys(jnp.concatenate([past_ik.astype(BF16), ik16.reshape(B, L, IDX_HD)], 1), S_pad)
    ob = _dsa(iq_hm, proj, q_hm, ik_all, k_all, v_all, B, L, S, pos0)

    m = _merge(oa, ob, wp["w_br_a"], wp["w_br_b"], proj)
    x1 = _oproj(m, wp["w_o"], x3, mod3, row0, wp["ln1_g"], wp["ln1_b"])

    kb_out = kf.reshape(B, L, ATT_KV_HEADS, ATT_HD)
    vb_out = vf.reshape(B, L, ATT_KV_HEADS, ATT_HD)
    ik_out = ikf.reshape(B, L, IDX_HD)
    return x1, (new_conv, new_gdn, kb_out, vb_out, ik_out)


WP_TN = 512
WP_TR = 1024
WP_NSRC = WP_TN // LANE + 1


def _w_in_source_columns():
    sizes = (GDN_CONV_CH, GDN_HEADS, GDN_HEADS, GDN_VAL, ATT_Q, ATT_KV, ATT_KV, IDX_HEADS * IDX_HD, IDX_HD,
             IDX_HEADS, 2 * D_MODEL)
    offs, off = [], 0
    for s in sizes:
        offs.append(off)
        off += s
    o_qkv, o_a, o_b, o_z, o_q, o_k, o_v, o_iq, o_ik, o_iw, o_g = offs
    srcs = []
    for dst, src, width in ((C_QKV, o_qkv, GDN_CONV_CH), (C_Z, o_z, GDN_VAL), (C_IQ, o_iq, IDX_HEADS * IDX_HD),
                            (C_Q, o_q, ATT_Q), (C_K, o_k, ATT_KV), (C_V, o_v, ATT_KV), (C_IK, -1, WP_TN),
                            (C_GA, o_g, 2 * D_MODEL)):
        assert dst == len(srcs) * WP_TN and width % WP_TN == 0
        srcs += [src + t * WP_TN if src >= 0 else -1 for t in range(width // WP_TN)]
    assert len(srcs) * WP_TN == NP
    return srcs, (o_a, o_b, o_ik, o_iw)


def _narrow_pieces():
    o_a, o_b, o_ik, o_iw = _W_IN_NARROW
    return ((o_ik, IDX_HD), (o_a, GDN_HEADS), (o_b, GDN_HEADS), (o_iw, IDX_HEADS))


def _narrow_blocks():
    blks = set()
    for o, w in _narrow_pieces():
        blks.update(range(o // LANE, (o + w - 1) // LANE + 1))
    return sorted(blks)


def _wperm_kernel(base_ref, shift_ref, *refs):
    src_refs = refs[:WP_NSRC]
    nb = dict(zip(_narrow_blocks(), refs[WP_NSRC:-1]))
    o_ref = refs[-1]
    shift = shift_ref[pl.program_id(0)]

    @pl.when(shift < 0)
    def _():
        def piece(o, w):
            blk, ln = o // LANE, o % LANE
            if ln + w <= LANE:
                return nb[blk][:, ln:ln + w]
            return jnp.concatenate([nb[blk][:, ln:], nb[blk + 1][:, :ln + w - LANE]], axis=1)

        parts = [piece(o, w) for o, w in _narrow_pieces()]
        used = sum(w for _, w in _narrow_pieces())
        parts.append(jnp.zeros((o_ref.shape[0], WP_TN - used), o_ref.dtype))
        o_ref[...] = jnp.concatenate(parts, axis=1)

    for sh in sorted(set(s for s in _W_IN_SHIFTS if s >= 0)):
        @pl.when(shift == sh)
        def _(sh=sh):
            a = jnp.concatenate([r[...] for r in src_refs], axis=1)
            o_ref[...] = a[:, sh:sh + WP_TN]


_W_IN_SRCS, _W_IN_NARROW = _w_in_source_columns()
_W_IN_SHIFTS = [s % LANE if s >= 0 else -1 for s in _W_IN_SRCS]


def _permute_w_in(w_in):
    D, ncols = w_in.shape
    base = jnp.asarray([max(s, 0) // LANE for s in _W_IN_SRCS], jnp.int32)
    shift = jnp.asarray(_W_IN_SHIFTS, jnp.int32)
    last = (ncols - 1) // LANE

    def src_spec(t):
        return pl.BlockSpec((WP_TR, LANE), lambda j, r, base, shift: (r, jnp.minimum(base[j] + t, last)))

    def narrow_spec(blk):
        return pl.BlockSpec((WP_TR, LANE), lambda j, r, base, shift: (r, blk))

    nblks = _narrow_blocks()
    grid_spec = pltpu.PrefetchScalarGridSpec(
        num_scalar_prefetch=2,
        grid=(NP // WP_TN, D // WP_TR),
        in_specs=[src_spec(t) for t in range(WP_NSRC)] + [narrow_spec(blk) for blk in nblks],
        out_specs=pl.BlockSpec((WP_TR, WP_TN), lambda j, r, base, shift: (r, j)),
    )
    return pl.pallas_call(
        _wperm_kernel,
        grid_spec=grid_spec,
        out_shape=jax.ShapeDtypeStruct((D, NP), BF16),
        compiler_params=_cparams(("parallel", "parallel")),
        name="w_in_permute",
    )(base, shift, *([w_in.astype(BF16)] * (WP_NSRC + len(nblks))))


def kernel(x_prompt, x_sample, c_prompt, c_sample, state_conv, state_gdn, cache_k, cache_v, cache_idx_k,
           w_ada, b_ada, w_in, conv_w, a_log, dt_bias, gdn_norm_w, w_br_a, w_br_b, w_o,
           ln1_g, ln1_b, ln2_g, ln2_b, w_grp, b_grp, w_rtr, b_rtr, w_up, w_down):
    Bp, Lp, D = x_prompt.shape
    Bs, Ls, _ = x_sample.shape
    l = 0
    row0_p = Bs
    nrows = ((Bs + Bp + 7) // 8) * 8
    c_all = jnp.concatenate([c_sample, c_prompt, jnp.zeros((nrows - Bs - Bp, D), F32)], 0)
    mod = _ada(c_all, w_ada[l], b_ada[l])
    mod3 = mod.reshape(nrows, 1, 6 * D)

    nr = LANE - N_GROUPS - N_EXPERTS
    wp = dict(
        w_in=_permute_w_in(w_in[l]),
        conv_w=conv_w[l], a_log=a_log[l], dt_bias=dt_bias[l], gdn_norm_w=gdn_norm_w[l],
        w_br_a=w_br_a[l].astype(BF16), w_br_b=w_br_b[l].astype(BF16), w_o=w_o[l].astype(BF16),
        ln1_g=ln1_g[l], ln1_b=ln1_b[l], ln2_g=ln2_g[l], ln2_b=ln2_b[l],
        w_r=jnp.concatenate([w_grp[l], w_rtr[l], jnp.zeros((D, nr), F32)], 1),
        b_r=jnp.concatenate([b_grp[l], b_rtr[l], jnp.zeros((nr,), F32)]).reshape(1, LANE),
        w_up=w_up[l].astype(BF16), w_down=w_down[l].astype(BF16),
    )

    zc = jnp.zeros((Bp, CONV_W - 1, GDN_CONV_CH), F32)
    zs = jnp.zeros((Bp, GDN_HEADS, GDN_DK, GDN_DV), F32)
    zk = jnp.zeros((Bp, 0, ATT_KV_HEADS, ATT_HD), F32)
    zik = jnp.zeros((Bp, 0, IDX_HD), F32)
    Tp, Ts = Bp * Lp, Bs * Ls
    t_all = Tp + Ts
    x1p, (c1, g1, k1, v1, i1) = _mixers(x_prompt, mod3, row0_p, zc, zs, zk, zk, zik, wp)
    x1s, (c2, g2, k2, v2, i2) = _mixers(x_sample, mod3, 0, state_conv[l], state_gdn[l], cache_k[l],
                                        cache_v[l], cache_idx_k[l], wp)

    bufs = (jnp.zeros((t_all, D), F32), jnp.zeros((t_all, LANE), jnp.int32), jnp.zeros((t_all, LANE), F32))
    bufs = _router(x1p, mod3, row0_p, wp["w_r"], wp["b_r"], bufs, 0)
    h2_all, ids, wts = _router(x1s, mod3, 0, wp["w_r"], wp["b_r"], bufs, Tp)
    dest, slot_tok, blk_e, nused = _route_metadata(ids, t_all, MOE_BM)
    yb = _experts(h2_all, blk_e, slot_tok, nused, wp["w_up"], wp["w_down"], MOE_BM)
    yp = _final(dest, yb, x1p, mod3, row0_p, wts, 0, wp["ln2_g"], wp["ln2_b"])
    ys = _final(dest, yb, x1s, mod3, 0, wts, Tp, wp["ln2_g"], wp["ln2_b"])
    st = lambda t: t[None]
    return (yp, ys, st(c1), st(g1), st(k1), st(v1), st(i1), st(c2), st(g2), st(k2), st(v2), st(i2))
```
